```python
import jax, jax.numpy as jnp
from jax import lax
import numpy as np

D_MODEL = 2048
BATCH = 1
SEQ = 16384
DEPTH = 1
DEC_BATCH = 128
DEC_SEQ = 4
PAST_LEN = 16384
PAGE_SIZE = 128

MIX_WIDTH = D_MODEL
HEAD_DIM = 64
RWKV_WIDTH = MIX_WIDTH // 2
RWKV_HEADS = RWKV_WIDTH // HEAD_DIM
SWA_WIDTH = MIX_WIDTH - RWKV_WIDTH
SWA_HEADS = SWA_WIDTH // HEAD_DIM
SWA_KV_HEADS = 4
SWA_GROUP = SWA_HEADS // SWA_KV_HEADS
KV_WIDTH = SWA_KV_HEADS * HEAD_DIM
WINDOW = 128
BLOCK = 128
DECAY_LORA = 64
AAA_LORA = 64
GATE_LORA = 160
RWKV_COLS = 3 * RWKV_WIDTH + DECAY_LORA + AAA_LORA + GATE_LORA
SWA_COLS = SWA_WIDTH + 2 * KV_WIDTH
IN_COLS = RWKV_COLS + SWA_COLS
D_FF = ((8 * D_MODEL + 3 * 256 - 1) // (3 * 256)) * 256
RMS_EPS = 1e-5
GN_EPS = 64e-5

kernel_name = 'hymba_rwkv7_swa_sink_decode_step'


def rmsnorm(x, g):
    xf = x.astype(jnp.float32)
    y = xf * lax.rsqrt(jnp.mean(xf * xf, axis=-1, keepdims=True) + RMS_EPS)
    return (y * g.astype(jnp.float32)).astype(x.dtype)


def alibi_slopes():
    return jnp.asarray([2.0 ** (-8.0 * (i + 1) / SWA_HEADS) for i in range(SWA_HEADS)], dtype=jnp.float32)


def rwkv7_time_mix(p, shift_prev, state0, mu, w0, w2, a0, a2, g2, k_k, k_a, r_k, ln_w, ln_b):
    B, T, _ = p.shape
    f32 = jnp.float32
    C, H, N = RWKV_WIDTH, RWKV_HEADS, HEAD_DIM
    pf = p.astype(f32)
    prev = jnp.concatenate([shift_prev.astype(f32)[:, None], pf[:, :-1]], axis=1)
    xs = pf + mu.astype(f32) * (prev - pf)
    r = xs[..., :C]
    k = xs[..., C:2 * C]
    v = xs[..., 2 * C:3 * C]
    o = 3 * C
    xw = xs[..., o:o + DECAY_LORA]
    o += DECAY_LORA
    xa = xs[..., o:o + AAA_LORA]
    o += AAA_LORA
    xg = xs[..., o:o + GATE_LORA]
    w = -jax.nn.softplus(-(w0.astype(f32) + jnp.tanh(xw) @ w2.astype(f32))) - 0.5
    decay = jnp.exp(-jnp.exp(w))
    a = jax.nn.sigmoid(a0.astype(f32) + xa @ a2.astype(f32))
    g = jax.nn.sigmoid(xg) @ g2.astype(f32)
    kk = (k * k_k.astype(f32)).reshape(B, T, H, N)
    kk = kk / jnp.maximum(jnp.sqrt(jnp.sum(kk * kk, axis=-1, keepdims=True)), 1e-12)
    k = k * (1.0 + (a - 1.0) * k_a.astype(f32))
    r_h = r.reshape(B, T, H, N)
    k_h = k.reshape(B, T, H, N)
    v_h = v.reshape(B, T, H, N)
    w_h = decay.reshape(B, T, H, N)
    b_h = kk * a.reshape(B, T, H, N)

    def step(S, inp):
        r_t, w_t, k_t, v_t, kk_t, b_t = inp
        sa = jnp.einsum('bhij,bhj->bhi', S, -kk_t)
        S = S * w_t[:, :, None, :] + sa[..., None] * b_t[:, :, None, :] + v_t[..., None] * k_t[:, :, None, :]
        return S, jnp.einsum('bhij,bhj->bhi', S, r_t)

    seq = (jnp.moveaxis(r_h, 1, 0), jnp.moveaxis(w_h, 1, 0), jnp.moveaxis(k_h, 1, 0),
           jnp.moveaxis(v_h, 1, 0), jnp.moveaxis(kk, 1, 0), jnp.moveaxis(b_h, 1, 0))
    S_fin, out = lax.scan(step, state0.astype(f32), seq)
    out = jnp.moveaxis(out, 0, 1)
    mean = jnp.mean(out, axis=-1, keepdims=True)
    var = jnp.mean(jnp.square(out - mean), axis=-1, keepdims=True)
    out = ((out - mean) * lax.rsqrt(var + GN_EPS)).reshape(B, T, C) * ln_w.astype(f32) + ln_b.astype(f32)
    bonus = (jnp.sum(r_h * k_h * r_k.astype(f32), axis=-1, keepdims=True) * v_h).reshape(B, T, C)
    return (out + bonus) * g, S_fin, p[:, -1]


def swa_sink_attention(q, k, v, k_past, v_past, p0, sinks):
    B, T = q.shape[0], q.shape[1]
    qb = min(BLOCK, T)
    nblk = T // qb
    span = WINDOW + qb
    k_cat = jnp.concatenate([k_past.astype(k.dtype), k], axis=1)
    v_cat = jnp.concatenate([v_past.astype(v.dtype), v], axis=1)
    idx = (np.arange(nblk) * qb)[:, None] + np.arange(span)[None, :]
    kb = k_cat[:, idx]
    vb = v_cat[:, idx]
    qr = q.reshape(B, nblk, qb, SWA_KV_HEADS, SWA_GROUP, HEAD_DIM)
    s = jnp.einsum('bnqkgd,bnskd->bnkgqs', qr, kb).astype(jnp.float32) * (HEAD_DIM ** -0.5)
    t_pos = p0 + np.arange(T).reshape(nblk, qb)
    s_pos = p0 - WINDOW + idx
    dist = t_pos[:, :, None] - s_pos[:, None, :]
    valid = (s_pos[:, None, :] >= 0) & (dist >= 0) & (dist < WINDOW)
    slopes = alibi_slopes().reshape(SWA_KV_HEADS, SWA_GROUP)
    s = s - slopes[None, None, :, :, None, None] * jnp.asarray(dist, jnp.float32)[None, :, None, None, :, :]
    s = jnp.where(jnp.asarray(valid)[None, :, None, None, :, :], s, -jnp.inf)
    sink = sinks.astype(jnp.float32).reshape(SWA_KV_HEADS, SWA_GROUP)[None, None, :, :, None, None]
    m = jnp.maximum(jnp.max(s, axis=-1, keepdims=True), sink)
    e = jnp.exp(s - m)
    pr = e / (jnp.sum(e, axis=-1, keepdims=True) + jnp.exp(sink - m))
    o = jnp.einsum('bnkgqs,bnskd->bnqkgd', pr.astype(v.dtype), vb).reshape(B, T, SWA_WIDTH)
    return o, k_cat[:, -WINDOW:], v_cat[:, -WINDOW:]


def hybrid_layer(x, shift_prev, rwkv_state, k_past, v_past, p0, norm_attn, w_in, mu_shift, w0, w2, a0, a2,
                 g2, k_k, k_a, r_k, ln_x_w, ln_x_b, sinks, w_out, norm_ffn, w_gate, w_up, w_down):
    B, T, _ = x.shape
    h = rmsnorm(x, norm_attn)
    proj = h @ w_in
    p_rwkv = proj[..., :RWKV_COLS]
    p_swa = proj[..., RWKV_COLS:]
    q = p_swa[..., :SWA_WIDTH].reshape(B, T, SWA_HEADS, HEAD_DIM)
    k = p_swa[..., SWA_WIDTH:SWA_WIDTH + KV_WIDTH].reshape(B, T, SWA_KV_HEADS, HEAD_DIM)
    v = p_swa[..., SWA_WIDTH + KV_WIDTH:].reshape(B, T, SWA_KV_HEADS, HEAD_DIM)
    o_rwkv, new_state, new_shift = rwkv7_time_mix(p_rwkv, shift_prev, rwkv_state, mu_shift, w0, w2, a0, a2,
                                                  g2, k_k, k_a, r_k, ln_x_w, ln_x_b)
    o_swa, new_k, new_v = swa_sink_attention(q, k, v, k_past, v_past, p0, sinks)
    mixed = jnp.concatenate([o_rwkv.astype(x.dtype), o_swa.astype(x.dtype)], axis=-1)
    x = x + mixed @ w_out
    h2 = rmsnorm(x, norm_ffn)
    x = x + (jax.nn.silu(h2 @ w_gate) * (h2 @ w_up)) @ w_down
    return x, new_state, new_shift, new_k, new_v


def setup_inputs(seed: int = 0) -> dict:
    key = jax.random.key(seed)
    ks = jax.random.split(key, 32)
    nrm = jax.random.normal
    f32 = jnp.float32
    L, C = DEPTH, RWKV_WIDTH
    return {
        'x_prompt': nrm(ks[0], (BATCH, SEQ, D_MODEL), f32),
        'x_sample': nrm(ks[1], (DEC_BATCH, DEC_SEQ, D_MODEL), f32),
        'state_rwkv': 0.2 * nrm(ks[2], (L, DEC_BATCH, RWKV_HEADS, HEAD_DIM, HEAD_DIM), f32),
        'state_shift': nrm(ks[3], (L, DEC_BATCH, RWKV_COLS), f32),
        'cache_swa_k': nrm(ks[4], (L, DEC_BATCH, WINDOW, SWA_KV_HEADS, HEAD_DIM), f32),
        'cache_swa_v': nrm(ks[5], (L, DEC_BATCH, WINDOW, SWA_KV_HEADS, HEAD_DIM), f32),
        'norm_attn': 1.0 + 0.02 * nrm(ks[6], (L, D_MODEL), f32),
        'w_in': nrm(ks[7], (L, D_MODEL, IN_COLS), f32) * D_MODEL ** -0.5,
        'mu_shift': jax.random.uniform(ks[8], (L, RWKV_COLS), f32),
        'w0': jax.random.uniform(ks[9], (L, C), f32, -6.0, -1.0),
        'w2': 0.5 * nrm(ks[10], (L, DECAY_LORA, C), f32) * DECAY_LORA ** -0.5,
        'a0': 0.1 * nrm(ks[11], (L, C), f32),
        'a2': nrm(ks[12], (L, AAA_LORA, C), f32) * AAA_LORA ** -0.5,
        'g2': nrm(ks[13], (L, GATE_LORA, C), f32) * GATE_LORA ** -0.5,
        'k_k': 1.0 + 0.1 * nrm(ks[14], (L, C), f32),
        'k_a': 1.0 + 0.1 * nrm(ks[15], (L, C), f32),
        'r_k': 0.1 * nrm(ks[16], (L, RWKV_HEADS, HEAD_DIM), f32),
        'ln_x_w': 1.0 + 0.02 * nrm(ks[17], (L, C), f32),
        'ln_x_b': 0.02 * nrm(ks[18], (L, C), f32),
        'sinks': 0.5 * nrm(ks[19], (L, SWA_HEADS), f32),
        'w_out': nrm(ks[20], (L, MIX_WIDTH, D_MODEL), f32) * MIX_WIDTH ** -0.5,
        'norm_ffn': 1.0 + 0.02 * nrm(ks[21], (L, D_MODEL), f32),
        'w_gate': nrm(ks[22], (L, D_MODEL, D_FF), f32) * D_MODEL ** -0.5,
        'w_up': nrm(ks[23], (L, D_MODEL, D_FF), f32) * D_MODEL ** -0.5,
        'w_down': nrm(ks[24], (L, D_FF, D_MODEL), f32) * D_FF ** -0.5,
        'norm_final': 1.0 + 0.02 * nrm(ks[25], (D_MODEL,), f32),
    }


def reference(x_prompt, x_sample, state_rwkv, state_shift, cache_swa_k, cache_swa_v, norm_attn, w_in,
              mu_shift, w0, w2, a0, a2, g2, k_k, k_a, r_k, ln_x_w, ln_x_b, sinks, w_out, norm_ffn,
              w_gate, w_up, w_down, norm_final):
    B = x_prompt.shape[0]
    yp, ys = x_prompt, x_sample
    p_state, p_shift, p_k, p_v = [], [], [], []
    s_state, s_shift, s_k, s_v = [], [], [], []
    for l in range(DEPTH):
        lw = (norm_attn[l], w_in[l], mu_shift[l], w0[l], w2[l], a0[l], a2[l], g2[l], k_k[l], k_a[l], r_k[l],
              ln_x_w[l], ln_x_b[l], sinks[l], w_out[l], norm_ffn[l], w_gate[l], w_up[l], w_down[l])
        zero_kv = jnp.zeros((B, WINDOW, SWA_KV_HEADS, HEAD_DIM), x_prompt.dtype)
        yp, st, sh, nk, nv = hybrid_layer(
            yp, jnp.zeros((B, RWKV_COLS), x_prompt.dtype),
            jnp.zeros((B, RWKV_HEADS, HEAD_DIM, HEAD_DIM), jnp.float32), zero_kv, zero_kv, 0, *lw)
        p_state.append(st); p_shift.append(sh); p_k.append(nk); p_v.append(nv)
        ys, st, sh, nk, nv = hybrid_layer(
            ys, state_shift[l], state_rwkv[l], cache_swa_k[l], cache_swa_v[l], PAST_LEN, *lw)
        s_state.append(st); s_shift.append(sh); s_k.append(nk); s_v.append(nv)
    y_prompt = rmsnorm(yp, norm_final)
    y_sample = rmsnorm(ys, norm_final)
    return (y_prompt, y_sample,
            jnp.stack(p_state), jnp.stack(p_shift), jnp.stack(p_k), jnp.stack(p_v),
            jnp.stack(s_state), jnp.stack(s_shift), jnp.stack(s_k), jnp.stack(s_v))
```

```python
import functools

import numpy as np
import jax
import jax.numpy as jnp
from jax import lax
from jax.experimental import pallas as pl
from jax.experimental.pallas import tpu as pltpu

F32 = jnp.float32
BF16 = jnp.bfloat16

D_MODEL = 2048
HEAD_DIM = 64
RWKV_WIDTH = 1024
RWKV_HEADS = 16
SWA_WIDTH = 1024
SWA_HEADS = 16
SWA_KV_HEADS = 4
SWA_GROUP = 4
KV_WIDTH = 256
WINDOW = 128
BLOCK = 128
DECAY_LORA = 64
AAA_LORA = 64
GATE_LORA = 160
LORA_COLS = DECAY_LORA + AAA_LORA + GATE_LORA
RWKV_COLS = 3 * RWKV_WIDTH + LORA_COLS
D_FF = 5632
RMS_EPS = 1e-5
GN_EPS = 64e-5

SUBLANES = 8
LANES = 128
I_LO = LANES // RWKV_HEADS


def _key_perm():
    p = np.arange(RWKV_WIDTH)
    return (p % RWKV_HEADS) * HEAD_DIM + p // RWKV_HEADS


def _val_perm():
    p = np.arange(RWKV_WIDTH)
    return ((p % LANES) // I_LO) * HEAD_DIM + (p // LANES) * I_LO + p % I_LO


KEY_PERM = _key_perm()
VAL_PERM = _val_perm()
KEY_INV = np.argsort(KEY_PERM)
VAL_INV = np.argsort(VAL_PERM)


def _scan_kernel(a_ref, w_ref, b_ref, k_ref, r_ref, v_ref, s0_ref, o_ref, sT_ref, s_scr, q_scr, *, tb):
    ti = pl.program_id(1)

    @pl.when(ti == 0)
    def _():
        s_scr[...] = s0_ref[0]

    gather_idx = (lax.broadcasted_iota(jnp.int32, (SUBLANES, LANES), 0) * RWKV_HEADS
                  + lax.broadcasted_iota(jnp.int32, (SUBLANES, LANES), 1) // I_LO)

    def build(t, carry):
        for qi, ref in enumerate((a_ref, w_ref, b_ref, k_ref, r_ref)):
            for jg in range(HEAD_DIM // SUBLANES):
                src = jnp.broadcast_to(ref[0, t, pl.ds(jg, 1), :], (SUBLANES, LANES))
                q_scr[t, qi, jg * SUBLANES:(jg + 1) * SUBLANES, :] = jnp.take_along_axis(src, gather_idx, axis=1)
        return carry

    lax.fori_loop(0, tb, build, 0)

    def row(t, qi, j):
        return jnp.broadcast_to(q_scr[t, qi, pl.ds(j, 1), :], (SUBLANES, LANES))

    def tree_sum(xs):
        while len(xs) > 1:
            xs = [xs[i] + xs[i + 1] for i in range(0, len(xs) - 1, 2)] + ([xs[-1]] if len(xs) % 2 else [])
        return xs[0]

    def step(t, carry):
        v = v_ref[0, t]
        sa = tree_sum([s_scr[j] * row(t, 0, j) for j in range(HEAD_DIM)])
        outs = []
        for j in range(HEAD_DIM):
            sj = s_scr[j] * row(t, 1, j) + sa * row(t, 2, j) + v * row(t, 3, j)
            s_scr[j] = sj
            outs.append(sj * row(t, 4, j))
        o_ref[0, t] = tree_sum(outs)
        return carry

    lax.fori_loop(0, tb, step, 0)

    @pl.when(ti == pl.num_programs(1) - 1)
    def _():
        sT_ref[0] = s_scr[...]


def _rwkv_scan(a, w, b, k, r, v, s0, *, tb):
    bsz, t_len = a.shape[0], a.shape[1]
    assert t_len % tb == 0
    tok_spec = pl.BlockSpec((1, tb, SUBLANES, LANES), lambda bi, ti: (bi, ti, 0, 0))
    st_spec = pl.BlockSpec((1, HEAD_DIM, SUBLANES, LANES), lambda bi, ti: (bi, 0, 0, 0))
    return pl.pallas_call(
        functools.partial(_scan_kernel, tb=tb),
        grid=(bsz, t_len // tb),
        in_specs=[tok_spec] * 6 + [st_spec],
        out_specs=[tok_spec, st_spec],
        out_shape=[jax.ShapeDtypeStruct(a.shape, F32), jax.ShapeDtypeStruct(s0.shape, F32)],
        scratch_shapes=[pltpu.VMEM((HEAD_DIM, SUBLANES, LANES), F32),
                        pltpu.VMEM((tb, 5, HEAD_DIM, LANES), F32)],
        compiler_params=pltpu.CompilerParams(dimension_semantics=("parallel", "arbitrary"),
                                             vmem_limit_bytes=48 * 1024 * 1024),
        name="rwkv_scan",
    )(a, w, b, k, r, v, s0)


PROJ_COLS = 4992
LORA_PAD = 384
COL_Q, COL_R, COL_K, COL_V, COL_SK, COL_SV, COL_L = 0, 1024, 2048, 3072, 4096, 4352, 4608
VMEM_LIMIT = 56 * 1024 * 1024


def _rms(x, g):
    return x * lax.rsqrt(jnp.mean(x * x, axis=-1, keepdims=True) + RMS_EPS) * g


def _inproj_kernel(x_ref, g_ref, w_ref, o_ref, xn_ref):
    @pl.when(pl.program_id(1) == 0)
    def _():
        xn_ref[...] = _rms(x_ref[...], g_ref[...]).astype(BF16)

    o_ref[...] = jnp.dot(xn_ref[...], w_ref[...], preferred_element_type=F32)


def _inproj(x, g, w, *, tm, tn):
    m, d = x.shape
    n = w.shape[1]
    return pl.pallas_call(
        _inproj_kernel,
        grid=(m // tm, n // tn),
        in_specs=[pl.BlockSpec((tm, d), lambda i, j: (i, 0)),
                  pl.BlockSpec((1, d), lambda i, j: (0, 0)),
                  pl.BlockSpec((d, tn), lambda i, j: (0, j))],
        out_specs=pl.BlockSpec((tm, tn), lambda i, j: (i, j)),
        out_shape=jax.ShapeDtypeStruct((m, n), F32),
        scratch_shapes=[pltpu.VMEM((tm, d), BF16)],
        compiler_params=pltpu.CompilerParams(dimension_semantics=("parallel", "arbitrary"),
                                             vmem_limit_bytes=VMEM_LIMIT),
        name="inproj",
    )(x, g, w)


def _tile_sum(x):
    parts = [x[:, c * LANES:(c + 1) * LANES] for c in range(RWKV_WIDTH // LANES)]
    while len(parts) > 1:
        parts = [parts[i] + parts[i + 1] for i in range(0, len(parts), 2)]
    return parts[0]


def _tile8(x):
    return jnp.concatenate([x] * (RWKV_WIDTH // LANES), axis=1)


def _hdot(x, m):
    return jnp.dot(x, m, preferred_element_type=F32, precision=lax.Precision.HIGHEST)


def _prep_kernel(pr_ref, pk_ref, pv_ref, pl_ref, fr_ref, fk_ref, fv_ref, fl_ref,
                 mur_ref, muk_ref, muv_ref, mul_ref, w0_ref, w2_ref, a0_ref, a2_ref, g2_ref,
                 kk_ref, ka_ref, sumh_ref,
                 a_out, w_out, b_out, k_out, r_out, v_out, g_out,
                 cr_ref, ck_ref, cv_ref, cl_ref, *, period):
    def shifted(p_ref, fix_ref, carry_ref, mu_ref):
        p = p_ref[...]
        rolled = pltpu.roll(p, 1, 0)
        rows = lax.broadcasted_iota(jnp.int32, p.shape, 0)
        if period is None:
            @pl.when(pl.program_id(0) == 0)
            def _():
                carry_ref[...] = fix_ref[...]

            prev = jnp.where(rows == 0, carry_ref[...], rolled)
            carry_ref[...] = p[p.shape[0] - 1:, :]
        else:
            prev = jnp.where(rows % period == 0, fix_ref[...], rolled)
        return p + mu_ref[...] * (prev - p)

    r = shifted(pr_ref, fr_ref, cr_ref, mur_ref)
    k = shifted(pk_ref, fk_ref, ck_ref, muk_ref)
    v = shifted(pv_ref, fv_ref, cv_ref, muv_ref)
    xl = shifted(pl_ref, fl_ref, cl_ref, mul_ref)

    x_wa = xl[:, :LANES]
    zw = w0_ref[...] + _hdot(jnp.tanh(x_wa), w2_ref[...])
    za = a0_ref[...] + _hdot(x_wa, a2_ref[...])
    g = _hdot(jax.nn.sigmoid(xl[:, LANES:]), g2_ref[...])
    y = -zw
    softplus = jnp.maximum(y, 0.0) + jnp.log(1.0 + jnp.exp(-jnp.abs(y)))
    decay = jnp.exp(-jnp.exp(-softplus - 0.5))
    a_sig = jax.nn.sigmoid(za)

    kk = k * kk_ref[...]
    ssq = _tile8(_hdot(_tile_sum(kk * kk), sumh_ref[...]))
    kk = kk / jnp.maximum(jnp.sqrt(ssq), 1e-12)

    a_out[...] = -kk
    w_out[...] = decay
    b_out[...] = kk * a_sig
    k_out[...] = k * (1.0 + (a_sig - 1.0) * ka_ref[...])
    r_out[...] = r
    v_out[...] = v
    g_out[...] = g


def _prep(proj, fixes, params, *, tm, period):
    m = proj.shape[0]
    c = RWKV_WIDTH
    grid = (m // tm,)

    def col(width, idx):
        return pl.BlockSpec((tm, width), lambda i: (i, idx))

    def full(arr):
        return pl.BlockSpec(arr.shape, lambda i: (0,) * arr.ndim)

    if period is None:
        fix_specs = [full(f) for f in fixes]
    else:
        fix_specs = [pl.BlockSpec((tm, f.shape[1]), lambda i: (i, 0)) for f in fixes]
    out_spec = pl.BlockSpec((tm, c), lambda i: (i, 0))
    return pl.pallas_call(
        functools.partial(_prep_kernel, period=period),
        grid=grid,
        in_specs=[col(c, COL_R // c), col(c, COL_K // c), col(c, COL_V // c), col(LORA_PAD, COL_L // LORA_PAD)]
        + fix_specs + [full(p) for p in params],
        out_specs=[out_spec] * 7,
        out_shape=[jax.ShapeDtypeStruct((m, c), F32)] * 7,
        scratch_shapes=[pltpu.VMEM((1, c), F32)] * 3 + [pltpu.VMEM((1, LORA_PAD), F32)],
        compiler_params=pltpu.CompilerParams(dimension_semantics=("arbitrary",), vmem_limit_bytes=VMEM_LIMIT),
        name="rwkv_prep",
    )(proj, proj, proj, proj, *fixes, *params)


def _post_kernel(o_ref, r_ref, k_ref, v_ref, g_ref, lnw_ref, lnb_ref, rk_ref, gsum_ref, kv_ref, out_ref):
    o = o_ref[...]
    inv_n = 1.0 / HEAD_DIM
    mean = _tile8(_hdot(_tile_sum(o), gsum_ref[...])) * inv_n
    d = o - mean
    var = _tile8(_hdot(_tile_sum(d * d), gsum_ref[...])) * inv_n
    normed = d * lax.rsqrt(var + GN_EPS) * lnw_ref[...] + lnb_ref[...]
    rk = _tile8(_hdot(_tile_sum(r_ref[...] * k_ref[...] * rk_ref[...]), kv_ref[...]))
    out_ref[...] = ((normed + rk * v_ref[...]) * g_ref[...]).astype(BF16)


def _post(o, r, k, v, g, params, *, tm):
    m, c = o.shape
    tok = pl.BlockSpec((tm, c), lambda i: (i, 0))
    return pl.pallas_call(
        _post_kernel,
        grid=(m // tm,),
        in_specs=[tok] * 5 + [pl.BlockSpec(p.shape, lambda i: (0, 0)) for p in params],
        out_specs=tok,
        out_shape=jax.ShapeDtypeStruct((m, c), BF16),
        compiler_params=pltpu.CompilerParams(dimension_semantics=("parallel",), vmem_limit_bytes=VMEM_LIMIT),
        name="rwkv_post",
    )(o, r, k, v, g, *params)


ALIBI_SLOPES = [2.0 ** (-8.0 * (i + 1) / SWA_HEADS) for i in range(SWA_HEADS)]
SCORE_SCALE = HEAD_DIM ** -0.5


def _attend(qh, ks, vs, slope, sink, distf, valid):
    s = lax.dot_general(qh, ks, (((1,), (1,)), ((), ())), preferred_element_type=F32) * SCORE_SCALE
    s = jnp.where(valid, s - slope * distf, -jnp.inf)
    m = jnp.maximum(jnp.max(s, axis=-1, keepdims=True), sink)
    e = jnp.exp(s - m)
    den = jnp.sum(e, axis=-1, keepdims=True) + jnp.exp(sink - m)
    return jnp.dot((e / den).astype(BF16), vs, preferred_element_type=F32)


def _attn_block_kernel(sink_ref, q_ref, kc_ref, vc_ref, kp_ref, vp_ref, o_ref):
    n = pl.program_id(0)
    span = WINDOW + BLOCK
    qi = lax.broadcasted_iota(jnp.int32, (BLOCK, span), 0)
    si = lax.broadcasted_iota(jnp.int32, (BLOCK, span), 1)
    dist = qi - si + WINDOW
    valid = (dist >= 0) & (dist < WINDOW) & (n * BLOCK - WINDOW + si >= 0)
    distf = dist.astype(F32)
    for kh in range(SWA_KV_HEADS):
        lanes = slice(kh * HEAD_DIM, (kh + 1) * HEAD_DIM)
        ks = jnp.concatenate([kp_ref[:, lanes], kc_ref[:, lanes]], axis=0).astype(BF16)
        vs = jnp.concatenate([vp_ref[:, lanes], vc_ref[:, lanes]], axis=0).astype(BF16)
        for gi in range(SWA_GROUP):
            h = kh * SWA_GROUP + gi
            hl = slice(h * HEAD_DIM, (h + 1) * HEAD_DIM)
            oh = _attend(q_ref[:, hl].astype(BF16), ks, vs, ALIBI_SLOPES[h], sink_ref[h], distf, valid)
            o_ref[:, hl] = oh.astype(BF16)


def _attn_prompt(proj, sinks):
    t = proj.shape[0]
    cur = lambda width, idx: pl.BlockSpec((BLOCK, width), lambda n: (n, idx))
    prev = lambda width, idx: pl.BlockSpec((BLOCK, width), lambda n: (jnp.maximum(n - 1, 0), idx))
    return pl.pallas_call(
        _attn_block_kernel,
        grid=(t // BLOCK,),
        in_specs=[pl.BlockSpec(memory_space=pltpu.SMEM),
                  cur(SWA_WIDTH, COL_Q // SWA_WIDTH), cur(KV_WIDTH, COL_SK // KV_WIDTH), cur(KV_WIDTH, COL_SV // KV_WIDTH),
                  prev(KV_WIDTH, COL_SK // KV_WIDTH), prev(KV_WIDTH, COL_SV // KV_WIDTH)],
        out_specs=pl.BlockSpec((BLOCK, SWA_WIDTH), lambda n: (n, 0)),
        out_shape=jax.ShapeDtypeStruct((t, SWA_WIDTH), BF16),
        compiler_params=pltpu.CompilerParams(dimension_semantics=("parallel",), vmem_limit_bytes=VMEM_LIMIT),
        name="swa_prompt",
    )(sinks, proj, proj, proj, proj, proj)


def _attn_decode_kernel(sink_ref, q_ref, kn_ref, vn_ref, kp_ref, vp_ref, o_ref, *, t_new):
    span = WINDOW + SUBLANES
    qi = lax.broadcasted_iota(jnp.int32, (t_new, span), 0)
    si = lax.broadcasted_iota(jnp.int32, (t_new, span), 1)
    dist = qi - si + WINDOW
    valid = (dist >= 0) & (dist < WINDOW)
    distf = dist.astype(F32)
    pad = jnp.zeros((SUBLANES - t_new, HEAD_DIM), F32)
    for kh in range(SWA_KV_HEADS):
        lanes = slice(kh * HEAD_DIM, (kh + 1) * HEAD_DIM)
        ks = jnp.concatenate([kp_ref[0, :, lanes], kn_ref[0, :, lanes], pad], axis=0).astype(BF16)
        vs = jnp.concatenate([vp_ref[0, :, lanes], vn_ref[0, :, lanes], pad], axis=0).astype(BF16)
        for gi in range(SWA_GROUP):
            h = kh * SWA_GROUP + gi
            hl = slice(h * HEAD_DIM, (h + 1) * HEAD_DIM)
            oh = _attend(q_ref[0, :, hl].astype(BF16), ks, vs, ALIBI_SLOPES[h], sink_ref[h], distf, valid)
            o_ref[0, :, hl] = oh.astype(BF16)


def _attn_decode(proj3, k_past, v_past, sinks):
    bsz, t_new = proj3.shape[0], proj3.shape[1]
    new = lambda width, idx: pl.BlockSpec((1, t_new, width), lambda b: (b, 0, idx))
    past = pl.BlockSpec((1, WINDOW, KV_WIDTH), lambda b: (b, 0, 0))
    return pl.pallas_call(
        functools.partial(_attn_decode_kernel, t_new=t_new),
        grid=(bsz,),
        in_specs=[pl.BlockSpec(memory_space=pltpu.SMEM),
                  new(SWA_WIDTH, COL_Q // SWA_WIDTH), new(KV_WIDTH, COL_SK // KV_WIDTH), new(KV_WIDTH, COL_SV // KV_WIDTH),
                  past, past],
        out_specs=pl.BlockSpec((1, t_new, SWA_WIDTH), lambda b: (b, 0, 0)),
        out_shape=jax.ShapeDtypeStruct((bsz, t_new, SWA_WIDTH), BF16),
        compiler_params=pltpu.CompilerParams(dimension_semantics=("parallel",), vmem_limit_bytes=VMEM_LIMIT),
        name="swa_decode",
    )(sinks, proj3, proj3, proj3, k_past, v_past)


def _outproj_kernel(x_ref, orw_ref, osw_ref, wr_ref, ws_ref, g_ref, x1_ref, h2_ref):
    x1 = (x_ref[...] + jnp.dot(orw_ref[...], wr_ref[...], preferred_element_type=F32)
          + jnp.dot(osw_ref[...], ws_ref[...], preferred_element_type=F32))
    x1_ref[...] = x1
    h2_ref[...] = _rms(x1, g_ref[...]).astype(BF16)


def _outproj(x, o_rwkv, o_swa, w_r, w_s, g, *, tm):
    m, d = x.shape
    row = lambda width: pl.BlockSpec((tm, width), lambda i: (i, 0))
    full = lambda arr: pl.BlockSpec(arr.shape, lambda i: (0, 0))
    return pl.pallas_call(
        _outproj_kernel,
        grid=(m // tm,),
        in_specs=[row(d), row(RWKV_WIDTH), row(SWA_WIDTH), full(w_r), full(w_s), full(g)],
        out_specs=[row(d), row(d)],
        out_shape=[jax.ShapeDtypeStruct((m, d), F32), jax.ShapeDtypeStruct((m, d), BF16)],
        compiler_params=pltpu.CompilerParams(dimension_semantics=("parallel",), vmem_limit_bytes=VMEM_LIMIT),
        name="outproj",
    )(x, o_rwkv, o_swa, w_r, w_s, g)


def _ffn_kernel(h_ref, x1_ref, wg_ref, wu_ref, wd_ref, gf_ref, y_ref, acc_ref):
    j = pl.program_id(1)

    @pl.when(j == 0)
    def _():
        acc_ref[...] = jnp.zeros_like(acc_ref)

    h = h_ref[...]
    gate = jnp.dot(h, wg_ref[...], preferred_element_type=F32)
    up = jnp.dot(h, wu_ref[...], preferred_element_type=F32)
    act = (gate * jax.nn.sigmoid(gate) * up).astype(BF16)
    acc_ref[...] += jnp.dot(act, wd_ref[...], preferred_element_type=F32)

    @pl.when(j == pl.num_programs(1) - 1)
    def _():
        y_ref[...] = _rms(x1_ref[...] + acc_ref[...], gf_ref[...])


def _ffn(h2, x1, w_gate, w_up, w_down, g_final, *, tm, tf):
    m, d = x1.shape
    f = w_gate.shape[1]
    return pl.pallas_call(
        _ffn_kernel,
        grid=(m // tm, f // tf),
        in_specs=[pl.BlockSpec((tm, d), lambda i, j: (i, 0)),
                  pl.BlockSpec((tm, d), lambda i, j: (i, 0)),
                  pl.BlockSpec((d, tf), lambda i, j: (0, j)),
                  pl.BlockSpec((d, tf), lambda i, j: (0, j)),
                  pl.BlockSpec((tf, d), lambda i, j: (j, 0)),
                  pl.BlockSpec((1, d), lambda i, j: (0, 0))],
        out_specs=pl.BlockSpec((tm, d), lambda i, j: (i, 0)),
        out_shape=jax.ShapeDtypeStruct((m, d), F32),
        scratch_shapes=[pltpu.VMEM((tm, d), F32)],
        compiler_params=pltpu.CompilerParams(dimension_semantics=("parallel", "arbitrary"),
                                             vmem_limit_bytes=VMEM_LIMIT),
        name="ffn",
    )(h2, x1, w_gate, w_up, w_down, g_final)


def _group_matrices():
    lane = np.arange(LANES)
    key_head = lane % RWKV_HEADS
    val_head = lane // I_LO
    sum_key = (key_head[:, None] == key_head[None, :]).astype(np.float32)
    sum_val = (val_head[:, None] == val_head[None, :]).astype(np.float32)
    key_to_val = (key_head[:, None] == val_head[None, :]).astype(np.float32)
    return sum_key, sum_val, key_to_val


def _state_to_kernel(s):
    b = s.shape[0]
    s = s.reshape(b, RWKV_HEADS, SUBLANES, I_LO, HEAD_DIM)
    return s.transpose(0, 4, 2, 1, 3).reshape(b, HEAD_DIM, SUBLANES, LANES)


def _state_from_kernel(s):
    b = s.shape[0]
    s = s.reshape(b, HEAD_DIM, SUBLANES, RWKV_HEADS, I_LO)
    return s.transpose(0, 3, 2, 4, 1).reshape(b, RWKV_HEADS, HEAD_DIM, HEAD_DIM)


def _shift_to_cols(shift):
    c = RWKV_WIDTH
    return (shift[:, 0:c][:, KEY_PERM], shift[:, c:2 * c][:, KEY_PERM], shift[:, 2 * c:3 * c][:, VAL_PERM],
            jnp.pad(shift[:, 3 * c:], ((0, 0), (0, LORA_PAD - LORA_COLS))))


def _shift_from_proj(rows):
    return jnp.concatenate([rows[:, COL_R:COL_K][:, KEY_INV], rows[:, COL_K:COL_V][:, KEY_INV],
                            rows[:, COL_V:COL_SK][:, VAL_INV], rows[:, COL_L:COL_L + LORA_COLS]], axis=1)


def _layer_weights(norm_attn, w_in, mu_shift, w0, w2, a0, a2, g2, k_k, k_a, r_k, ln_x_w, ln_x_b, sinks, w_out,
                   norm_ffn, w_gate, w_up, w_down, norm_final):
    c = RWKV_WIDTH
    swa0 = RWKV_COLS
    w_proj = jnp.concatenate([
        w_in[:, swa0:swa0 + SWA_WIDTH],
        w_in[:, 0:c][:, KEY_PERM], w_in[:, c:2 * c][:, KEY_PERM], w_in[:, 2 * c:3 * c][:, VAL_PERM],
        w_in[:, swa0 + SWA_WIDTH:swa0 + SWA_WIDTH + 2 * KV_WIDTH],
        jnp.pad(w_in[:, 3 * c:RWKV_COLS], ((0, 0), (0, LORA_PAD - LORA_COLS))),
    ], axis=1).astype(BF16)
    row = lambda v: v.reshape(1, -1)
    mu = _shift_to_cols(mu_shift.reshape(1, -1))
    w2p = jnp.pad(w2[:, KEY_PERM], ((0, LANES - DECAY_LORA), (0, 0)))
    a2p = jnp.pad(a2[:, KEY_PERM], ((DECAY_LORA, 0), (0, 0)))
    g2p = jnp.pad(g2[:, VAL_PERM], ((0, LORA_PAD - LANES - GATE_LORA), (0, 0)))
    sum_key, sum_val, key_to_val = _group_matrices()
    prep_params = (*mu, row(w0[KEY_PERM]), w2p, row(a0[KEY_PERM]), a2p, g2p,
                   row(k_k[KEY_PERM]), row(k_a[KEY_PERM]), jnp.asarray(sum_key))
    post_params = (row(ln_x_w[VAL_PERM]), row(ln_x_b[VAL_PERM]), row(r_k.reshape(-1)[KEY_PERM]),
                   jnp.asarray(sum_val), jnp.asarray(key_to_val))
    return dict(
        norm_attn=row(norm_attn), w_proj=w_proj, prep=prep_params, post=post_params, sinks=sinks,
        w_out_r=w_out[:c][VAL_PERM].astype(BF16), w_out_s=w_out[c:].astype(BF16), norm_ffn=row(norm_ffn),
        w_gate=w_gate.astype(BF16), w_up=w_up.astype(BF16), w_down=w_down.astype(BF16), norm_final=row(norm_final))


def _layer(x, shift_prev, state0, kv_past, lw, *, tiles):
    bsz, t_len, d = x.shape
    m = bsz * t_len
    x2 = x.reshape(m, d)
    proj = _inproj(x2, lw["norm_attn"], lw["w_proj"], tm=tiles["tm_in"], tn=tiles["tn_in"])

    fix_cols = _shift_to_cols(shift_prev)
    if bsz == 1:
        fixes, period = fix_cols, None
    else:
        fixes = tuple(jnp.pad(f[:, None, :], ((0, 0), (0, t_len - 1), (0, 0))).reshape(m, -1) for f in fix_cols)
        period = t_len
    a, w, b, k, r, v, g = _prep(proj, fixes, lw["prep"], tm=tiles["tm_prep"], period=period)

    tok = lambda z: z.reshape(bsz, t_len, SUBLANES, LANES)
    o, s_fin = _rwkv_scan(tok(a), tok(w), tok(b), tok(k), tok(r), tok(v), _state_to_kernel(state0), tb=tiles["tb"])
    o_rwkv = _post(o.reshape(m, RWKV_WIDTH), r, k, v, g, lw["post"], tm=tiles["tm_prep"])

    if kv_past is None:
        o_swa = _attn_prompt(proj, lw["sinks"])
        new_k = proj[m - WINDOW:, COL_SK:COL_SV].reshape(1, WINDOW, SWA_KV_HEADS, HEAD_DIM)
        new_v = proj[m - WINDOW:, COL_SV:COL_L].reshape(1, WINDOW, SWA_KV_HEADS, HEAD_DIM)
    else:
        k_past, v_past = kv_past
        proj3 = proj.reshape(bsz, t_len, PROJ_COLS)
        o_swa = _attn_decode(proj3, k_past.reshape(bsz, WINDOW, KV_WIDTH), v_past.reshape(bsz, WINDOW, KV_WIDTH),
                             lw["sinks"]).reshape(m, SWA_WIDTH)
        kv_shape = (bsz, t_len, SWA_KV_HEADS, HEAD_DIM)
        new_k = jnp.concatenate([k_past[:, t_len:], proj3[:, :, COL_SK:COL_SV].reshape(kv_shape)], axis=1)
        new_v = jnp.concatenate([v_past[:, t_len:], proj3[:, :, COL_SV:COL_L].reshape(kv_shape)], axis=1)

    x1, h2 = _outproj(x2, o_rwkv, o_swa, lw["w_out_r"], lw["w_out_s"], lw["norm_ffn"], tm=tiles["tm_out"])
    y = _ffn(h2, x1, lw["w_gate"], lw["w_up"], lw["w_down"], lw["norm_final"], tm=tiles["tm_ffn"], tf=tiles["tf"])

    new_shift = _shift_from_proj(proj.reshape(bsz, t_len, PROJ_COLS)[:, -1])
    return y.reshape(bsz, t_len, d), _state_from_kernel(s_fin), new_shift, new_k, new_v


def _tiles(m, t_len):
    return dict(tm_in=min(512, m), tn_in=1664, tm_prep=min(256, m), tb=min(64, t_len),
                tm_out=min(256, m), tm_ffn=min(512, m), tf=512)


def kernel(x_prompt, x_sample, state_rwkv, state_shift, cache_swa_k, cache_swa_v, norm_attn, w_in, mu_shift, w0, w2, a0, a2, g2, k_k, k_a, r_k, ln_x_w, ln_x_b, sinks, w_out, norm_ffn, w_gate, w_up, w_down, norm_final):
    assert norm_attn.shape[0] == 1, "single trunk layer"
    lw = _layer_weights(norm_attn[0], w_in[0], mu_shift[0], w0[0], w2[0], a0[0], a2[0], g2[0], k_k[0], k_a[0],
                        r_k[0], ln_x_w[0], ln_x_b[0], sinks[0], w_out[0], norm_ffn[0], w_gate[0], w_up[0],
                        w_down[0], norm_final)
    bp, tp, _ = x_prompt.shape
    bs, ts, _ = x_sample.shape
    assert bp == 1 and tp % BLOCK == 0 and ts < SUBLANES
    yp, p_state, p_shift, p_k, p_v = _layer(
        x_prompt, jnp.zeros((bp, RWKV_COLS), F32), jnp.zeros((bp, RWKV_HEADS, HEAD_DIM, HEAD_DIM), F32),
        None, lw, tiles=_tiles(bp * tp, tp))
    ys, s_state, s_shift, s_k, s_v = _layer(
        x_sample, state_shift[0], state_rwkv[0], (cache_swa_k[0], cache_swa_v[0]), lw, tiles=_tiles(bs * ts, ts))
    return (yp, ys, p_state[None], p_shift[None], p_k[None], p_v[None],
            s_state[None], s_shift[None], s_k[None], s_v[None])
```

```python
import functools

import numpy as np
import jax
import jax.numpy as jnp
from jax import lax
from jax.experimental import pallas as pl
from jax.experimental.pallas import tpu as pltpu

F32 = jnp.float32
BF16 = jnp.bfloat16

D_MODEL = 2048
HEAD_DIM = 64
RWKV_WIDTH = 1024
RWKV_HEADS = 16
SWA_WIDTH = 1024
SWA_HEADS = 16
SWA_KV_HEADS = 4
SWA_GROUP = 4
KV_WIDTH = 256
WINDOW = 128
BLOCK = 128
DECAY_LORA = 64
AAA_LORA = 64
GATE_LORA = 160
LORA_COLS = DECAY_LORA + AAA_LORA + GATE_LORA
RWKV_COLS = 3 * RWKV_WIDTH + LORA_COLS
D_FF = 5632
RMS_EPS = 1e-5
GN_EPS = 64e-5

SUBLANES = 8
LANES = 128
I_LO = LANES // RWKV_HEADS


def _key_perm():
    p = np.arange(RWKV_WIDTH)
    return (p % RWKV_HEADS) * HEAD_DIM + p // RWKV_HEADS


def _val_perm():
    p = np.arange(RWKV_WIDTH)
    return ((p % LANES) // I_LO) * HEAD_DIM + (p // LANES) * I_LO + p % I_LO


KEY_PERM = _key_perm()
VAL_PERM = _val_perm()
KEY_INV = np.argsort(KEY_PERM)
VAL_INV = np.argsort(VAL_PERM)


J_GROUPS = HEAD_DIM // SUBLANES
I_GROUPS = HEAD_DIM // I_LO


def _sublane_allsum(x):
    for shift in (4, 2, 1):
        x = x + pltpu.roll(x, shift, 0)
    return x


def _sublane_sums(xs):
    sub = lax.broadcasted_iota(jnp.int32, (SUBLANES, LANES), 0)

    def combine(x, y, d):
        clear = (sub & d) == 0
        return jnp.where(clear, x, pltpu.roll(y, d, 0)) + jnp.where(clear, pltpu.roll(x, SUBLANES - d, 0), y)

    z = [combine(xs[0], xs[4], 4), combine(xs[2], xs[6], 4), combine(xs[1], xs[5], 4), combine(xs[3], xs[7], 4)]
    return combine(combine(z[0], z[1], 2), combine(z[2], z[3], 2), 1)


def _scan_kernel(a_ref, w_ref, b_ref, k_ref, r_ref, v_ref, s0_ref, o_ref, sT_ref, s_scr, sa_scr, *tiles, tb):
    ti = pl.program_id(1)

    @pl.when(ti == 0)
    def _():
        s_scr[...] = s0_ref[0]

    gather_idx = (lax.broadcasted_iota(jnp.int32, (SUBLANES, LANES), 0) * RWKV_HEADS
                  + lax.broadcasted_iota(jnp.int32, (SUBLANES, LANES), 1) // I_LO)

    srcs = (a_ref, w_ref, b_ref, k_ref, r_ref)
    tile = lambda qi, parity: tiles[2 * qi + parity]

    def build(qi, t, parity):
        src_t = jnp.minimum(t, tb - 1)
        for jg in range(J_GROUPS):
            src = jnp.broadcast_to(srcs[qi][0, src_t, pl.ds(jg, 1), :], (SUBLANES, LANES))
            tile(qi, parity)[jg] = jnp.take_along_axis(src, gather_idx, axis=1)

    for qi in range(5):
        build(qi, 0, 0)
    build(0, 1, 1)
    sa0 = [None] * I_GROUPS
    for jg in range(J_GROUPS):
        a0 = tile(0, 0)[jg]
        for ig in range(I_GROUPS):
            term = s_scr[jg * I_GROUPS + ig] * a0
            sa0[ig] = term if sa0[ig] is None else sa0[ig] + term
    sa0 = [_sublane_allsum(x) for x in sa0]

    def step(t, parity, sa):
        for qi in range(1, 5):
            build(qi, t + 1, 1 - parity)
        build(0, t + 2, parity)
        vb = [jnp.broadcast_to(v_ref[0, t, pl.ds(ig, 1), :], (SUBLANES, LANES)) for ig in range(I_GROUPS)]
        out = [None] * I_GROUPS
        nxt = [None] * I_GROUPS
        for jg in range(J_GROUPS):
            w, b, k, r = (tile(qi, parity)[jg] for qi in range(1, 5))
            a_next = tile(0, 1 - parity)[jg]
            for ig in range(I_GROUPS):
                idx = jg * I_GROUPS + ig
                sj = s_scr[idx] * w + sa[ig] * b + vb[ig] * k
                s_scr[idx] = sj
                to, tn = sj * r, sj * a_next
                out[ig] = to if out[ig] is None else out[ig] + to
                nxt[ig] = tn if nxt[ig] is None else nxt[ig] + tn
        o_ref[0, t] = _sublane_sums(out)
        sa_scr[parity] = _sublane_sums(nxt)
        return [jnp.broadcast_to(sa_scr[parity, pl.ds(ig, 1), :], (SUBLANES, LANES)) for ig in range(I_GROUPS)]

    def two_steps(u, sa):
        return step(2 * u + 1, 1, step(2 * u, 0, sa))

    lax.fori_loop(0, tb // 2, two_steps, sa0)

    @pl.when(ti == pl.num_programs(1) - 1)
    def _():
        sT_ref[0] = s_scr[...]


def _rwkv_scan(a, w, b, k, r, v, s0, *, tb):
    bsz, t_len = a.shape[0], a.shape[1]
    assert t_len % tb == 0 and tb % 2 == 0
    tok_spec = pl.BlockSpec((1, tb, SUBLANES, LANES), lambda bi, ti: (bi, ti, 0, 0))
    st_spec = pl.BlockSpec((1, HEAD_DIM, SUBLANES, LANES), lambda bi, ti: (bi, 0, 0, 0))
    return pl.pallas_call(
        functools.partial(_scan_kernel, tb=tb),
        grid=(bsz, t_len // tb),
        in_specs=[tok_spec] * 6 + [st_spec],
        out_specs=[tok_spec, st_spec],
        out_shape=[jax.ShapeDtypeStruct(a.shape, F32), jax.ShapeDtypeStruct(s0.shape, F32)],
        scratch_shapes=([pltpu.VMEM((HEAD_DIM, SUBLANES, LANES), F32), pltpu.VMEM((2, SUBLANES, LANES), F32)]
                        + [pltpu.VMEM((J_GROUPS, SUBLANES, LANES), F32)] * 10),
        compiler_params=pltpu.CompilerParams(dimension_semantics=("parallel", "arbitrary"),
                                             vmem_limit_bytes=48 * 1024 * 1024),
        name="rwkv_scan",
    )(a, w, b, k, r, v, s0)


PROJ_COLS = 4992
LORA_PAD = 384
COL_Q, COL_R, COL_K, COL_V, COL_SK, COL_SV, COL_L = 0, 1024, 2048, 3072, 4096, 4352, 4608
VMEM_LIMIT = 56 * 1024 * 1024


def _rms(x, g):
    return x * lax.rsqrt(jnp.mean(x * x, axis=-1, keepdims=True) + RMS_EPS) * g


def _inproj_kernel(x_ref, g_ref, w_ref, o_ref, xn_ref):
    @pl.when(pl.program_id(1) == 0)
    def _():
        xn_ref[...] = _rms(x_ref[...], g_ref[...]).astype(BF16)

    o_ref[...] = jnp.dot(xn_ref[...], w_ref[...], preferred_element_type=F32)


def _inproj(x, g, w, *, tm, tn):
    m, d = x.shape
    n = w.shape[1]
    return pl.pallas_call(
        _inproj_kernel,
        grid=(m // tm, n // tn),
        in_specs=[pl.BlockSpec((tm, d), lambda i, j: (i, 0)),
                  pl.BlockSpec((1, d), lambda i, j: (0, 0)),
                  pl.BlockSpec((d, tn), lambda i, j: (0, j))],
        out_specs=pl.BlockSpec((tm, tn), lambda i, j: (i, j)),
        out_shape=jax.ShapeDtypeStruct((m, n), F32),
        scratch_shapes=[pltpu.VMEM((tm, d), BF16)],
        compiler_params=pltpu.CompilerParams(dimension_semantics=("parallel", "arbitrary"),
                                             vmem_limit_bytes=VMEM_LIMIT),
        name="inproj",
    )(x, g, w)


def _tile_sum(x):
    parts = [x[:, c * LANES:(c + 1) * LANES] for c in range(RWKV_WIDTH // LANES)]
    while len(parts) > 1:
        parts = [parts[i] + parts[i + 1] for i in range(0, len(parts), 2)]
    return parts[0]


def _tile8(x):
    return jnp.concatenate([x] * (RWKV_WIDTH // LANES), axis=1)


def _hdot(x, m):
    return jnp.dot(x, m, preferred_element_type=F32, precision=lax.Precision.HIGHEST)


def _prep_kernel(pr_ref, pk_ref, pv_ref, pl_ref, fr_ref, fk_ref, fv_ref, fl_ref,
                 mur_ref, muk_ref, muv_ref, mul_ref, w0_ref, w2_ref, a0_ref, a2_ref, g2_ref,
                 kk_ref, ka_ref, sumh_ref,
                 a_out, w_out, b_out, k_out, r_out, v_out, g_out,
                 cr_ref, ck_ref, cv_ref, cl_ref, *, period):
    def shifted(p_ref, fix_ref, carry_ref, mu_ref):
        p = p_ref[...]
        rolled = pltpu.roll(p, 1, 0)
        rows = lax.broadcasted_iota(jnp.int32, p.shape, 0)
        if period is None:
            @pl.when(pl.program_id(0) == 0)
            def _():
                carry_ref[...] = fix_ref[...]

            prev = jnp.where(rows == 0, carry_ref[...], rolled)
            carry_ref[...] = p[p.shape[0] - 1:, :]
        else:
            prev = jnp.where(rows % period == 0, fix_ref[...], rolled)
        return p + mu_ref[...] * (prev - p)

    r = shifted(pr_ref, fr_ref, cr_ref, mur_ref)
    k = shifted(pk_ref, fk_ref, ck_ref, muk_ref)
    v = shifted(pv_ref, fv_ref, cv_ref, muv_ref)
    xl = shifted(pl_ref, fl_ref, cl_ref, mul_ref)

    x_wa = xl[:, :LANES]
    zw = w0_ref[...] + _hdot(jnp.tanh(x_wa), w2_ref[...])
    za = a0_ref[...] + _hdot(x_wa, a2_ref[...])
    g = _hdot(jax.nn.sigmoid(xl[:, LANES:]), g2_ref[...])
    y = -zw
    softplus = jnp.maximum(y, 0.0) + jnp.log(1.0 + jnp.exp(-jnp.abs(y)))
    decay = jnp.exp(-jnp.exp(-softplus - 0.5))
    a_sig = jax.nn.sigmoid(za)

    kk = k * kk_ref[...]
    ssq = _tile8(_hdot(_tile_sum(kk * kk), sumh_ref[...]))
    kk = kk / jnp.maximum(jnp.sqrt(ssq), 1e-12)

    a_out[...] = -kk
    w_out[...] = decay
    b_out[...] = kk * a_sig
    k_out[...] = k * (1.0 + (a_sig - 1.0) * ka_ref[...])
    r_out[...] = r
    v_out[...] = v
    g_out[...] = g


def _prep(proj, fixes, params, *, tm, period):
    m = proj.shape[0]
    c = RWKV_WIDTH
    grid = (m // tm,)

    def col(width, idx):
        return pl.BlockSpec((tm, width), lambda i: (i, idx))

    def full(arr):
        return pl.BlockSpec(arr.shape, lambda i: (0,) * arr.ndim)

    if period is None:
        fix_specs = [full(f) for f in fixes]
    else:
        fix_specs = [pl.BlockSpec((tm, f.shape[1]), lambda i: (i, 0)) for f in fixes]
    out_spec = pl.BlockSpec((tm, c), lambda i: (i, 0))
    return pl.pallas_call(
        functools.partial(_prep_kernel, period=period),
        grid=grid,
        in_specs=[col(c, COL_R // c), col(c, COL_K // c), col(c, COL_V // c), col(LORA_PAD, COL_L // LORA_PAD)]
        + fix_specs + [full(p) for p in params],
        out_specs=[out_spec] * 7,
        out_shape=[jax.ShapeDtypeStruct((m, c), F32)] * 7,
        scratch_shapes=[pltpu.VMEM((1, c), F32)] * 3 + [pltpu.VMEM((1, LORA_PAD), F32)],
        compiler_params=pltpu.CompilerParams(dimension_semantics=("arbitrary",), vmem_limit_bytes=VMEM_LIMIT),
        name="rwkv_prep",
    )(proj, proj, proj, proj, *fixes, *params)


def _post_kernel(o_ref, r_ref, k_ref, v_ref, g_ref, lnw_ref, lnb_ref, rk_ref, gsum_ref, kv_ref, out_ref):
    o = o_ref[...]
    inv_n = 1.0 / HEAD_DIM
    mean = _tile8(_hdot(_tile_sum(o), gsum_ref[...])) * inv_n
    d = o - mean
    var = _tile8(_hdot(_tile_sum(d * d), gsum_ref[...])) * inv_n
    normed = d * lax.rsqrt(var + GN_EPS) * lnw_ref[...] + lnb_ref[...]
    rk = _tile8(_hdot(_tile_sum(r_ref[...] * k_ref[...] * rk_ref[...]), kv_ref[...]))
    out_ref[...] = ((normed + rk * v_ref[...]) * g_ref[...]).astype(BF16)


def _post(o, r, k, v, g, params, *, tm):
    m, c = o.shape
    tok = pl.BlockSpec((tm, c), lambda i: (i, 0))
    return pl.pallas_call(
        _post_kernel,
        grid=(m // tm,),
        in_specs=[tok] * 5 + [pl.BlockSpec(p.shape, lambda i: (0, 0)) for p in params],
        out_specs=tok,
        out_shape=jax.ShapeDtypeStruct((m, c), BF16),
        compiler_params=pltpu.CompilerParams(dimension_semantics=("parallel",), vmem_limit_bytes=VMEM_LIMIT),
        name="rwkv_post",
    )(o, r, k, v, g, *params)


ALIBI_SLOPES = [2.0 ** (-8.0 * (i + 1) / SWA_HEADS) for i in range(SWA_HEADS)]
SCORE_SCALE = HEAD_DIM ** -0.5


def _attend(qh, ks, vs, slope, sink, distf, valid):
    s = lax.dot_general(qh, ks, (((1,), (1,)), ((), ())), preferred_element_type=F32) * SCORE_SCALE
    s = jnp.where(valid, s - slope * distf, -jnp.inf)
    m = jnp.maximum(jnp.max(s, axis=-1, keepdims=True), sink)
    e = jnp.exp(s - m)
    den = jnp.sum(e, axis=-1, keepdims=True) + jnp.exp(sink - m)
    return jnp.dot((e / den).astype(BF16), vs, preferred_element_type=F32)


def _attn_block_kernel(sink_ref, q_ref, kc_ref, vc_ref, kp_ref, vp_ref, o_ref):
    n = pl.program_id(0)
    span = WINDOW + BLOCK
    qi = lax.broadcasted_iota(jnp.int32, (BLOCK, span), 0)
    si = lax.broadcasted_iota(jnp.int32, (BLOCK, span), 1)
    dist = qi - si + WINDOW
    valid = (dist >= 0) & (dist < WINDOW) & (n * BLOCK - WINDOW + si >= 0)
    distf = dist.astype(F32)
    for kh in range(SWA_KV_HEADS):
        lanes = slice(kh * HEAD_DIM, (kh + 1) * HEAD_DIM)
        ks = jnp.concatenate([kp_ref[:, lanes], kc_ref[:, lanes]], axis=0).astype(BF16)
        vs = jnp.concatenate([vp_ref[:, lanes], vc_ref[:, lanes]], axis=0).astype(BF16)
        for gi in range(SWA_GROUP):
            h = kh * SWA_GROUP + gi
            hl = slice(h * HEAD_DIM, (h + 1) * HEAD_DIM)
            oh = _attend(q_ref[:, hl].astype(BF16), ks, vs, ALIBI_SLOPES[h], sink_ref[h], distf, valid)
            o_ref[:, hl] = oh.astype(BF16)


def _attn_prompt(proj, sinks):
    t = proj.shape[0]
    cur = lambda width, idx: pl.BlockSpec((BLOCK, width), lambda n: (n, idx))
    prev = lambda width, idx: pl.BlockSpec((BLOCK, width), lambda n: (jnp.maximum(n - 1, 0), idx))
    return pl.pallas_call(
        _attn_block_kernel,
        grid=(t // BLOCK,),
        in_specs=[pl.BlockSpec(memory_space=pltpu.SMEM),
                  cur(SWA_WIDTH, COL_Q // SWA_WIDTH), cur(KV_WIDTH, COL_SK // KV_WIDTH), cur(KV_WIDTH, COL_SV // KV_WIDTH),
                  prev(KV_WIDTH, COL_SK // KV_WIDTH), prev(KV_WIDTH, COL_SV // KV_WIDTH)],
        out_specs=pl.BlockSpec((BLOCK, SWA_WIDTH), lambda n: (n, 0)),
        out_shape=jax.ShapeDtypeStruct((t, SWA_WIDTH), BF16),
        compiler_params=pltpu.CompilerParams(dimension_semantics=("parallel",), vmem_limit_bytes=VMEM_LIMIT),
        name="swa_prompt",
    )(sinks, proj, proj, proj, proj, proj)


def _attn_decode_kernel(sink_ref, q_ref, kn_ref, vn_ref, kp_ref, vp_ref, o_ref, *, t_new, bt):
    span = WINDOW + SUBLANES
    qi = lax.broadcasted_iota(jnp.int32, (bt, SUBLANES, span), 1)
    si = lax.broadcasted_iota(jnp.int32, (bt, SUBLANES, span), 2)
    dist = qi - si + WINDOW
    valid = (dist >= 0) & (dist < WINDOW)
    distf = dist.astype(F32)
    pad = jnp.zeros((bt, SUBLANES - t_new, HEAD_DIM), F32)
    for kh in range(SWA_KV_HEADS):
        lanes = slice(kh * HEAD_DIM, (kh + 1) * HEAD_DIM)
        ks = jnp.concatenate([kp_ref[:, :, lanes], kn_ref[:, :, lanes], pad], axis=1).astype(BF16)
        vs = jnp.concatenate([vp_ref[:, :, lanes], vn_ref[:, :, lanes], pad], axis=1).astype(BF16)
        for gi in range(SWA_GROUP):
            h = kh * SWA_GROUP + gi
            hl = slice(h * HEAD_DIM, (h + 1) * HEAD_DIM)
            qh = jnp.concatenate([q_ref[:, :, hl], pad], axis=1).astype(BF16)
            s = jnp.einsum("bqd,bsd->bqs", qh, ks, preferred_element_type=F32) * SCORE_SCALE
            s = jnp.where(valid, s - ALIBI_SLOPES[h] * distf, -jnp.inf)
            sink = sink_ref[h]
            m = jnp.maximum(jnp.max(s, axis=-1, keepdims=True), sink)
            e = jnp.exp(s - m)
            den = jnp.sum(e, axis=-1, keepdims=True) + jnp.exp(sink - m)
            oh = jnp.einsum("bqs,bsd->bqd", (e / den).astype(BF16), vs, preferred_element_type=F32)
            o_ref[:, :, hl] = oh[:, :t_new, :].astype(BF16)


def _attn_decode(proj3, k_past, v_past, sinks, *, bt):
    bsz, t_new = proj3.shape[0], proj3.shape[1]
    new = lambda width, idx: pl.BlockSpec((bt, t_new, width), lambda b: (b, 0, idx))
    past = pl.BlockSpec((bt, WINDOW, KV_WIDTH), lambda b: (b, 0, 0))
    return pl.pallas_call(
        functools.partial(_attn_decode_kernel, t_new=t_new, bt=bt),
        grid=(bsz // bt,),
        in_specs=[pl.BlockSpec(memory_space=pltpu.SMEM),
                  new(SWA_WIDTH, COL_Q // SWA_WIDTH), new(KV_WIDTH, COL_SK // KV_WIDTH), new(KV_WIDTH, COL_SV // KV_WIDTH),
                  past, past],
        out_specs=pl.BlockSpec((bt, t_new, SWA_WIDTH), lambda b: (b, 0, 0)),
        out_shape=jax.ShapeDtypeStruct((bsz, t_new, SWA_WIDTH), BF16),
        compiler_params=pltpu.CompilerParams(dimension_semantics=("parallel",), vmem_limit_bytes=VMEM_LIMIT),
        name="swa_decode",
    )(sinks, proj3, proj3, proj3, k_past, v_past)


def _outproj_kernel(x_ref, orw_ref, osw_ref, wr_ref, ws_ref, g_ref, x1_ref, h2_ref):
    x1 = (x_ref[...] + jnp.dot(orw_ref[...], wr_ref[...], preferred_element_type=F32)
          + jnp.dot(osw_ref[...], ws_ref[...], preferred_element_type=F32))
    x1_ref[...] = x1
    h2_ref[...] = _rms(x1, g_ref[...]).astype(BF16)


def _outproj(x, o_rwkv, o_swa, w_r, w_s, g, *, tm):
    m, d = x.shape
    row = lambda width: pl.BlockSpec((tm, width), lambda i: (i, 0))
    full = lambda arr: pl.BlockSpec(arr.shape, lambda i: (0, 0))
    return pl.pallas_call(
        _outproj_kernel,
        grid=(m // tm,),
        in_specs=[row(d), row(RWKV_WIDTH), row(SWA_WIDTH), full(w_r), full(w_s), full(g)],
        out_specs=[row(d), row(d)],
        out_shape=[jax.ShapeDtypeStruct((m, d), F32), jax.ShapeDtypeStruct((m, d), BF16)],
        compiler_params=pltpu.CompilerParams(dimension_semantics=("parallel",), vmem_limit_bytes=VMEM_LIMIT),
        name="outproj",
    )(x, o_rwkv, o_swa, w_r, w_s, g)


def _ffn_kernel(h_ref, x1_ref, wg_ref, wu_ref, wd_ref, gf_ref, y_ref, acc_ref):
    j = pl.program_id(1)

    @pl.when(j == 0)
    def _():
        acc_ref[...] = jnp.zeros_like(acc_ref)

    h = h_ref[...]
    gate = jnp.dot(h, wg_ref[...], preferred_element_type=F32)
    up = jnp.dot(h, wu_ref[...], preferred_element_type=F32)
    act = (gate * jax.nn.sigmoid(gate) * up).astype(BF16)
    acc_ref[...] += jnp.dot(act, wd_ref[...], preferred_element_type=F32)

    @pl.when(j == pl.num_programs(1) - 1)
    def _():
        y_ref[...] = _rms(x1_ref[...] + acc_ref[...], gf_ref[...])


def _ffn(h2, x1, w_gate, w_up, w_down, g_final, *, tm, tf):
    m, d = x1.shape
    f = w_gate.shape[1]
    return pl.pallas_call(
        _ffn_kernel,
        grid=(m // tm, f // tf),
        in_specs=[pl.BlockSpec((tm, d), lambda i, j: (i, 0)),
                  pl.BlockSpec((tm, d), lambda i, j: (i, 0)),
                  pl.BlockSpec((d, tf), lambda i, j: (0, j)),
                  pl.BlockSpec((d, tf), lambda i, j: (0, j)),
                  pl.BlockSpec((tf, d), lambda i, j: (j, 0)),
                  pl.BlockSpec((1, d), lambda i, j: (0, 0))],
        out_specs=pl.BlockSpec((tm, d), lambda i, j: (i, 0)),
        out_shape=jax.ShapeDtypeStruct((m, d), F32),
        scratch_shapes=[pltpu.VMEM((tm, d), F32)],
        compiler_params=pltpu.CompilerParams(dimension_semantics=("parallel", "arbitrary"),
                                             vmem_limit_bytes=VMEM_LIMIT),
        name="ffn",
    )(h2, x1, w_gate, w_up, w_down, g_final)


def _group_matrices():
    lane = np.arange(LANES)
    key_head = lane % RWKV_HEADS
    val_head = lane // I_LO
    sum_key = (key_head[:, None] == key_head[None, :]).astype(np.float32)
    sum_val = (val_head[:, None] == val_head[None, :]).astype(np.float32)
    key_to_val = (key_head[:, None] == val_head[None, :]).astype(np.float32)
    return sum_key, sum_val, key_to_val


def _state_to_kernel(s):
    b = s.shape[0]
    s = s.reshape(b, RWKV_HEADS, I_GROUPS, I_LO, J_GROUPS, SUBLANES)
    return s.transpose(0, 4, 2, 5, 1, 3).reshape(b, HEAD_DIM, SUBLANES, LANES)


def _state_from_kernel(s):
    b = s.shape[0]
    s = s.reshape(b, J_GROUPS, I_GROUPS, SUBLANES, RWKV_HEADS, I_LO)
    return s.transpose(0, 4, 2, 5, 1, 3).reshape(b, RWKV_HEADS, HEAD_DIM, HEAD_DIM)


def _shift_to_cols(shift):
    c = RWKV_WIDTH
    return (shift[:, 0:c][:, KEY_PERM], shift[:, c:2 * c][:, KEY_PERM], shift[:, 2 * c:3 * c][:, VAL_PERM],
            jnp.pad(shift[:, 3 * c:], ((0, 0), (0, LORA_PAD - LORA_COLS))))


def _shift_from_proj(rows):
    return jnp.concatenate([rows[:, COL_R:COL_K][:, KEY_INV], rows[:, COL_K:COL_V][:, KEY_INV],
                            rows[:, COL_V:COL_SK][:, VAL_INV], rows[:, COL_L:COL_L + LORA_COLS]], axis=1)


def _layer_weights(norm_attn, w_in, mu_shift, w0, w2, a0, a2, g2, k_k, k_a, r_k, ln_x_w, ln_x_b, sinks, w_out,
                   norm_ffn, w_gate, w_up, w_down, norm_final):
    c = RWKV_WIDTH
    swa0 = RWKV_COLS
    w_proj = jnp.concatenate([
        w_in[:, swa0:swa0 + SWA_WIDTH],
        w_in[:, 0:c][:, KEY_PERM], w_in[:, c:2 * c][:, KEY_PERM], w_in[:, 2 * c:3 * c][:, VAL_PERM],
        w_in[:, swa0 + SWA_WIDTH:swa0 + SWA_WIDTH + 2 * KV_WIDTH],
        jnp.pad(w_in[:, 3 * c:RWKV_COLS], ((0, 0), (0, LORA_PAD - LORA_COLS))),
    ], axis=1).astype(BF16)
    row = lambda v: v.reshape(1, -1)
    mu = _shift_to_cols(mu_shift.reshape(1, -1))
    w2p = jnp.pad(w2[:, KEY_PERM], ((0, LANES - DECAY_LORA), (0, 0)))
    a2p = jnp.pad(a2[:, KEY_PERM], ((DECAY_LORA, 0), (0, 0)))
    g2p = jnp.pad(g2[:, VAL_PERM], ((0, LORA_PAD - LANES - GATE_LORA), (0, 0)))
    sum_key, sum_val, key_to_val = _group_matrices()
    prep_params = (*mu, row(w0[KEY_PERM]), w2p, row(a0[KEY_PERM]), a2p, g2p,
                   row(k_k[KEY_PERM]), row(k_a[KEY_PERM]), jnp.asarray(sum_key))
    post_params = (row(ln_x_w[VAL_PERM]), row(ln_x_b[VAL_PERM]), row(r_k.reshape(-1)[KEY_PERM]),
                   jnp.asarray(sum_val), jnp.asarray(key_to_val))
    return dict(
        norm_attn=row(norm_attn), w_proj=w_proj, prep=prep_params, post=post_params, sinks=sinks,
        w_out_r=w_out[:c][VAL_PERM].astype(BF16), w_out_s=w_out[c:].astype(BF16), norm_ffn=row(norm_ffn),
        w_gate=w_gate.astype(BF16), w_up=w_up.astype(BF16), w_down=w_down.astype(BF16), norm_final=row(norm_final))


def _layer(x, shift_prev, state0, kv_past, lw, *, tiles):
    bsz, t_len, d = x.shape
    m = bsz * t_len
    x2 = x.reshape(m, d)
    proj = _inproj(x2, lw["norm_attn"], lw["w_proj"], tm=tiles["tm_in"], tn=tiles["tn_in"])

    fix_cols = _shift_to_cols(shift_prev)
    if bsz == 1:
        fixes, period = fix_cols, None
    else:
        fixes = tuple(jnp.pad(f[:, None, :], ((0, 0), (0, t_len - 1), (0, 0))).reshape(m, -1) for f in fix_cols)
        period = t_len
    a, w, b, k, r, v, g = _prep(proj, fixes, lw["prep"], tm=tiles["tm_prep"], period=period)

    tok = lambda z: z.reshape(bsz, t_len, SUBLANES, LANES)
    o, s_fin = _rwkv_scan(tok(a), tok(w), tok(b), tok(k), tok(r), tok(v), _state_to_kernel(state0), tb=tiles["tb"])
    o_rwkv = _post(o.reshape(m, RWKV_WIDTH), r, k, v, g, lw["post"], tm=tiles["tm_prep"])

    if kv_past is None:
        o_swa = _attn_prompt(proj, lw["sinks"])
        new_k = proj[m - WINDOW:, COL_SK:COL_SV].reshape(1, WINDOW, SWA_KV_HEADS, HEAD_DIM)
        new_v = proj[m - WINDOW:, COL_SV:COL_L].reshape(1, WINDOW, SWA_KV_HEADS, HEAD_DIM)
    else:
        k_past, v_past = kv_past
        proj3 = proj.reshape(bsz, t_len, PROJ_COLS)
        o_swa = _attn_decode(proj3, k_past.reshape(bsz, WINDOW, KV_WIDTH), v_past.reshape(bsz, WINDOW, KV_WIDTH),
                             lw["sinks"], bt=min(tiles["bt_attn"], bsz)).reshape(m, SWA_WIDTH)
        kv_shape = (bsz, t_len, SWA_KV_HEADS, HEAD_DIM)
        new_k = jnp.concatenate([k_past[:, t_len:], proj3[:, :, COL_SK:COL_SV].reshape(kv_shape)], axis=1)
        new_v = jnp.concatenate([v_past[:, t_len:], proj3[:, :, COL_SV:COL_L].reshape(kv_shape)], axis=1)

    x1, h2 = _outproj(x2, o_rwkv, o_swa, lw["w_out_r"], lw["w_out_s"], lw["norm_ffn"], tm=tiles["tm_out"])
    y = _ffn(h2, x1, lw["w_gate"], lw["w_up"], lw["w_down"], lw["norm_final"], tm=tiles["tm_ffn"], tf=tiles["tf"])

    new_shift = _shift_from_proj(proj.reshape(bsz, t_len, PROJ_COLS)[:, -1])
    return y.reshape(bsz, t_len, d), _state_from_kernel(s_fin), new_shift, new_k, new_v


def _tiles(m, t_len):
    return dict(tm_in=min(512, m), tn_in=1664, tm_prep=min(256, m), tb=min(64, t_len),
                tm_out=min(256, m), tm_ffn=min(512, m), tf=512, bt_attn=16)


def kernel(x_prompt, x_sample, state_rwkv, state_shift, cache_swa_k, cache_swa_v, norm_attn, w_in, mu_shift, w0, w2, a0, a2, g2, k_k, k_a, r_k, ln_x_w, ln_x_b, sinks, w_out, norm_ffn, w_gate, w_up, w_down, norm_final):
    assert norm_attn.shape[0] == 1, "single trunk layer"
    lw = _layer_weights(norm_attn[0], w_in[0], mu_shift[0], w0[0], w2[0], a0[0], a2[0], g2[0], k_k[0], k_a[0],
                        r_k[0], ln_x_w[0], ln_x_b[0], sinks[0], w_out[0], norm_ffn[0], w_gate[0], w_up[0],
                        w_down[0], norm_final)
    bp, tp, _ = x_prompt.shape
    bs, ts, _ = x_sample.shape
    assert bp == 1 and tp % BLOCK == 0 and ts < SUBLANES
    yp, p_state, p_shift, p_k, p_v = _layer(
        x_prompt, jnp.zeros((bp, RWKV_COLS), F32), jnp.zeros((bp, RWKV_HEADS, HEAD_DIM, HEAD_DIM), F32),
        None, lw, tiles=_tiles(bp * tp, tp))
    ys, s_state, s_shift, s_k, s_v = _layer(
        x_sample, state_shift[0], state_rwkv[0], (cache_swa_k[0], cache_swa_v[0]), lw, tiles=_tiles(bs * ts, ts))
    return (yp, ys, p_state[None], p_shift[None], p_k[None], p_v[None],
            s_state[None], s_shift[None], s_k[None], s_v[None])
```

```python
import functools

import numpy as np
import jax
import jax.numpy as jnp
from jax import lax
from jax.experimental import pallas as pl
from jax.experimental.pallas import tpu as pltpu

F32 = jnp.float32
BF16 = jnp.bfloat16

D_MODEL = 2048
HEAD_DIM = 64
RWKV_WIDTH = 1024
RWKV_HEADS = 16
SWA_WIDTH = 1024
SWA_HEADS = 16
SWA_KV_HEADS = 4
SWA_GROUP = 4
KV_WIDTH = 256
WINDOW = 128
BLOCK = 128
DECAY_LORA = 64
AAA_LORA = 64
GATE_LORA = 160
LORA_COLS = DECAY_LORA + AAA_LORA + GATE_LORA
RWKV_COLS = 3 * RWKV_WIDTH + LORA_COLS
D_FF = 5632
RMS_EPS = 1e-5
GN_EPS = 64e-5

SUBLANES = 8
LANES = 128
I_LO = LANES // RWKV_HEADS


def _reorder_last(x, split, a, b):
    return x.reshape(x.shape[:-1] + split).swapaxes(a, b).reshape(x.shape)


def _key_order(x):
    return _reorder_last(x, (RWKV_HEADS, HEAD_DIM), -1, -2)


def _key_natural(x):
    return _reorder_last(x, (HEAD_DIM, RWKV_HEADS), -1, -2)


def _val_order(x):
    return _reorder_last(x, (RWKV_HEADS, HEAD_DIM // I_LO, I_LO), -3, -2)


def _val_natural(x):
    return _reorder_last(x, (HEAD_DIM // I_LO, RWKV_HEADS, I_LO), -3, -2)


J_GROUPS = HEAD_DIM // SUBLANES
I_GROUPS = HEAD_DIM // I_LO


def _sublane_allsum(x):
    for shift in (4, 2, 1):
        x = x + pltpu.roll(x, shift, 0)
    return x


def _sublane_sums(xs):
    sub = lax.broadcasted_iota(jnp.int32, (SUBLANES, LANES), 0)

    def combine(x, y, d):
        clear = (sub & d) == 0
        return jnp.where(clear, x, pltpu.roll(y, d, 0)) + jnp.where(clear, pltpu.roll(x, SUBLANES - d, 0), y)

    z = [combine(xs[0], xs[4], 4), combine(xs[2], xs[6], 4), combine(xs[1], xs[5], 4), combine(xs[3], xs[7], 4)]
    return combine(combine(z[0], z[1], 2), combine(z[2], z[3], 2), 1)


def _scan_kernel(a_ref, w_ref, b_ref, k_ref, r_ref, v_ref, s0_ref, o_ref, sT_ref, s_scr, sa_scr, *tiles, tb):
    ti = pl.program_id(1)

    @pl.when(ti == 0)
    def _():
        s_scr[...] = s0_ref[0]

    gather_idx = (lax.broadcasted_iota(jnp.int32, (SUBLANES, LANES), 0) * RWKV_HEADS
                  + lax.broadcasted_iota(jnp.int32, (SUBLANES, LANES), 1) // I_LO)

    srcs = (a_ref, w_ref, b_ref, k_ref, r_ref)
    tile = lambda qi, parity: tiles[2 * qi + parity]

    def build(qi, t, parity):
        src_t = jnp.minimum(t, tb - 1)
        for jg in range(J_GROUPS):
            src = jnp.broadcast_to(srcs[qi][0, src_t, pl.ds(jg, 1), :], (SUBLANES, LANES))
            tile(qi, parity)[jg] = jnp.take_along_axis(src, gather_idx, axis=1)

    for qi in range(5):
        build(qi, 0, 0)
    build(0, 1, 1)
    sa0 = [None] * I_GROUPS
    for jg in range(J_GROUPS):
        a0 = tile(0, 0)[jg]
        for ig in range(I_GROUPS):
            term = s_scr[jg * I_GROUPS + ig] * a0
            sa0[ig] = term if sa0[ig] is None else sa0[ig] + term
    sa0 = [_sublane_allsum(x) for x in sa0]

    def step(t, parity, sa):
        for qi in range(1, 5):
            build(qi, t + 1, 1 - parity)
        build(0, t + 2, parity)
        vb = [jnp.broadcast_to(v_ref[0, t, pl.ds(ig, 1), :], (SUBLANES, LANES)) for ig in range(I_GROUPS)]
        out = [None] * I_GROUPS
        nxt = [None] * I_GROUPS
        for jg in range(J_GROUPS):
            w, b, k, r = (tile(qi, parity)[jg] for qi in range(1, 5))
            a_next = tile(0, 1 - parity)[jg]
            for ig in range(I_GROUPS):
                idx = jg * I_GROUPS + ig
                sj = s_scr[idx] * w + sa[ig] * b + vb[ig] * k
                s_scr[idx] = sj
                to, tn = sj * r, sj * a_next
                out[ig] = to if out[ig] is None else out[ig] + to
                nxt[ig] = tn if nxt[ig] is None else nxt[ig] + tn
        o_ref[0, t] = _sublane_sums(out)
        sa_scr[parity] = _sublane_sums(nxt)
        return [jnp.broadcast_to(sa_scr[parity, pl.ds(ig, 1), :], (SUBLANES, LANES)) for ig in range(I_GROUPS)]

    def two_steps(u, sa):
        return step(2 * u + 1, 1, step(2 * u, 0, sa))

    lax.fori_loop(0, tb // 2, two_steps, sa0)

    @pl.when(ti == pl.num_programs(1) - 1)
    def _():
        sT_ref[0] = s_scr[...]


def _rwkv_scan(a, w, b, k, r, v, s0, *, tb):
    bsz, t_len = a.shape[0], a.shape[1]
    assert t_len % tb == 0 and tb % 2 == 0
    tok_spec = pl.BlockSpec((1, tb, SUBLANES, LANES), lambda bi, ti: (bi, ti, 0, 0))
    st_spec = pl.BlockSpec((1, HEAD_DIM, SUBLANES, LANES), lambda bi, ti: (bi, 0, 0, 0))
    return pl.pallas_call(
        functools.partial(_scan_kernel, tb=tb),
        grid=(bsz, t_len // tb),
        in_specs=[tok_spec] * 6 + [st_spec],
        out_specs=[tok_spec, st_spec],
        out_shape=[jax.ShapeDtypeStruct(a.shape, F32), jax.ShapeDtypeStruct(s0.shape, F32)],
        scratch_shapes=([pltpu.VMEM((HEAD_DIM, SUBLANES, LANES), F32), pltpu.VMEM((2, SUBLANES, LANES), F32)]
                        + [pltpu.VMEM((J_GROUPS, SUBLANES, LANES), F32)] * 10),
        compiler_params=pltpu.CompilerParams(dimension_semantics=("parallel", "arbitrary"),
                                             vmem_limit_bytes=48 * 1024 * 1024),
        name="rwkv_scan",
    )(a, w, b, k, r, v, s0)


PROJ_COLS = 4992
LORA_PAD = 384
COL_Q, COL_R, COL_K, COL_V, COL_SK, COL_SV, COL_L = 0, 1024, 2048, 3072, 4096, 4352, 4608
VMEM_LIMIT = 56 * 1024 * 1024


def _rms(x, g):
    return x * lax.rsqrt(jnp.mean(x * x, axis=-1, keepdims=True) + RMS_EPS) * g


def _inproj_kernel(x_ref, g_ref, w_ref, o_ref, xn_ref):
    @pl.when(pl.program_id(1) == 0)
    def _():
        xn_ref[...] = _rms(x_ref[...], g_ref[...]).astype(BF16)

    o_ref[...] = jnp.dot(xn_ref[...], w_ref[...], preferred_element_type=F32)


def _inproj(x, g, w, *, tm, tn):
    m, d = x.shape
    n = w.shape[1]
    return pl.pallas_call(
        _inproj_kernel,
        grid=(m // tm, n // tn),
        in_specs=[pl.BlockSpec((tm, d), lambda i, j: (i, 0)),
                  pl.BlockSpec((1, d), lambda i, j: (0, 0)),
                  pl.BlockSpec((d, tn), lambda i, j: (0, j))],
        out_specs=pl.BlockSpec((tm, tn), lambda i, j: (i, j)),
        out_shape=jax.ShapeDtypeStruct((m, n), F32),
        scratch_shapes=[pltpu.VMEM((tm, d), BF16)],
        compiler_params=pltpu.CompilerParams(dimension_semantics=("parallel", "arbitrary"),
                                             vmem_limit_bytes=VMEM_LIMIT),
        name="inproj",
    )(x, g, w)


def _tile_sum(x):
    parts = [x[:, c * LANES:(c + 1) * LANES] for c in range(RWKV_WIDTH // LANES)]
    while len(parts) > 1:
        parts = [parts[i] + parts[i + 1] for i in range(0, len(parts), 2)]
    return parts[0]


def _tile8(x):
    return jnp.concatenate([x] * (RWKV_WIDTH // LANES), axis=1)


def _hdot(x, m):
    return jnp.dot(x, m, preferred_element_type=F32, precision=lax.Precision.HIGHEST)


KV_MU_R, KV_MU_K, KV_W0, KV_A0, KV_KK, KV_KA, KV_RK = range(7)
VV_MU_V, VV_LN_W, VV_LN_B = range(3)


def _prep_kernel(pr_ref, pk_ref, pv_ref, pl_ref, fr_ref, fk_ref, fv_ref, fl_ref,
                 kvec_ref, vvec_ref, mul_ref, wa_ref, g2_ref, sumh_ref,
                 a_out, w_out, b_out, k_out, r_out, v_out, g_out,
                 cr_ref, ck_ref, cv_ref, cl_ref, *, period):
    krow = lambda i: kvec_ref[i:i + 1, :]

    def shifted(p_ref, fix_ref, carry_ref, mu):
        p = p_ref[...]
        rolled = pltpu.roll(p, 1, 0)
        rows = lax.broadcasted_iota(jnp.int32, p.shape, 0)
        if period is None:
            @pl.when(pl.program_id(0) == 0)
            def _():
                carry_ref[...] = fix_ref[...]

            prev = jnp.where(rows == 0, carry_ref[...], rolled)
            carry_ref[...] = p[p.shape[0] - 1:, :]
        else:
            prev = jnp.where(rows % period == 0, fix_ref[...], rolled)
        return p + mu * (prev - p)

    r = shifted(pr_ref, fr_ref, cr_ref, krow(KV_MU_R))
    k = shifted(pk_ref, fk_ref, ck_ref, krow(KV_MU_K))
    v = shifted(pv_ref, fv_ref, cv_ref, vvec_ref[VV_MU_V:VV_MU_V + 1, :])
    xl = shifted(pl_ref, fl_ref, cl_ref, mul_ref[...])

    x_wa = xl[:, :LANES]
    is_w = lax.broadcasted_iota(jnp.int32, x_wa.shape, 1) < DECAY_LORA
    zw = krow(KV_W0) + _hdot(jnp.where(is_w, jnp.tanh(x_wa), 0.0), wa_ref[...])
    za = krow(KV_A0) + _hdot(jnp.where(is_w, 0.0, x_wa), wa_ref[...])
    g = _hdot(jax.nn.sigmoid(xl[:, LANES:]), g2_ref[...])
    y = -zw
    softplus = jnp.maximum(y, 0.0) + jnp.log(1.0 + jnp.exp(-jnp.abs(y)))
    decay = jnp.exp(-jnp.exp(-softplus - 0.5))
    a_sig = jax.nn.sigmoid(za)

    kk = k * krow(KV_KK)
    ssq = _tile8(_hdot(_tile_sum(kk * kk), sumh_ref[...]))
    kk = kk / jnp.maximum(jnp.sqrt(ssq), 1e-12)

    a_out[...] = -kk
    w_out[...] = decay
    b_out[...] = kk * a_sig
    k_out[...] = k * (1.0 + (a_sig - 1.0) * krow(KV_KA))
    r_out[...] = r
    v_out[...] = v
    g_out[...] = g


def _prep(proj, fixes, params, *, tm, period):
    m = proj.shape[0]
    c = RWKV_WIDTH
    grid = (m // tm,)

    def col(width, idx):
        return pl.BlockSpec((tm, width), lambda i: (i, idx))

    def full(arr):
        return pl.BlockSpec(arr.shape, lambda i: (0,) * arr.ndim)

    if period is None:
        fix_specs = [full(f) for f in fixes]
    else:
        fix_specs = [pl.BlockSpec((tm, f.shape[1]), lambda i: (i, 0)) for f in fixes]
    out_spec = pl.BlockSpec((tm, c), lambda i: (i, 0))
    return pl.pallas_call(
        functools.partial(_prep_kernel, period=period),
        grid=grid,
        in_specs=[col(c, COL_R // c), col(c, COL_K // c), col(c, COL_V // c), col(LORA_PAD, COL_L // LORA_PAD)]
        + fix_specs + [full(p) for p in params],
        out_specs=[out_spec] * 7,
        out_shape=[jax.ShapeDtypeStruct((m, c), F32)] * 7,
        scratch_shapes=[pltpu.VMEM((1, c), F32)] * 3 + [pltpu.VMEM((1, LORA_PAD), F32)],
        compiler_params=pltpu.CompilerParams(dimension_semantics=("arbitrary",), vmem_limit_bytes=VMEM_LIMIT),
        name="rwkv_prep",
    )(proj, proj, proj, proj, *fixes, *params)


def _post_kernel(o_ref, r_ref, k_ref, v_ref, g_ref, kvec_ref, vvec_ref, gsum_ref, kv_ref, out_ref):
    o = o_ref[...]
    inv_n = 1.0 / HEAD_DIM
    mean = _tile8(_hdot(_tile_sum(o), gsum_ref[...])) * inv_n
    d = o - mean
    var = _tile8(_hdot(_tile_sum(d * d), gsum_ref[...])) * inv_n
    normed = (d * lax.rsqrt(var + GN_EPS) * vvec_ref[VV_LN_W:VV_LN_W + 1, :]
              + vvec_ref[VV_LN_B:VV_LN_B + 1, :])
    rk = _tile8(_hdot(_tile_sum(r_ref[...] * k_ref[...] * kvec_ref[KV_RK:KV_RK + 1, :]), kv_ref[...]))
    out_ref[...] = ((normed + rk * v_ref[...]) * g_ref[...]).astype(BF16)


def _post(o, r, k, v, g, params, *, tm):
    m, c = o.shape
    tok = pl.BlockSpec((tm, c), lambda i: (i, 0))
    return pl.pallas_call(
        _post_kernel,
        grid=(m // tm,),
        in_specs=[tok] * 5 + [pl.BlockSpec(p.shape, lambda i: (0, 0)) for p in params],
        out_specs=tok,
        out_shape=jax.ShapeDtypeStruct((m, c), BF16),
        compiler_params=pltpu.CompilerParams(dimension_semantics=("parallel",), vmem_limit_bytes=VMEM_LIMIT),
        name="rwkv_post",
    )(o, r, k, v, g, *params)


ALIBI_SLOPES = [2.0 ** (-8.0 * (i + 1) / SWA_HEADS) for i in range(SWA_HEADS)]
SCORE_SCALE = HEAD_DIM ** -0.5


def _attend(qh, ks, vs, slope, sink, distf, valid):
    s = lax.dot_general(qh, ks, (((1,), (1,)), ((), ())), preferred_element_type=F32) * SCORE_SCALE
    s = jnp.where(valid, s - slope * distf, -jnp.inf)
    m = jnp.maximum(jnp.max(s, axis=-1, keepdims=True), sink)
    e = jnp.exp(s - m)
    den = jnp.sum(e, axis=-1, keepdims=True) + jnp.exp(sink - m)
    return jnp.dot((e / den).astype(BF16), vs, preferred_element_type=F32)


def _attn_block_kernel(sink_ref, q_ref, kc_ref, vc_ref, kp_ref, vp_ref, o_ref):
    n = pl.program_id(0)
    span = WINDOW + BLOCK
    qi = lax.broadcasted_iota(jnp.int32, (BLOCK, span), 0)
    si = lax.broadcasted_iota(jnp.int32, (BLOCK, span), 1)
    dist = qi - si + WINDOW
    valid = (dist >= 0) & (dist < WINDOW) & (n * BLOCK - WINDOW + si >= 0)
    distf = dist.astype(F32)
    for kh in range(SWA_KV_HEADS):
        lanes = slice(kh * HEAD_DIM, (kh + 1) * HEAD_DIM)
        ks = jnp.concatenate([kp_ref[:, lanes], kc_ref[:, lanes]], axis=0).astype(BF16)
        vs = jnp.concatenate([vp_ref[:, lanes], vc_ref[:, lanes]], axis=0).astype(BF16)
        for gi in range(SWA_GROUP):
            h = kh * SWA_GROUP + gi
            hl = slice(h * HEAD_DIM, (h + 1) * HEAD_DIM)
            oh = _attend(q_ref[:, hl].astype(BF16), ks, vs, ALIBI_SLOPES[h], sink_ref[h], distf, valid)
            o_ref[:, hl] = oh.astype(BF16)


def _attn_prompt(proj, sinks):
    t = proj.shape[0]
    cur = lambda width, idx: pl.BlockSpec((BLOCK, width), lambda n: (n, idx))
    prev = lambda width, idx: pl.BlockSpec((BLOCK, width), lambda n: (jnp.maximum(n - 1, 0), idx))
    return pl.pallas_call(
        _attn_block_kernel,
        grid=(t // BLOCK,),
        in_specs=[pl.BlockSpec(memory_space=pltpu.SMEM),
                  cur(SWA_WIDTH, COL_Q // SWA_WIDTH), cur(KV_WIDTH, COL_SK // KV_WIDTH), cur(KV_WIDTH, COL_SV // KV_WIDTH),
                  prev(KV_WIDTH, COL_SK // KV_WIDTH), prev(KV_WIDTH, COL_SV // KV_WIDTH)],
        out_specs=pl.BlockSpec((BLOCK, SWA_WIDTH), lambda n: (n, 0)),
        out_shape=jax.ShapeDtypeStruct((t, SWA_WIDTH), BF16),
        compiler_params=pltpu.CompilerParams(dimension_semantics=("parallel",), vmem_limit_bytes=VMEM_LIMIT),
        name="swa_prompt",
    )(sinks, proj, proj, proj, proj, proj)


def _attn_decode_kernel(sink_ref, q_ref, kn_ref, vn_ref, kp_ref, vp_ref, o_ref, ko_ref, vo_ref, *, t_new, bt):
    span = WINDOW + SUBLANES
    qi = lax.broadcasted_iota(jnp.int32, (bt, SUBLANES, span), 1)
    si = lax.broadcasted_iota(jnp.int32, (bt, SUBLANES, span), 2)
    dist = qi - si + WINDOW
    valid = (dist >= 0) & (dist < WINDOW)
    distf = dist.astype(F32)
    pad = jnp.zeros((bt, SUBLANES - t_new, HEAD_DIM), F32)
    for kh in range(SWA_KV_HEADS):
        lanes = slice(kh * HEAD_DIM, (kh + 1) * HEAD_DIM)
        ks = jnp.concatenate([kp_ref[:, :, lanes], kn_ref[:, :, lanes], pad], axis=1).astype(BF16)
        vs = jnp.concatenate([vp_ref[:, :, lanes], vn_ref[:, :, lanes], pad], axis=1).astype(BF16)
        for gi in range(SWA_GROUP):
            h = kh * SWA_GROUP + gi
            hl = slice(h * HEAD_DIM, (h + 1) * HEAD_DIM)
            qh = jnp.concatenate([q_ref[:, :, hl], pad], axis=1).astype(BF16)
            s = jnp.einsum("bqd,bsd->bqs", qh, ks, preferred_element_type=F32) * SCORE_SCALE
            s = jnp.where(valid, s - ALIBI_SLOPES[h] * distf, -jnp.inf)
            sink = sink_ref[h]
            m = jnp.maximum(jnp.max(s, axis=-1, keepdims=True), sink)
            e = jnp.exp(s - m)
            den = jnp.sum(e, axis=-1, keepdims=True) + jnp.exp(sink - m)
            oh = jnp.einsum("bqs,bsd->bqd", (e / den).astype(BF16), vs, preferred_element_type=F32)
            o_ref[:, :, hl] = oh[:, :t_new, :].astype(BF16)
    ko_ref[:, :WINDOW - t_new, :] = kp_ref[:, t_new:, :]
    ko_ref[:, WINDOW - t_new:, :] = kn_ref[...]
    vo_ref[:, :WINDOW - t_new, :] = vp_ref[:, t_new:, :]
    vo_ref[:, WINDOW - t_new:, :] = vn_ref[...]


def _attn_decode(proj3, k_past, v_past, sinks, *, bt):
    bsz, t_new = proj3.shape[0], proj3.shape[1]
    new = lambda width, idx: pl.BlockSpec((bt, t_new, width), lambda b: (b, 0, idx))
    past = pl.BlockSpec((bt, WINDOW, KV_WIDTH), lambda b: (b, 0, 0))
    return pl.pallas_call(
        functools.partial(_attn_decode_kernel, t_new=t_new, bt=bt),
        grid=(bsz // bt,),
        in_specs=[pl.BlockSpec(memory_space=pltpu.SMEM),
                  new(SWA_WIDTH, COL_Q // SWA_WIDTH), new(KV_WIDTH, COL_SK // KV_WIDTH), new(KV_WIDTH, COL_SV // KV_WIDTH),
                  past, past],
        out_specs=[pl.BlockSpec((bt, t_new, SWA_WIDTH), lambda b: (b, 0, 0)), past, past],
        out_shape=[jax.ShapeDtypeStruct((bsz, t_new, SWA_WIDTH), BF16),
                   jax.ShapeDtypeStruct(k_past.shape, F32), jax.ShapeDtypeStruct(v_past.shape, F32)],
        compiler_params=pltpu.CompilerParams(dimension_semantics=("parallel",), vmem_limit_bytes=VMEM_LIMIT),
        name="swa_decode",
    )(sinks, proj3, proj3, proj3, k_past, v_past)


def _outproj_kernel(x_ref, orw_ref, osw_ref, wr_ref, ws_ref, g_ref, x1_ref, h2_ref):
    x1 = (x_ref[...] + jnp.dot(orw_ref[...], wr_ref[...], preferred_element_type=F32)
          + jnp.dot(osw_ref[...], ws_ref[...], preferred_element_type=F32))
    x1_ref[...] = x1
    h2_ref[...] = _rms(x1, g_ref[...]).astype(BF16)


def _outproj(x, o_rwkv, o_swa, w_r, w_s, g, *, tm):
    m, d = x.shape
    row = lambda width: pl.BlockSpec((tm, width), lambda i: (i, 0))
    full = lambda arr: pl.BlockSpec(arr.shape, lambda i: (0, 0))
    return pl.pallas_call(
        _outproj_kernel,
        grid=(m // tm,),
        in_specs=[row(d), row(RWKV_WIDTH), row(SWA_WIDTH), full(w_r), full(w_s), full(g)],
        out_specs=[row(d), row(d)],
        out_shape=[jax.ShapeDtypeStruct((m, d), F32), jax.ShapeDtypeStruct((m, d), BF16)],
        compiler_params=pltpu.CompilerParams(dimension_semantics=("parallel",), vmem_limit_bytes=VMEM_LIMIT),
        name="outproj",
    )(x, o_rwkv, o_swa, w_r, w_s, g)


def _ffn_kernel(h_ref, x1_ref, wg_ref, wu_ref, wd_ref, gf_ref, y_ref, acc_ref):
    j = pl.program_id(1)

    @pl.when(j == 0)
    def _():
        acc_ref[...] = jnp.zeros_like(acc_ref)

    h = h_ref[...]
    gate = jnp.dot(h, wg_ref[...], preferred_element_type=F32)
    up = jnp.dot(h, wu_ref[...], preferred_element_type=F32)
    act = (gate * jax.nn.sigmoid(gate) * up).astype(BF16)
    acc_ref[...] += jnp.dot(act, wd_ref[...], preferred_element_type=F32)

    @pl.when(j == pl.num_programs(1) - 1)
    def _():
        y_ref[...] = _rms(x1_ref[...] + acc_ref[...], gf_ref[...])


def _ffn(h2, x1, w_gate, w_up, w_down, g_final, *, tm, tf):
    m, d = x1.shape
    f = w_gate.shape[1]
    return pl.pallas_call(
        _ffn_kernel,
        grid=(m // tm, f // tf),
        in_specs=[pl.BlockSpec((tm, d), lambda i, j: (i, 0)),
                  pl.BlockSpec((tm, d), lambda i, j: (i, 0)),
                  pl.BlockSpec((d, tf), lambda i, j: (0, j)),
                  pl.BlockSpec((d, tf), lambda i, j: (0, j)),
                  pl.BlockSpec((tf, d), lambda i, j: (j, 0)),
                  pl.BlockSpec((1, d), lambda i, j: (0, 0))],
        out_specs=pl.BlockSpec((tm, d), lambda i, j: (i, 0)),
        out_shape=jax.ShapeDtypeStruct((m, d), F32),
        scratch_shapes=[pltpu.VMEM((tm, d), F32)],
        compiler_params=pltpu.CompilerParams(dimension_semantics=("parallel", "arbitrary"),
                                             vmem_limit_bytes=VMEM_LIMIT),
        name="ffn",
    )(h2, x1, w_gate, w_up, w_down, g_final)


def _group_matrices():
    lane = np.arange(LANES)
    key_head = lane % RWKV_HEADS
    val_head = lane // I_LO
    sum_key = (key_head[:, None] == key_head[None, :]).astype(np.float32)
    sum_val = (val_head[:, None] == val_head[None, :]).astype(np.float32)
    key_to_val = (key_head[:, None] == val_head[None, :]).astype(np.float32)
    return sum_key, sum_val, key_to_val


def _state_to_kernel(s):
    b = s.shape[0]
    s = s.reshape(b, RWKV_HEADS, I_GROUPS, I_LO, J_GROUPS, SUBLANES)
    return s.transpose(0, 4, 2, 5, 1, 3).reshape(b, HEAD_DIM, SUBLANES, LANES)


def _state_from_kernel(s):
    b = s.shape[0]
    s = s.reshape(b, J_GROUPS, I_GROUPS, SUBLANES, RWKV_HEADS, I_LO)
    return s.transpose(0, 4, 2, 5, 1, 3).reshape(b, RWKV_HEADS, HEAD_DIM, HEAD_DIM)


def _shift_to_cols(shift):
    c = RWKV_WIDTH
    return (_key_order(shift[:, 0:c]), _key_order(shift[:, c:2 * c]), _val_order(shift[:, 2 * c:3 * c]),
            jnp.pad(shift[:, 3 * c:], ((0, 0), (0, LORA_PAD - LORA_COLS))))


def _shift_from_proj(rows):
    return jnp.concatenate([_key_natural(rows[:, COL_R:COL_K]), _key_natural(rows[:, COL_K:COL_V]),
                            _val_natural(rows[:, COL_V:COL_SK]), rows[:, COL_L:COL_L + LORA_COLS]], axis=1)


def _layer_weights(norm_attn, w_in, mu_shift, w0, w2, a0, a2, g2, k_k, k_a, r_k, ln_x_w, ln_x_b, sinks, w_out,
                   norm_ffn, w_gate, w_up, w_down, norm_final):
    c = RWKV_WIDTH
    swa0 = RWKV_COLS
    w_proj = jnp.concatenate([
        w_in[:, swa0:swa0 + SWA_WIDTH],
        _key_order(w_in[:, 0:c]), _key_order(w_in[:, c:2 * c]), _val_order(w_in[:, 2 * c:3 * c]),
        w_in[:, swa0 + SWA_WIDTH:swa0 + SWA_WIDTH + 2 * KV_WIDTH],
        jnp.pad(w_in[:, 3 * c:RWKV_COLS], ((0, 0), (0, LORA_PAD - LORA_COLS))),
    ], axis=1).astype(BF16)
    row = lambda v: v.reshape(1, -1)
    zero = jnp.zeros((c,), F32)
    kvec = _key_order(jnp.stack([mu_shift[0:c], mu_shift[c:2 * c], w0, a0, k_k, k_a, r_k.reshape(-1), zero]))
    vvec = _val_order(jnp.stack([mu_shift[2 * c:3 * c], ln_x_w, ln_x_b] + [zero] * 5))
    mu_l = jnp.pad(mu_shift[3 * c:], (0, LORA_PAD - LORA_COLS)).reshape(1, -1)
    lora_wa = _key_order(jnp.concatenate([w2, a2], axis=0))
    g2p = jnp.pad(_val_order(g2), ((0, LORA_PAD - LANES - GATE_LORA), (0, 0)))
    sum_key, sum_val, key_to_val = _group_matrices()
    prep_params = (kvec, vvec, mu_l, lora_wa, g2p, jnp.asarray(sum_key))
    post_params = (kvec, vvec, jnp.asarray(sum_val), jnp.asarray(key_to_val))
    w_out_r = w_out[:c].reshape(RWKV_HEADS, I_GROUPS, I_LO, -1).swapaxes(0, 1).reshape(c, -1)
    return dict(
        norm_attn=row(norm_attn), w_proj=w_proj, prep=prep_params, post=post_params, sinks=sinks,
        w_out_r=w_out_r.astype(BF16), w_out_s=w_out[c:].astype(BF16), norm_ffn=row(norm_ffn),
        w_gate=w_gate.astype(BF16), w_up=w_up.astype(BF16), w_down=w_down.astype(BF16), norm_final=row(norm_final))


def _layer(x, shift_prev, state0, kv_past, lw, *, tiles):
    bsz, t_len, d = x.shape
    m = bsz * t_len
    x2 = x.reshape(m, d)
    proj = _inproj(x2, lw["norm_attn"], lw["w_proj"], tm=tiles["tm_in"], tn=tiles["tn_in"])

    fix_cols = _shift_to_cols(shift_prev)
    if bsz == 1:
        fixes, period = fix_cols, None
    else:
        fixes = tuple(jnp.pad(f[:, None, :], ((0, 0), (0, t_len - 1), (0, 0))).reshape(m, -1) for f in fix_cols)
        period = t_len
    a, w, b, k, r, v, g = _prep(proj, fixes, lw["prep"], tm=tiles["tm_prep"], period=period)

    tok = lambda z: z.reshape(bsz, t_len, SUBLANES, LANES)
    o, s_fin = _rwkv_scan(tok(a), tok(w), tok(b), tok(k), tok(r), tok(v), _state_to_kernel(state0), tb=tiles["tb"])
    o_rwkv = _post(o.reshape(m, RWKV_WIDTH), r, k, v, g, lw["post"], tm=tiles["tm_prep"])

    kv_shape = (bsz, WINDOW, SWA_KV_HEADS, HEAD_DIM)
    if kv_past is None:
        o_swa = _attn_prompt(proj, lw["sinks"])
        new_k = proj[m - WINDOW:, COL_SK:COL_SV].reshape(kv_shape)
        new_v = proj[m - WINDOW:, COL_SV:COL_L].reshape(kv_shape)
    else:
        k_past, v_past = (z.reshape(bsz, WINDOW, KV_WIDTH) for z in kv_past)
        o_swa, new_k, new_v = _attn_decode(proj.reshape(bsz, t_len, PROJ_COLS), k_past, v_past, lw["sinks"],
                                           bt=min(tiles["bt_attn"], bsz))
        o_swa, new_k, new_v = o_swa.reshape(m, SWA_WIDTH), new_k.reshape(kv_shape), new_v.reshape(kv_shape)

    x1, h2 = _outproj(x2, o_rwkv, o_swa, lw["w_out_r"], lw["w_out_s"], lw["norm_ffn"], tm=tiles["tm_out"])
    y = _ffn(h2, x1, lw["w_gate"], lw["w_up"], lw["w_down"], lw["norm_final"], tm=tiles["tm_ffn"], tf=tiles["tf"])

    new_shift = _shift_from_proj(proj.reshape(bsz, t_len, PROJ_COLS)[:, -1])
    return y.reshape(bsz, t_len, d), _state_from_kernel(s_fin), new_shift, new_k, new_v


def _tiles(m, t_len):
    return dict(tm_in=min(1024, m), tn_in=1664, tm_prep=min(256, m), tb=min(64, t_len),
                tm_out=min(256, m), tm_ffn=min(512, m), tf=512, bt_attn=16)


def kernel(x_prompt, x_sample, state_rwkv, state_shift, cache_swa_k, cache_swa_v, norm_attn, w_in, mu_shift, w0, w2, a0, a2, g2, k_k, k_a, r_k, ln_x_w, ln_x_b, sinks, w_out, norm_ffn, w_gate, w_up, w_down, norm_final):
    assert norm_attn.shape[0] == 1, "single trunk layer"
    lw = _layer_weights(norm_attn[0], w_in[0], mu_shift[0], w0[0], w2[0], a0[0], a2[0], g2[0], k_k[0], k_a[0],
                        r_k[0], ln_x_w[0], ln_x_b[0], sinks[0], w_out[0], norm_ffn[0], w_gate[0], w_up[0],
                        w_down[0], norm_final)
    bp, tp, _ = x_prompt.shape
    bs, ts, _ = x_sample.shape
    assert bp == 1 and tp % BLOCK == 0 and ts < SUBLANES
    yp, p_state, p_shift, p_k, p_v = _layer(
        x_prompt, jnp.zeros((bp, RWKV_COLS), F32), jnp.zeros((bp, RWKV_HEADS, HEAD_DIM, HEAD_DIM), F32),
        None, lw, tiles=_tiles(bp * tp, tp))
    ys, s_state, s_shift, s_k, s_v = _layer(
        x_sample, state_shift[0], state_rwkv[0], (cache_swa_k[0], cache_swa_v[0]), lw, tiles=_tiles(bs * ts, ts))
    return (yp, ys, p_state[None], p_shift[None], p_k[None], p_v[None],
            s_state[None], s_shift[None], s_k[None], s_v[None])
```

```python
import functools

import numpy as np
import jax
import jax.numpy as jnp
from jax import lax
from jax.experimental import pallas as pl
from jax.experimental.pallas import tpu as pltpu

F32 = jnp.float32
BF16 = jnp.bfloat16

D_MODEL = 2048
HEAD_DIM = 64
RWKV_WIDTH = 1024
RWKV_HEADS = 16
SWA_WIDTH = 1024
SWA_HEADS = 16
SWA_KV_HEADS = 4
SWA_GROUP = 4
KV_WIDTH = 256
WINDOW = 128
BLOCK = 128
DECAY_LORA = 64
AAA_LORA = 64
GATE_LORA = 160
LORA_COLS = DECAY_LORA + AAA_LORA + GATE_LORA
RWKV_COLS = 3 * RWKV_WIDTH + LORA_COLS
D_FF = 5632
RMS_EPS = 1e-5
GN_EPS = 64e-5

SUBLANES = 8
LANES = 128
I_LO = LANES // RWKV_HEADS


def _reorder_last(x, split, a, b):
    return x.reshape(x.shape[:-1] + split).swapaxes(a, b).reshape(x.shape)


def _key_order(x):
    return _reorder_last(x, (RWKV_HEADS, HEAD_DIM), -1, -2)


def _key_natural(x):
    return _reorder_last(x, (HEAD_DIM, RWKV_HEADS), -1, -2)


def _val_order(x):
    return _reorder_last(x, (RWKV_HEADS, HEAD_DIM // I_LO, I_LO), -3, -2)


def _val_natural(x):
    return _reorder_last(x, (HEAD_DIM // I_LO, RWKV_HEADS, I_LO), -3, -2)


J_GROUPS = HEAD_DIM // SUBLANES
I_GROUPS = HEAD_DIM // I_LO


def _sublane_allsum(x):
    for shift in (4, 2, 1):
        x = x + pltpu.roll(x, shift, 0)
    return x


def _sublane_sums(xs):
    sub = lax.broadcasted_iota(jnp.int32, (SUBLANES, LANES), 0)

    def combine(x, y, d):
        clear = (sub & d) == 0
        return jnp.where(clear, x, pltpu.roll(y, d, 0)) + jnp.where(clear, pltpu.roll(x, SUBLANES - d, 0), y)

    z = [combine(xs[0], xs[4], 4), combine(xs[2], xs[6], 4), combine(xs[1], xs[5], 4), combine(xs[3], xs[7], 4)]
    return combine(combine(z[0], z[1], 2), combine(z[2], z[3], 2), 1)


def _scan_kernel(a_ref, w_ref, b_ref, k_ref, r_ref, v_ref, s0_ref, o_ref, sT_ref, s_scr, sa_scr, *tiles, tb):
    ti = pl.program_id(1)

    @pl.when(ti == 0)
    def _():
        s_scr[...] = s0_ref[0]

    gather_idx = (lax.broadcasted_iota(jnp.int32, (SUBLANES, LANES), 0) * RWKV_HEADS
                  + lax.broadcasted_iota(jnp.int32, (SUBLANES, LANES), 1) // I_LO)

    srcs = (a_ref, w_ref, b_ref, k_ref, r_ref)
    tile = lambda qi, parity: tiles[2 * qi + parity]

    def build(qi, t, parity):
        src_t = jnp.minimum(t, tb - 1)
        for jg in range(J_GROUPS):
            src = jnp.broadcast_to(srcs[qi][0, src_t, pl.ds(jg, 1), :], (SUBLANES, LANES))
            tile(qi, parity)[jg] = jnp.take_along_axis(src, gather_idx, axis=1)

    for qi in range(5):
        build(qi, 0, 0)
    build(0, 1, 1)
    sa0 = [None] * I_GROUPS
    for jg in range(J_GROUPS):
        a0 = tile(0, 0)[jg]
        for ig in range(I_GROUPS):
            term = s_scr[jg * I_GROUPS + ig] * a0
            sa0[ig] = term if sa0[ig] is None else sa0[ig] + term
    sa0 = [_sublane_allsum(x) for x in sa0]

    def step(t, parity, sa):
        for qi in range(1, 5):
            build(qi, t + 1, 1 - parity)
        build(0, t + 2, parity)
        vb = [jnp.broadcast_to(v_ref[0, t, pl.ds(ig, 1), :], (SUBLANES, LANES)) for ig in range(I_GROUPS)]
        out = [None] * I_GROUPS
        nxt = [None] * I_GROUPS
        for jg in range(J_GROUPS):
            w, b, k, r = (tile(qi, parity)[jg] for qi in range(1, 5))
            a_next = tile(0, 1 - parity)[jg]
            for ig in range(I_GROUPS):
                idx = jg * I_GROUPS + ig
                sj = s_scr[idx] * w + sa[ig] * b + vb[ig] * k
                s_scr[idx] = sj
                to, tn = sj * r, sj * a_next
                out[ig] = to if out[ig] is None else out[ig] + to
                nxt[ig] = tn if nxt[ig] is None else nxt[ig] + tn
        o_ref[0, t] = _sublane_sums(out)
        sa_scr[parity] = _sublane_sums(nxt)
        return [jnp.broadcast_to(sa_scr[parity, pl.ds(ig, 1), :], (SUBLANES, LANES)) for ig in range(I_GROUPS)]

    def two_steps(u, sa):
        return step(2 * u + 1, 1, step(2 * u, 0, sa))

    lax.fori_loop(0, tb // 2, two_steps, sa0)

    @pl.when(ti == pl.num_programs(1) - 1)
    def _():
        sT_ref[0] = s_scr[...]


def _rwkv_scan(a, w, b, k, r, v, s0, *, tb):
    bsz, t_len = a.shape[0], a.shape[1]
    assert t_len % tb == 0 and tb % 2 == 0
    tok_spec = pl.BlockSpec((1, tb, SUBLANES, LANES), lambda bi, ti: (bi, ti, 0, 0))
    st_spec = pl.BlockSpec((1, HEAD_DIM, SUBLANES, LANES), lambda bi, ti: (bi, 0, 0, 0))
    return pl.pallas_call(
        functools.partial(_scan_kernel, tb=tb),
        grid=(bsz, t_len // tb),
        in_specs=[tok_spec] * 6 + [st_spec],
        out_specs=[tok_spec, st_spec],
        out_shape=[jax.ShapeDtypeStruct(a.shape, F32), jax.ShapeDtypeStruct(s0.shape, F32)],
        scratch_shapes=([pltpu.VMEM((HEAD_DIM, SUBLANES, LANES), F32), pltpu.VMEM((2, SUBLANES, LANES), F32)]
                        + [pltpu.VMEM((J_GROUPS, SUBLANES, LANES), F32)] * 10),
        compiler_params=pltpu.CompilerParams(dimension_semantics=("parallel", "arbitrary"),
                                             vmem_limit_bytes=48 * 1024 * 1024),
        name="rwkv_scan",
    )(a, w, b, k, r, v, s0)


PROJ_COLS = 4992
LORA_PAD = 384
COL_Q, COL_R, COL_K, COL_V, COL_SK, COL_SV, COL_L = 0, 1024, 2048, 3072, 4096, 4352, 4608
VMEM_LIMIT = 56 * 1024 * 1024


def _rms(x, g):
    return x * lax.rsqrt(jnp.mean(x * x, axis=-1, keepdims=True) + RMS_EPS) * g


def _inproj_kernel(x_ref, g_ref, w_ref, o_ref, xn_ref):
    @pl.when(pl.program_id(1) == 0)
    def _():
        xn_ref[...] = _rms(x_ref[...], g_ref[...]).astype(BF16)

    o_ref[...] = jnp.dot(xn_ref[...], w_ref[...], preferred_element_type=F32)


def _inproj(x, g, w, *, tm, tn):
    m, d = x.shape
    n = w.shape[1]
    return pl.pallas_call(
        _inproj_kernel,
        grid=(m // tm, n // tn),
        in_specs=[pl.BlockSpec((tm, d), lambda i, j: (i, 0)),
                  pl.BlockSpec((1, d), lambda i, j: (0, 0)),
                  pl.BlockSpec((d, tn), lambda i, j: (0, j))],
        out_specs=pl.BlockSpec((tm, tn), lambda i, j: (i, j)),
        out_shape=jax.ShapeDtypeStruct((m, n), F32),
        scratch_shapes=[pltpu.VMEM((tm, d), BF16)],
        compiler_params=pltpu.CompilerParams(dimension_semantics=("parallel", "arbitrary"),
                                             vmem_limit_bytes=VMEM_LIMIT),
        name="inproj",
    )(x, g, w)


def _tile_sum(x):
    parts = [x[:, c * LANES:(c + 1) * LANES] for c in range(RWKV_WIDTH // LANES)]
    while len(parts) > 1:
        parts = [parts[i] + parts[i + 1] for i in range(0, len(parts), 2)]
    return parts[0]


def _tile8(x):
    return jnp.concatenate([x] * (RWKV_WIDTH // LANES), axis=1)


def _group_sum(x, m):
    hi = x.astype(BF16)
    rest = x - hi.astype(F32)
    mid = rest.astype(BF16)
    lo = (rest - mid.astype(F32)).astype(BF16)
    dot = lambda a: jnp.dot(a, m, preferred_element_type=F32)
    return dot(hi) + dot(mid) + dot(lo)


def _lora_dot(x, w):
    return jnp.dot(x.astype(BF16), w, preferred_element_type=F32)


KV_MU_R, KV_MU_K, KV_W0, KV_A0, KV_KK, KV_KA, KV_RK = range(7)
VV_MU_V, VV_LN_W, VV_LN_B = range(3)


def _prep_kernel(pr_ref, pk_ref, pv_ref, pl_ref, fr_ref, fk_ref, fv_ref, fl_ref,
                 kvec_ref, vvec_ref, mul_ref, wa_ref, g2_ref, sumh_ref,
                 a_out, w_out, b_out, k_out, r_out, v_out, g_out,
                 cr_ref, ck_ref, cv_ref, cl_ref, *, period):
    krow = lambda i: kvec_ref[i:i + 1, :]

    def shifted(p_ref, fix_ref, carry_ref, mu):
        p = p_ref[...]
        rolled = pltpu.roll(p, 1, 0)
        rows = lax.broadcasted_iota(jnp.int32, p.shape, 0)
        if period is None:
            @pl.when(pl.program_id(0) == 0)
            def _():
                carry_ref[...] = fix_ref[...]

            prev = jnp.where(rows == 0, carry_ref[...], rolled)
            carry_ref[...] = p[p.shape[0] - 1:, :]
        else:
            prev = jnp.where(rows % period == 0, fix_ref[...], rolled)
        return p + mu * (prev - p)

    r = shifted(pr_ref, fr_ref, cr_ref, krow(KV_MU_R))
    k = shifted(pk_ref, fk_ref, ck_ref, krow(KV_MU_K))
    v = shifted(pv_ref, fv_ref, cv_ref, vvec_ref[VV_MU_V:VV_MU_V + 1, :])
    xl = shifted(pl_ref, fl_ref, cl_ref, mul_ref[...])

    x_wa = xl[:, :LANES]
    is_w = lax.broadcasted_iota(jnp.int32, x_wa.shape, 1) < DECAY_LORA
    zw = krow(KV_W0) + _lora_dot(jnp.where(is_w, jnp.tanh(x_wa), 0.0), wa_ref[...])
    za = krow(KV_A0) + _lora_dot(jnp.where(is_w, 0.0, x_wa), wa_ref[...])
    g = _lora_dot(jax.nn.sigmoid(xl[:, LANES:]), g2_ref[...])
    y = -zw
    softplus = jnp.maximum(y, 0.0) + jnp.log(1.0 + jnp.exp(-jnp.abs(y)))
    decay = jnp.exp(-jnp.exp(-softplus - 0.5))
    a_sig = jax.nn.sigmoid(za)

    kk = k * krow(KV_KK)
    ssq = _tile8(_group_sum(_tile_sum(kk * kk), sumh_ref[...]))
    kk = kk / jnp.maximum(jnp.sqrt(ssq), 1e-12)

    a_out[...] = -kk
    w_out[...] = decay
    b_out[...] = kk * a_sig
    k_out[...] = k * (1.0 + (a_sig - 1.0) * krow(KV_KA))
    r_out[...] = r
    v_out[...] = v
    g_out[...] = g


def _prep(proj, fixes, params, *, tm, period):
    m = proj.shape[0]
    c = RWKV_WIDTH
    grid = (m // tm,)

    def col(width, idx):
        return pl.BlockSpec((tm, width), lambda i: (i, idx))

    def full(arr):
        return pl.BlockSpec(arr.shape, lambda i: (0,) * arr.ndim)

    if period is None:
        fix_specs = [full(f) for f in fixes]
    else:
        fix_specs = [pl.BlockSpec((tm, f.shape[1]), lambda i: (i, 0)) for f in fixes]
    out_spec = pl.BlockSpec((tm, c), lambda i: (i, 0))
    return pl.pallas_call(
        functools.partial(_prep_kernel, period=period),
        grid=grid,
        in_specs=[col(c, COL_R // c), col(c, COL_K // c), col(c, COL_V // c), col(LORA_PAD, COL_L // LORA_PAD)]
        + fix_specs + [full(p) for p in params],
        out_specs=[out_spec] * 7,
        out_shape=[jax.ShapeDtypeStruct((m, c), F32)] * 7,
        scratch_shapes=[pltpu.VMEM((1, c), F32)] * 3 + [pltpu.VMEM((1, LORA_PAD), F32)],
        compiler_params=pltpu.CompilerParams(dimension_semantics=("arbitrary",), vmem_limit_bytes=VMEM_LIMIT),
        name="rwkv_prep",
    )(proj, proj, proj, proj, *fixes, *params)


def _post_kernel(o_ref, r_ref, k_ref, v_ref, g_ref, kvec_ref, vvec_ref, gsum_ref, kv_ref, out_ref):
    o = o_ref[...]
    inv_n = 1.0 / HEAD_DIM
    mean = _tile8(_group_sum(_tile_sum(o), gsum_ref[...])) * inv_n
    d = o - mean
    var = _tile8(_group_sum(_tile_sum(d * d), gsum_ref[...])) * inv_n
    normed = (d * lax.rsqrt(var + GN_EPS) * vvec_ref[VV_LN_W:VV_LN_W + 1, :]
              + vvec_ref[VV_LN_B:VV_LN_B + 1, :])
    rk = _tile8(_group_sum(_tile_sum(r_ref[...] * k_ref[...] * kvec_ref[KV_RK:KV_RK + 1, :]), kv_ref[...]))
    out_ref[...] = ((normed + rk * v_ref[...]) * g_ref[...]).astype(BF16)


def _post(o, r, k, v, g, params, *, tm):
    m, c = o.shape
    tok = pl.BlockSpec((tm, c), lambda i: (i, 0))
    return pl.pallas_call(
        _post_kernel,
        grid=(m // tm,),
        in_specs=[tok] * 5 + [pl.BlockSpec(p.shape, lambda i: (0, 0)) for p in params],
        out_specs=tok,
        out_shape=jax.ShapeDtypeStruct((m, c), BF16),
        compiler_params=pltpu.CompilerParams(dimension_semantics=("parallel",), vmem_limit_bytes=VMEM_LIMIT),
        name="rwkv_post",
    )(o, r, k, v, g, *params)


ALIBI_SLOPES = [2.0 ** (-8.0 * (i + 1) / SWA_HEADS) for i in range(SWA_HEADS)]
SCORE_SCALE = HEAD_DIM ** -0.5


LOG2E = 1.4426950408889634


def _attn_block_kernel(sink_ref, q_ref, kc_ref, vc_ref, kp_ref, vp_ref, o_ref, ot_scr, bias_scr, *, sub_blocks):
    n = pl.program_id(0)
    span = WINDOW + BLOCK

    @pl.when(n == 0)
    def _():
        si = lax.broadcasted_iota(jnp.int32, (span, BLOCK), 0)
        qi = lax.broadcasted_iota(jnp.int32, (span, BLOCK), 1)
        dist = qi - si + WINDOW
        valid = (dist >= 0) & (dist < WINDOW)
        distf = dist.astype(F32)
        for h in range(SWA_HEADS):
            bias = jnp.where(valid, (-ALIBI_SLOPES[h] * LOG2E) * distf, -jnp.inf)
            bias_scr[1, h] = bias
            bias_scr[0, h] = jnp.where(si >= WINDOW, bias, -jnp.inf)

    for sb in range(sub_blocks):
        rows = slice(sb * BLOCK, (sb + 1) * BLOCK)
        if sb == 0:
            k_prev, v_prev = kp_ref[...], vp_ref[...]
            table = jnp.where(n == 0, 0, 1)
        else:
            prev_rows = slice((sb - 1) * BLOCK, sb * BLOCK)
            k_prev, v_prev = kc_ref[prev_rows, :], vc_ref[prev_rows, :]
            table = 1
        k_span = jnp.concatenate([k_prev, kc_ref[rows, :]], axis=0)
        v_t = jnp.concatenate([v_prev, vc_ref[rows, :]], axis=0).T.astype(BF16)
        for kh in range(SWA_KV_HEADS):
            ks = k_span[:, kh * HEAD_DIM:(kh + 1) * HEAD_DIM].astype(BF16)
            heads = range(kh * SWA_GROUP, (kh + 1) * SWA_GROUP)
            qg = jnp.concatenate([q_ref[rows, h * HEAD_DIM:(h + 1) * HEAD_DIM] for h in heads], axis=0).astype(BF16)
            s_all = lax.dot_general(ks, qg, (((1,), (1,)), ((), ())), preferred_element_type=F32)
            probs, dens = [], []
            for gi, h in enumerate(heads):
                s = s_all[:, gi * BLOCK:(gi + 1) * BLOCK] * (SCORE_SCALE * LOG2E) + bias_scr[table, h]
                sink = sink_ref[h] * LOG2E
                m = jnp.maximum(jnp.max(s, axis=0, keepdims=True), sink)
                e = jnp.exp2(s - m)
                dens.append(jnp.sum(e, axis=0, keepdims=True) + jnp.exp2(sink - m))
                probs.append(e.astype(BF16))
            o_t = jnp.dot(v_t[kh * HEAD_DIM:(kh + 1) * HEAD_DIM, :], jnp.concatenate(probs, axis=1),
                          preferred_element_type=F32) / jnp.concatenate(dens, axis=1)
            for gi, h in enumerate(heads):
                ot_scr[sb, h * HEAD_DIM:(h + 1) * HEAD_DIM, :] = o_t[:, gi * BLOCK:(gi + 1) * BLOCK]
        o_ref[rows, :] = ot_scr[sb].T.astype(BF16)


def _attn_prompt(proj, sinks, *, sub_blocks):
    t = proj.shape[0]
    tq = sub_blocks * BLOCK
    cur = lambda width, idx: pl.BlockSpec((tq, width), lambda n: (n, idx))
    prev = lambda width, idx: pl.BlockSpec((BLOCK, width), lambda n: (jnp.maximum(n * sub_blocks - 1, 0), idx))
    return pl.pallas_call(
        functools.partial(_attn_block_kernel, sub_blocks=sub_blocks),
        grid=(t // tq,),
        in_specs=[pl.BlockSpec(memory_space=pltpu.SMEM),
                  cur(SWA_WIDTH, COL_Q // SWA_WIDTH), cur(KV_WIDTH, COL_SK // KV_WIDTH), cur(KV_WIDTH, COL_SV // KV_WIDTH),
                  prev(KV_WIDTH, COL_SK // KV_WIDTH), prev(KV_WIDTH, COL_SV // KV_WIDTH)],
        out_specs=pl.BlockSpec((tq, SWA_WIDTH), lambda n: (n, 0)),
        out_shape=jax.ShapeDtypeStruct((t, SWA_WIDTH), BF16),
        scratch_shapes=[pltpu.VMEM((sub_blocks, SWA_WIDTH, BLOCK), F32),
                        pltpu.VMEM((2, SWA_HEADS, WINDOW + BLOCK, BLOCK), F32)],
        compiler_params=pltpu.CompilerParams(dimension_semantics=("arbitrary",), vmem_limit_bytes=VMEM_LIMIT),
        name="swa_prompt",
    )(sinks, proj, proj, proj, proj, proj)


def _attn_decode_kernel(sink_ref, q_ref, kn_ref, vn_ref, kp_ref, vp_ref, o_ref, ko_ref, vo_ref, *, t_new, bt):
    span = WINDOW + SUBLANES
    qi = lax.broadcasted_iota(jnp.int32, (bt, SUBLANES, span), 1)
    si = lax.broadcasted_iota(jnp.int32, (bt, SUBLANES, span), 2)
    dist = qi - si + WINDOW
    valid = (dist >= 0) & (dist < WINDOW)
    distf = dist.astype(F32)
    pad = jnp.zeros((bt, SUBLANES - t_new, HEAD_DIM), F32)
    for kh in range(SWA_KV_HEADS):
        lanes = slice(kh * HEAD_DIM, (kh + 1) * HEAD_DIM)
        ks = jnp.concatenate([kp_ref[:, :, lanes], kn_ref[:, :, lanes], pad], axis=1).astype(BF16)
        vs = jnp.concatenate([vp_ref[:, :, lanes], vn_ref[:, :, lanes], pad], axis=1).astype(BF16)
        for gi in range(SWA_GROUP):
            h = kh * SWA_GROUP + gi
            hl = slice(h * HEAD_DIM, (h + 1) * HEAD_DIM)
            qh = jnp.concatenate([q_ref[:, :, hl], pad], axis=1).astype(BF16)
            s = jnp.einsum("bqd,bsd->bqs", qh, ks, preferred_element_type=F32) * SCORE_SCALE
            s = jnp.where(valid, s - ALIBI_SLOPES[h] * distf, -jnp.inf)
            sink = sink_ref[h]
            m = jnp.maximum(jnp.max(s, axis=-1, keepdims=True), sink)
            e = jnp.exp(s - m)
            den = jnp.sum(e, axis=-1, keepdims=True) + jnp.exp(sink - m)
            oh = jnp.einsum("bqs,bsd->bqd", (e / den).astype(BF16), vs, preferred_element_type=F32)
            o_ref[:, :, hl] = oh[:, :t_new, :].astype(BF16)
    ko_ref[:, :WINDOW - t_new, :] = kp_ref[:, t_new:, :]
    ko_ref[:, WINDOW - t_new:, :] = kn_ref[...]
    vo_ref[:, :WINDOW - t_new, :] = vp_ref[:, t_new:, :]
    vo_ref[:, WINDOW - t_new:, :] = vn_ref[...]


def _attn_decode(proj3, k_past, v_past, sinks, *, bt):
    bsz, t_new = proj3.shape[0], proj3.shape[1]
    new = lambda width, idx: pl.BlockSpec((bt, t_new, width), lambda b: (b, 0, idx))
    past = pl.BlockSpec((bt, WINDOW, KV_WIDTH), lambda b: (b, 0, 0))
    return pl.pallas_call(
        functools.partial(_attn_decode_kernel, t_new=t_new, bt=bt),
        grid=(bsz // bt,),
        in_specs=[pl.BlockSpec(memory_space=pltpu.SMEM),
                  new(SWA_WIDTH, COL_Q // SWA_WIDTH), new(KV_WIDTH, COL_SK // KV_WIDTH), new(KV_WIDTH, COL_SV // KV_WIDTH),
                  past, past],
        out_specs=[pl.BlockSpec((bt, t_new, SWA_WIDTH), lambda b: (b, 0, 0)), past, past],
        out_shape=[jax.ShapeDtypeStruct((bsz, t_new, SWA_WIDTH), BF16),
                   jax.ShapeDtypeStruct(k_past.shape, F32), jax.ShapeDtypeStruct(v_past.shape, F32)],
        compiler_params=pltpu.CompilerParams(dimension_semantics=("parallel",), vmem_limit_bytes=VMEM_LIMIT),
        name="swa_decode",
    )(sinks, proj3, proj3, proj3, k_past, v_past)


def _outproj_kernel(x_ref, orw_ref, osw_ref, wr_ref, ws_ref, g_ref, x1_ref, h2_ref):
    x1 = (x_ref[...] + jnp.dot(orw_ref[...], wr_ref[...], preferred_element_type=F32)
          + jnp.dot(osw_ref[...], ws_ref[...], preferred_element_type=F32))
    x1_ref[...] = x1
    h2_ref[...] = _rms(x1, g_ref[...]).astype(BF16)


def _outproj(x, o_rwkv, o_swa, w_r, w_s, g, *, tm):
    m, d = x.shape
    row = lambda width: pl.BlockSpec((tm, width), lambda i: (i, 0))
    full = lambda arr: pl.BlockSpec(arr.shape, lambda i: (0, 0))
    return pl.pallas_call(
        _outproj_kernel,
        grid=(m // tm,),
        in_specs=[row(d), row(RWKV_WIDTH), row(SWA_WIDTH), full(w_r), full(w_s), full(g)],
        out_specs=[row(d), row(d)],
        out_shape=[jax.ShapeDtypeStruct((m, d), F32), jax.ShapeDtypeStruct((m, d), BF16)],
        compiler_params=pltpu.CompilerParams(dimension_semantics=("parallel",), vmem_limit_bytes=VMEM_LIMIT),
        name="outproj",
    )(x, o_rwkv, o_swa, w_r, w_s, g)


def _ffn_kernel(h_ref, x1_ref, wg_ref, wu_ref, wd_ref, gf_ref, y_ref, acc_ref):
    j = pl.program_id(1)

    @pl.when(j == 0)
    def _():
        acc_ref[...] = jnp.zeros_like(acc_ref)

    h = h_ref[...]
    gate = jnp.dot(h, wg_ref[...], preferred_element_type=F32)
    up = jnp.dot(h, wu_ref[...], preferred_element_type=F32)
    act = (gate * jax.nn.sigmoid(gate) * up).astype(BF16)
    acc_ref[...] += jnp.dot(act, wd_ref[...], preferred_element_type=F32)

    @pl.when(j == pl.num_programs(1) - 1)
    def _():
        y_ref[...] = _rms(x1_ref[...] + acc_ref[...], gf_ref[...])


def _ffn(h2, x1, w_gate, w_up, w_down, g_final, *, tm, tf):
    m, d = x1.shape
    f = w_gate.shape[1]
    return pl.pallas_call(
        _ffn_kernel,
        grid=(m // tm, f // tf),
        in_specs=[pl.BlockSpec((tm, d), lambda i, j: (i, 0)),
                  pl.BlockSpec((tm, d), lambda i, j: (i, 0)),
                  pl.BlockSpec((d, tf), lambda i, j: (0, j)),
                  pl.BlockSpec((d, tf), lambda i, j: (0, j)),
                  pl.BlockSpec((tf, d), lambda i, j: (j, 0)),
                  pl.BlockSpec((1, d), lambda i, j: (0, 0))],
        out_specs=pl.BlockSpec((tm, d), lambda i, j: (i, 0)),
        out_shape=jax.ShapeDtypeStruct((m, d), F32),
        scratch_shapes=[pltpu.VMEM((tm, d), F32)],
        compiler_params=pltpu.CompilerParams(dimension_semantics=("parallel", "arbitrary"),
                                             vmem_limit_bytes=VMEM_LIMIT),
        name="ffn",
    )(h2, x1, w_gate, w_up, w_down, g_final)


def _group_matrices():
    lane = np.arange(LANES)
    key_head = lane % RWKV_HEADS
    val_head = lane // I_LO
    sum_key = (key_head[:, None] == key_head[None, :]).astype(np.float32)
    sum_val = (val_head[:, None] == val_head[None, :]).astype(np.float32)
    key_to_val = (key_head[:, None] == val_head[None, :]).astype(np.float32)
    return sum_key, sum_val, key_to_val


def _state_to_kernel(s):
    b = s.shape[0]
    s = s.reshape(b, RWKV_HEADS, I_GROUPS, I_LO, J_GROUPS, SUBLANES)
    return s.transpose(0, 4, 2, 5, 1, 3).reshape(b, HEAD_DIM, SUBLANES, LANES)


def _state_from_kernel(s):
    b = s.shape[0]
    s = s.reshape(b, J_GROUPS, I_GROUPS, SUBLANES, RWKV_HEADS, I_LO)
    return s.transpose(0, 4, 2, 5, 1, 3).reshape(b, RWKV_HEADS, HEAD_DIM, HEAD_DIM)


def _shift_to_cols(shift):
    c = RWKV_WIDTH
    return (_key_order(shift[:, 0:c]), _key_order(shift[:, c:2 * c]), _val_order(shift[:, 2 * c:3 * c]),
            jnp.pad(shift[:, 3 * c:], ((0, 0), (0, LORA_PAD - LORA_COLS))))


def _shift_from_proj(rows):
    return jnp.concatenate([_key_natural(rows[:, COL_R:COL_K]), _key_natural(rows[:, COL_K:COL_V]),
                            _val_natural(rows[:, COL_V:COL_SK]), rows[:, COL_L:COL_L + LORA_COLS]], axis=1)


def _layer_weights(norm_attn, w_in, mu_shift, w0, w2, a0, a2, g2, k_k, k_a, r_k, ln_x_w, ln_x_b, sinks, w_out,
                   norm_ffn, w_gate, w_up, w_down, norm_final):
    c = RWKV_WIDTH
    swa0 = RWKV_COLS
    w_proj = jnp.concatenate([
        w_in[:, swa0:swa0 + SWA_WIDTH],
        _key_order(w_in[:, 0:c]), _key_order(w_in[:, c:2 * c]), _val_order(w_in[:, 2 * c:3 * c]),
        w_in[:, swa0 + SWA_WIDTH:swa0 + SWA_WIDTH + 2 * KV_WIDTH],
        jnp.pad(w_in[:, 3 * c:RWKV_COLS], ((0, 0), (0, LORA_PAD - LORA_COLS))),
    ], axis=1).astype(BF16)
    row = lambda v: v.reshape(1, -1)
    zero = jnp.zeros((c,), F32)
    kvec = _key_order(jnp.stack([mu_shift[0:c], mu_shift[c:2 * c], w0, a0, k_k, k_a, r_k.reshape(-1), zero]))
    vvec = _val_order(jnp.stack([mu_shift[2 * c:3 * c], ln_x_w, ln_x_b] + [zero] * 5))
    mu_l = jnp.pad(mu_shift[3 * c:], (0, LORA_PAD - LORA_COLS)).reshape(1, -1)
    lora_wa = _key_order(jnp.concatenate([w2, a2], axis=0)).astype(BF16)
    g2p = jnp.pad(_val_order(g2), ((0, LORA_PAD - LANES - GATE_LORA), (0, 0))).astype(BF16)
    sum_key, sum_val, key_to_val = _group_matrices()
    prep_params = (kvec, vvec, mu_l, lora_wa, g2p, jnp.asarray(sum_key, BF16))
    post_params = (kvec, vvec, jnp.asarray(sum_val, BF16), jnp.asarray(key_to_val, BF16))
    w_out_r = w_out[:c].reshape(RWKV_HEADS, I_GROUPS, I_LO, -1).swapaxes(0, 1).reshape(c, -1)
    return dict(
        norm_attn=row(norm_attn), w_proj=w_proj, prep=prep_params, post=post_params, sinks=sinks,
        w_out_r=w_out_r.astype(BF16), w_out_s=w_out[c:].astype(BF16), norm_ffn=row(norm_ffn),
        w_gate=w_gate.astype(BF16), w_up=w_up.astype(BF16), w_down=w_down.astype(BF16), norm_final=row(norm_final))


def _layer(x, shift_prev, state0, kv_past, lw, *, tiles):
    bsz, t_len, d = x.shape
    m = bsz * t_len
    x2 = x.reshape(m, d)
    proj = _inproj(x2, lw["norm_attn"], lw["w_proj"], tm=tiles["tm_in"], tn=tiles["tn_in"])

    fix_cols = _shift_to_cols(shift_prev)
    if bsz == 1:
        fixes, period = fix_cols, None
    else:
        fixes = tuple(jnp.pad(f[:, None, :], ((0, 0), (0, t_len - 1), (0, 0))).reshape(m, -1) for f in fix_cols)
        period = t_len
    a, w, b, k, r, v, g = _prep(proj, fixes, lw["prep"], tm=tiles["tm_prep"], period=period)

    tok = lambda z: z.reshape(bsz, t_len, SUBLANES, LANES)
    o, s_fin = _rwkv_scan(tok(a), tok(w), tok(b), tok(k), tok(r), tok(v), _state_to_kernel(state0), tb=tiles["tb"])
    o_rwkv = _post(o.reshape(m, RWKV_WIDTH), r, k, v, g, lw["post"], tm=tiles["tm_post"])

    kv_shape = (bsz, WINDOW, SWA_KV_HEADS, HEAD_DIM)
    if kv_past is None:
        o_swa = _attn_prompt(proj, lw["sinks"], sub_blocks=tiles["attn_sub"])
        new_k = proj[m - WINDOW:, COL_SK:COL_SV].reshape(kv_shape)
        new_v = proj[m - WINDOW:, COL_SV:COL_L].reshape(kv_shape)
    else:
        k_past, v_past = (z.reshape(bsz, WINDOW, KV_WIDTH) for z in kv_past)
        o_swa, new_k, new_v = _attn_decode(proj.reshape(bsz, t_len, PROJ_COLS), k_past, v_past, lw["sinks"],
                                           bt=min(tiles["bt_attn"], bsz))
        o_swa, new_k, new_v = o_swa.reshape(m, SWA_WIDTH), new_k.reshape(kv_shape), new_v.reshape(kv_shape)

    x1, h2 = _outproj(x2, o_rwkv, o_swa, lw["w_out_r"], lw["w_out_s"], lw["norm_ffn"], tm=tiles["tm_out"])
    y = _ffn(h2, x1, lw["w_gate"], lw["w_up"], lw["w_down"], lw["norm_final"], tm=tiles["tm_ffn"], tf=tiles["tf"])

    new_shift = _shift_from_proj(proj.reshape(bsz, t_len, PROJ_COLS)[:, -1])
    return y.reshape(bsz, t_len, d), _state_from_kernel(s_fin), new_shift, new_k, new_v


def _tiles(m, t_len):
    return dict(tm_in=min(1024, m), tn_in=1664, tm_prep=min(256, m), tb=min(64, t_len),
                tm_post=min(512, m), tm_out=min(256, m), tm_ffn=min(512, m), tf=512, bt_attn=16, attn_sub=2)


def kernel(x_prompt, x_sample, state_rwkv, state_shift, cache_swa_k, cache_swa_v, norm_attn, w_in, mu_shift, w0, w2, a0, a2, g2, k_k, k_a, r_k, ln_x_w, ln_x_b, sinks, w_out, norm_ffn, w_gate, w_up, w_down, norm_final):
    assert norm_attn.shape[0] == 1, "single trunk layer"
    lw = _layer_weights(norm_attn[0], w_in[0], mu_shift[0], w0[0], w2[0], a0[0], a2[0], g2[0], k_k[0], k_a[0],
                        r_k[0], ln_x_w[0], ln_x_b[0], sinks[0], w_out[0], norm_ffn[0], w_gate[0], w_up[0],
                        w_down[0], norm_final)
    bp, tp, _ = x_prompt.shape
    bs, ts, _ = x_sample.shape
    assert bp == 1 and tp % BLOCK == 0 and ts < SUBLANES
    yp, p_state, p_shift, p_k, p_v = _layer(
        x_prompt, jnp.zeros((bp, RWKV_COLS), F32), jnp.zeros((bp, RWKV_HEADS, HEAD_DIM, HEAD_DIM), F32),
        None, lw, tiles=_tiles(bp * tp, tp))
    ys, s_state, s_shift, s_k, s_v = _layer(
        x_sample, state_shift[0], state_rwkv[0], (cache_swa_k[0], cache_swa_v[0]), lw, tiles=_tiles(bs * ts, ts))
    return (yp, ys, p_state[None], p_shift[None], p_k[None], p_v[None],
            s_state[None], s_shift[None], s_k[None], s_v[None])
```

```python
import functools

import numpy as np
import jax
import jax.numpy as jnp
from jax import lax
from jax.experimental import pallas as pl
from jax.experimental.pallas import tpu as pltpu

F32 = jnp.float32
BF16 = jnp.bfloat16

D_MODEL = 2048
HEAD_DIM = 64
RWKV_WIDTH = 1024
RWKV_HEADS = 16
SWA_WIDTH = 1024
SWA_HEADS = 16
SWA_KV_HEADS = 4
SWA_GROUP = 4
KV_WIDTH = 256
WINDOW = 128
BLOCK = 128
DECAY_LORA = 64
AAA_LORA = 64
GATE_LORA = 160
LORA_COLS = DECAY_LORA + AAA_LORA + GATE_LORA
RWKV_COLS = 3 * RWKV_WIDTH + LORA_COLS
D_FF = 5632
RMS_EPS = 1e-5
GN_EPS = 64e-5

SUBLANES = 8
LANES = 128
I_LO = LANES // RWKV_HEADS


def _reorder_last(x, split, a, b):
    return x.reshape(x.shape[:-1] + split).swapaxes(a, b).reshape(x.shape)


def _key_order(x):
    return _reorder_last(x, (RWKV_HEADS, HEAD_DIM), -1, -2)


def _key_natural(x):
    return _reorder_last(x, (HEAD_DIM, RWKV_HEADS), -1, -2)


def _val_order(x):
    return _reorder_last(x, (RWKV_HEADS, HEAD_DIM // I_LO, I_LO), -3, -2)


def _val_natural(x):
    return _reorder_last(x, (HEAD_DIM // I_LO, RWKV_HEADS, I_LO), -3, -2)


J_GROUPS = HEAD_DIM // SUBLANES
I_GROUPS = HEAD_DIM // I_LO


def _sublane_allsum(x):
    for shift in (4, 2, 1):
        x = x + pltpu.roll(x, shift, 0)
    return x


def _sublane_sums(xs):
    sub = lax.broadcasted_iota(jnp.int32, (SUBLANES, LANES), 0)

    def combine(x, y, d):
        clear = (sub & d) == 0
        return jnp.where(clear, x, pltpu.roll(y, d, 0)) + jnp.where(clear, pltpu.roll(x, SUBLANES - d, 0), y)

    z = [combine(xs[0], xs[4], 4), combine(xs[2], xs[6], 4), combine(xs[1], xs[5], 4), combine(xs[3], xs[7], 4)]
    return combine(combine(z[0], z[1], 2), combine(z[2], z[3], 2), 1)


def _scan_kernel(a_ref, w_ref, b_ref, k_ref, r_ref, v_ref, s0_ref, o_ref, sT_ref, s_scr, sa_scr, *tiles, tb, bb):
    ti = pl.program_id(1)

    @pl.when(ti == 0)
    def _():
        s_scr[...] = s0_ref[...]

    gather_idx = (lax.broadcasted_iota(jnp.int32, (SUBLANES, LANES), 0) * RWKV_HEADS
                  + lax.broadcasted_iota(jnp.int32, (SUBLANES, LANES), 1) // I_LO)

    srcs = (a_ref, w_ref, b_ref, k_ref, r_ref)
    tile = lambda bi, qi, parity: tiles[2 * qi + parity].at[bi]

    def build(bi, qi, t, parity):
        src_t = jnp.minimum(t, tb - 1)
        for jg in range(J_GROUPS):
            src = jnp.broadcast_to(srcs[qi][bi, src_t, pl.ds(jg, 1), :], (SUBLANES, LANES))
            tile(bi, qi, parity)[jg] = jnp.take_along_axis(src, gather_idx, axis=1)

    def first_sa(bi):
        for qi in range(5):
            build(bi, qi, 0, 0)
        build(bi, 0, 1, 1)
        sa0 = [None] * I_GROUPS
        for jg in range(J_GROUPS):
            a0 = tile(bi, 0, 0)[jg]
            for ig in range(I_GROUPS):
                term = s_scr[bi, jg * I_GROUPS + ig] * a0
                sa0[ig] = term if sa0[ig] is None else sa0[ig] + term
        return [_sublane_allsum(x) for x in sa0]

    def step(bi, t, parity, sa):
        for qi in range(1, 5):
            build(bi, qi, t + 1, 1 - parity)
        build(bi, 0, t + 2, parity)
        vb = [jnp.broadcast_to(v_ref[bi, t, pl.ds(ig, 1), :], (SUBLANES, LANES)) for ig in range(I_GROUPS)]
        out = [None] * I_GROUPS
        nxt = [None] * I_GROUPS
        for jg in range(J_GROUPS):
            w, b, k, r = (tile(bi, qi, parity)[jg] for qi in range(1, 5))
            a_next = tile(bi, 0, 1 - parity)[jg]
            for ig in range(I_GROUPS):
                idx = jg * I_GROUPS + ig
                sj = s_scr[bi, idx] * w + sa[ig] * b + vb[ig] * k
                s_scr[bi, idx] = sj
                to, tn = sj * r, sj * a_next
                out[ig] = to if out[ig] is None else out[ig] + to
                nxt[ig] = tn if nxt[ig] is None else nxt[ig] + tn
        o_ref[bi, t] = _sublane_sums(out)
        sa_scr[bi, parity] = _sublane_sums(nxt)
        return [jnp.broadcast_to(sa_scr[bi, parity, pl.ds(ig, 1), :], (SUBLANES, LANES)) for ig in range(I_GROUPS)]

    if tb <= SUBLANES:
        sas = [first_sa(bi) for bi in range(bb)]
        for t in range(tb):
            sas = [step(bi, t, t % 2, sas[bi]) for bi in range(bb)]
    else:
        for bi in range(bb):
            def two_steps(u, sa, bi=bi):
                return step(bi, 2 * u + 1, 1, step(bi, 2 * u, 0, sa))

            lax.fori_loop(0, tb // 2, two_steps, first_sa(bi))

    @pl.when(ti == pl.num_programs(1) - 1)
    def _():
        sT_ref[...] = s_scr[...]


def _rwkv_scan(a, w, b, k, r, v, s0, *, tb, bb):
    bsz, t_len = a.shape[0], a.shape[1]
    assert t_len % tb == 0 and tb % 2 == 0 and bsz % bb == 0
    tok_spec = pl.BlockSpec((bb, tb, SUBLANES, LANES), lambda bi, ti: (bi, ti, 0, 0))
    st_spec = pl.BlockSpec((bb, HEAD_DIM, SUBLANES, LANES), lambda bi, ti: (bi, 0, 0, 0))
    return pl.pallas_call(
        functools.partial(_scan_kernel, tb=tb, bb=bb),
        grid=(bsz // bb, t_len // tb),
        in_specs=[tok_spec] * 6 + [st_spec],
        out_specs=[tok_spec, st_spec],
        out_shape=[jax.ShapeDtypeStruct(a.shape, F32), jax.ShapeDtypeStruct(s0.shape, F32)],
        scratch_shapes=([pltpu.VMEM((bb, HEAD_DIM, SUBLANES, LANES), F32), pltpu.VMEM((bb, 2, SUBLANES, LANES), F32)]
                        + [pltpu.VMEM((bb, J_GROUPS, SUBLANES, LANES), F32)] * 10),
        compiler_params=pltpu.CompilerParams(dimension_semantics=("parallel", "arbitrary"),
                                             vmem_limit_bytes=VMEM_LIMIT),
        name="rwkv_scan",
    )(a, w, b, k, r, v, s0)


PROJ_COLS = 4992
LORA_PAD = 384
COL_Q, COL_R, COL_K, COL_V, COL_SK, COL_SV, COL_L = 0, 1024, 2048, 3072, 4096, 4352, 4608
VMEM_LIMIT = 56 * 1024 * 1024


def _rms(x, g):
    return x * lax.rsqrt(jnp.mean(x * x, axis=-1, keepdims=True) + RMS_EPS) * g


def _inproj_kernel(x_ref, g_ref, w_ref, o_ref, xn_ref):
    @pl.when(pl.program_id(1) == 0)
    def _():
        xn_ref[...] = _rms(x_ref[...], g_ref[...]).astype(BF16)

    o_ref[...] = jnp.dot(xn_ref[...], w_ref[...], preferred_element_type=F32)


def _inproj(x, g, w, *, tm, tn):
    m, d = x.shape
    n = w.shape[1]
    return pl.pallas_call(
        _inproj_kernel,
        grid=(m // tm, n // tn),
        in_specs=[pl.BlockSpec((tm, d), lambda i, j: (i, 0)),
                  pl.BlockSpec((1, d), lambda i, j: (0, 0)),
                  pl.BlockSpec((d, tn), lambda i, j: (0, j))],
        out_specs=pl.BlockSpec((tm, tn), lambda i, j: (i, j)),
        out_shape=jax.ShapeDtypeStruct((m, n), F32),
        scratch_shapes=[pltpu.VMEM((tm, d), BF16)],
        compiler_params=pltpu.CompilerParams(dimension_semantics=("parallel", "arbitrary"),
                                             vmem_limit_bytes=VMEM_LIMIT),
        name="inproj",
    )(x, g, w)


def _tile_sum(x):
    parts = [x[:, c * LANES:(c + 1) * LANES] for c in range(RWKV_WIDTH // LANES)]
    while len(parts) > 1:
        parts = [parts[i] + parts[i + 1] for i in range(0, len(parts), 2)]
    return parts[0]


def _tile8(x):
    return jnp.concatenate([x] * (RWKV_WIDTH // LANES), axis=1)


def _group_sum(x, m):
    hi = x.astype(BF16)
    rest = x - hi.astype(F32)
    mid = rest.astype(BF16)
    lo = (rest - mid.astype(F32)).astype(BF16)
    dot = lambda a: jnp.dot(a, m, preferred_element_type=F32)
    return dot(hi) + dot(mid) + dot(lo)


def _lora_dot(x, w):
    return jnp.dot(x.astype(BF16), w, preferred_element_type=F32)


KV_MU_R, KV_MU_K, KV_W0, KV_A0, KV_KK, KV_KA, KV_RK = range(7)
VV_MU_V, VV_LN_W, VV_LN_B = range(3)


def _prep_kernel(pr_ref, pk_ref, pv_ref, pl_ref, fr_ref, fk_ref, fv_ref, fl_ref,
                 kvec_ref, vvec_ref, mul_ref, wa_ref, g2_ref, sumh_ref,
                 a_out, w_out, b_out, k_out, r_out, v_out, g_out,
                 cr_ref, ck_ref, cv_ref, cl_ref, *, period):
    krow = lambda i: kvec_ref[i:i + 1, :]

    def shifted(p_ref, fix_ref, carry_ref, mu):
        p = p_ref[...]
        rolled = pltpu.roll(p, 1, 0)
        rows = lax.broadcasted_iota(jnp.int32, p.shape, 0)
        if period is None:
            @pl.when(pl.program_id(0) == 0)
            def _():
                carry_ref[...] = fix_ref[...]

            prev = jnp.where(rows == 0, carry_ref[...], rolled)
            carry_ref[...] = p[p.shape[0] - 1:, :]
        else:
            prev = jnp.where(rows % period == 0, fix_ref[...], rolled)
        return p + mu * (prev - p)

    r = shifted(pr_ref, fr_ref, cr_ref, krow(KV_MU_R))
    k = shifted(pk_ref, fk_ref, ck_ref, krow(KV_MU_K))
    v = shifted(pv_ref, fv_ref, cv_ref, vvec_ref[VV_MU_V:VV_MU_V + 1, :])
    xl = shifted(pl_ref, fl_ref, cl_ref, mul_ref[...])

    x_wa = xl[:, :LANES]
    is_w = lax.broadcasted_iota(jnp.int32, x_wa.shape, 1) < DECAY_LORA
    zw = krow(KV_W0) + _lora_dot(jnp.where(is_w, jnp.tanh(x_wa), 0.0), wa_ref[...])
    za = krow(KV_A0) + _lora_dot(jnp.where(is_w, 0.0, x_wa), wa_ref[...])
    g = _lora_dot(jax.nn.sigmoid(xl[:, LANES:]), g2_ref[...])
    y = -zw
    softplus = jnp.maximum(y, 0.0) + jnp.log(1.0 + jnp.exp(-jnp.abs(y)))
    decay = jnp.exp(-jnp.exp(-softplus - 0.5))
    a_sig = jax.nn.sigmoid(za)

    kk = k * krow(KV_KK)
    ssq = _tile8(_group_sum(_tile_sum(kk * kk), sumh_ref[...]))
    kk = kk / jnp.maximum(jnp.sqrt(ssq), 1e-12)

    a_out[...] = -kk
    w_out[...] = decay
    b_out[...] = kk * a_sig
    k_out[...] = k * (1.0 + (a_sig - 1.0) * krow(KV_KA))
    r_out[...] = r
    v_out[...] = v
    g_out[...] = g


def _prep(proj, fixes, params, *, tm, period):
    m = proj.shape[0]
    c = RWKV_WIDTH
    grid = (m // tm,)

    def col(width, idx):
        return pl.BlockSpec((tm, width), lambda i: (i, idx))

    def full(arr):
        return pl.BlockSpec(arr.shape, lambda i: (0,) * arr.ndim)

    if period is None:
        fix_specs = [full(f) for f in fixes]
    else:
        fix_specs = [pl.BlockSpec((tm, f.shape[1]), lambda i: (i, 0)) for f in fixes]
    out_spec = pl.BlockSpec((tm, c), lambda i: (i, 0))
    return pl.pallas_call(
        functools.partial(_prep_kernel, period=period),
        grid=grid,
        in_specs=[col(c, COL_R // c), col(c, COL_K // c), col(c, COL_V // c), col(LORA_PAD, COL_L // LORA_PAD)]
        + fix_specs + [full(p) for p in params],
        out_specs=[out_spec] * 7,
        out_shape=[jax.ShapeDtypeStruct((m, c), F32)] * 7,
        scratch_shapes=[pltpu.VMEM((1, c), F32)] * 3 + [pltpu.VMEM((1, LORA_PAD), F32)],
        compiler_params=pltpu.CompilerParams(dimension_semantics=("arbitrary",), vmem_limit_bytes=VMEM_LIMIT),
        name="rwkv_prep",
    )(proj, proj, proj, proj, *fixes, *params)


def _post_kernel(o_ref, r_ref, k_ref, v_ref, g_ref, kvec_ref, vvec_ref, gsum_ref, kv_ref, out_ref):
    o = o_ref[...]
    inv_n = 1.0 / HEAD_DIM
    mean = _tile8(_group_sum(_tile_sum(o), gsum_ref[...])) * inv_n
    d = o - mean
    var = _tile8(_group_sum(_tile_sum(d * d), gsum_ref[...])) * inv_n
    normed = (d * lax.rsqrt(var + GN_EPS) * vvec_ref[VV_LN_W:VV_LN_W + 1, :]
              + vvec_ref[VV_LN_B:VV_LN_B + 1, :])
    rk = _tile8(_group_sum(_tile_sum(r_ref[...] * k_ref[...] * kvec_ref[KV_RK:KV_RK + 1, :]), kv_ref[...]))
    out_ref[...] = ((normed + rk * v_ref[...]) * g_ref[...]).astype(BF16)


def _post(o, r, k, v, g, params, *, tm):
    m, c = o.shape
    tok = pl.BlockSpec((tm, c), lambda i: (i, 0))
    return pl.pallas_call(
        _post_kernel,
        grid=(m // tm,),
        in_specs=[tok] * 5 + [pl.BlockSpec(p.shape, lambda i: (0, 0)) for p in params],
        out_specs=tok,
        out_shape=jax.ShapeDtypeStruct((m, c), BF16),
        compiler_params=pltpu.CompilerParams(dimension_semantics=("parallel",), vmem_limit_bytes=VMEM_LIMIT),
        name="rwkv_post",
    )(o, r, k, v, g, *params)


ALIBI_SLOPES = [2.0 ** (-8.0 * (i + 1) / SWA_HEADS) for i in range(SWA_HEADS)]
SCORE_SCALE = HEAD_DIM ** -0.5


LOG2E = 1.4426950408889634


def _attn_block_kernel(sink_ref, q_ref, kc_ref, vc_ref, kp_ref, vp_ref, o_ref, ot_scr, bias_scr, *, sub_blocks):
    n = pl.program_id(0)
    span = WINDOW + BLOCK

    @pl.when(n == 0)
    def _():
        si = lax.broadcasted_iota(jnp.int32, (span, BLOCK), 0)
        qi = lax.broadcasted_iota(jnp.int32, (span, BLOCK), 1)
        dist = qi - si + WINDOW
        valid = (dist >= 0) & (dist < WINDOW)
        distf = dist.astype(F32)
        for h in range(SWA_HEADS):
            bias = jnp.where(valid, (-ALIBI_SLOPES[h] * LOG2E) * distf, -jnp.inf)
            bias_scr[1, h] = bias
            bias_scr[0, h] = jnp.where(si >= WINDOW, bias, -jnp.inf)

    for sb in range(sub_blocks):
        rows = slice(sb * BLOCK, (sb + 1) * BLOCK)
        if sb == 0:
            k_prev, v_prev = kp_ref[...], vp_ref[...]
            table = jnp.where(n == 0, 0, 1)
        else:
            prev_rows = slice((sb - 1) * BLOCK, sb * BLOCK)
            k_prev, v_prev = kc_ref[prev_rows, :], vc_ref[prev_rows, :]
            table = 1
        k_span = jnp.concatenate([k_prev, kc_ref[rows, :]], axis=0)
        v_t = jnp.concatenate([v_prev, vc_ref[rows, :]], axis=0).T.astype(BF16)
        for kh in range(SWA_KV_HEADS):
            ks = k_span[:, kh * HEAD_DIM:(kh + 1) * HEAD_DIM].astype(BF16)
            heads = range(kh * SWA_GROUP, (kh + 1) * SWA_GROUP)
            qg = jnp.concatenate([q_ref[rows, h * HEAD_DIM:(h + 1) * HEAD_DIM] for h in heads], axis=0).astype(BF16)
            s_all = lax.dot_general(ks, qg, (((1,), (1,)), ((), ())), preferred_element_type=F32)
            probs, dens = [], []
            for gi, h in enumerate(heads):
                s = s_all[:, gi * BLOCK:(gi + 1) * BLOCK] * (SCORE_SCALE * LOG2E) + bias_scr[table, h]
                sink = sink_ref[h] * LOG2E
                m = jnp.maximum(jnp.max(s, axis=0, keepdims=True), sink)
                e = jnp.exp2(s - m)
                dens.append(jnp.sum(e, axis=0, keepdims=True) + jnp.exp2(sink - m))
                probs.append(e.astype(BF16))
            o_t = jnp.dot(v_t[kh * HEAD_DIM:(kh + 1) * HEAD_DIM, :], jnp.concatenate(probs, axis=1),
                          preferred_element_type=F32) / jnp.concatenate(dens, axis=1)
            for gi, h in enumerate(heads):
                ot_scr[sb, h * HEAD_DIM:(h + 1) * HEAD_DIM, :] = o_t[:, gi * BLOCK:(gi + 1) * BLOCK]
        o_ref[rows, :] = ot_scr[sb].T.astype(BF16)


def _attn_prompt(proj, sinks, *, sub_blocks):
    t = proj.shape[0]
    tq = sub_blocks * BLOCK
    cur = lambda width, idx: pl.BlockSpec((tq, width), lambda n: (n, idx))
    prev = lambda width, idx: pl.BlockSpec((BLOCK, width), lambda n: (jnp.maximum(n * sub_blocks - 1, 0), idx))
    return pl.pallas_call(
        functools.partial(_attn_block_kernel, sub_blocks=sub_blocks),
        grid=(t // tq,),
        in_specs=[pl.BlockSpec(memory_space=pltpu.SMEM),
                  cur(SWA_WIDTH, COL_Q // SWA_WIDTH), cur(KV_WIDTH, COL_SK // KV_WIDTH), cur(KV_WIDTH, COL_SV // KV_WIDTH),
                  prev(KV_WIDTH, COL_SK // KV_WIDTH), prev(KV_WIDTH, COL_SV // KV_WIDTH)],
        out_specs=pl.BlockSpec((tq, SWA_WIDTH), lambda n: (n, 0)),
        out_shape=jax.ShapeDtypeStruct((t, SWA_WIDTH), BF16),
        scratch_shapes=[pltpu.VMEM((sub_blocks, SWA_WIDTH, BLOCK), F32),
                        pltpu.VMEM((2, SWA_HEADS, WINDOW + BLOCK, BLOCK), F32)],
        compiler_params=pltpu.CompilerParams(dimension_semantics=("arbitrary",), vmem_limit_bytes=VMEM_LIMIT),
        name="swa_prompt",
    )(sinks, proj, proj, proj, proj, proj)


def _attn_decode_kernel(sink_ref, q_ref, kn_ref, vn_ref, kp_ref, vp_ref, o_ref, ko_ref, vo_ref, *, t_new, bt):
    span = WINDOW + SUBLANES
    rows = SWA_GROUP * SUBLANES
    row = lax.broadcasted_iota(jnp.int32, (bt, rows, span), 1)
    si = lax.broadcasted_iota(jnp.int32, (bt, rows, span), 2)
    dist = row % SUBLANES - si + WINDOW
    valid = (dist >= 0) & (dist < WINDOW)
    distf = dist.astype(F32)
    grp = row // SUBLANES
    grp_col = grp[:, :, :1]

    def per_group(tile_grp, vals):
        out = vals[-1]
        for gi in range(SWA_GROUP - 2, -1, -1):
            out = jnp.where(tile_grp == gi, vals[gi], out)
        return out

    pad = jnp.zeros((bt, SUBLANES - t_new, HEAD_DIM), F32)
    for kh in range(SWA_KV_HEADS):
        lanes = slice(kh * HEAD_DIM, (kh + 1) * HEAD_DIM)
        heads = range(kh * SWA_GROUP, (kh + 1) * SWA_GROUP)
        ks = jnp.concatenate([kp_ref[:, :, lanes], kn_ref[:, :, lanes], pad], axis=1).astype(BF16)
        vs = jnp.concatenate([vp_ref[:, :, lanes], vn_ref[:, :, lanes], pad], axis=1).astype(BF16)
        qg = jnp.concatenate([piece for h in heads for piece in (q_ref[:, :, h * HEAD_DIM:(h + 1) * HEAD_DIM], pad)],
                             axis=1).astype(BF16)
        s = jnp.einsum("bqd,bsd->bqs", qg, ks, preferred_element_type=F32) * SCORE_SCALE
        slope = per_group(grp, [ALIBI_SLOPES[h] for h in heads])
        s = jnp.where(valid, s - slope * distf, -jnp.inf)
        sink = per_group(grp_col, [sink_ref[h] for h in heads])
        m = jnp.maximum(jnp.max(s, axis=-1, keepdims=True), sink)
        e = jnp.exp(s - m)
        den = jnp.sum(e, axis=-1, keepdims=True) + jnp.exp(sink - m)
        og = jnp.einsum("bqs,bsd->bqd", (e / den).astype(BF16), vs, preferred_element_type=F32)
        for gi, h in enumerate(heads):
            o_ref[:, :, h * HEAD_DIM:(h + 1) * HEAD_DIM] = og[:, gi * SUBLANES:gi * SUBLANES + t_new, :].astype(BF16)
    ko_ref[:, :WINDOW - t_new, :] = kp_ref[:, t_new:, :]
    ko_ref[:, WINDOW - t_new:, :] = kn_ref[...]
    vo_ref[:, :WINDOW - t_new, :] = vp_ref[:, t_new:, :]
    vo_ref[:, WINDOW - t_new:, :] = vn_ref[...]


def _attn_decode(proj3, k_past, v_past, sinks, *, bt):
    bsz, t_new = proj3.shape[0], proj3.shape[1]
    new = lambda width, idx: pl.BlockSpec((bt, t_new, width), lambda b: (b, 0, idx))
    past = pl.BlockSpec((bt, WINDOW, KV_WIDTH), lambda b: (b, 0, 0))
    return pl.pallas_call(
        functools.partial(_attn_decode_kernel, t_new=t_new, bt=bt),
        grid=(bsz // bt,),
        in_specs=[pl.BlockSpec(memory_space=pltpu.SMEM),
                  new(SWA_WIDTH, COL_Q // SWA_WIDTH), new(KV_WIDTH, COL_SK // KV_WIDTH), new(KV_WIDTH, COL_SV // KV_WIDTH),
                  past, past],
        out_specs=[pl.BlockSpec((bt, t_new, SWA_WIDTH), lambda b: (b, 0, 0)), past, past],
        out_shape=[jax.ShapeDtypeStruct((bsz, t_new, SWA_WIDTH), BF16),
                   jax.ShapeDtypeStruct(k_past.shape, F32), jax.ShapeDtypeStruct(v_past.shape, F32)],
        compiler_params=pltpu.CompilerParams(dimension_semantics=("parallel",), vmem_limit_bytes=VMEM_LIMIT),
        name="swa_decode",
    )(sinks, proj3, proj3, proj3, k_past, v_past)


def _outproj_kernel(x_ref, orw_ref, osw_ref, wr_ref, ws_ref, g_ref, x1_ref, h2_ref):
    x1 = (x_ref[...] + jnp.dot(orw_ref[...], wr_ref[...], preferred_element_type=F32)
          + jnp.dot(osw_ref[...], ws_ref[...], preferred_element_type=F32))
    x1_ref[...] = x1
    h2_ref[...] = _rms(x1, g_ref[...]).astype(BF16)


def _outproj(x, o_rwkv, o_swa, w_r, w_s, g, *, tm):
    m, d = x.shape
    row = lambda width: pl.BlockSpec((tm, width), lambda i: (i, 0))
    full = lambda arr: pl.BlockSpec(arr.shape, lambda i: (0, 0))
    return pl.pallas_call(
        _outproj_kernel,
        grid=(m // tm,),
        in_specs=[row(d), row(RWKV_WIDTH), row(SWA_WIDTH), full(w_r), full(w_s), full(g)],
        out_specs=[row(d), row(d)],
        out_shape=[jax.ShapeDtypeStruct((m, d), F32), jax.ShapeDtypeStruct((m, d), BF16)],
        compiler_params=pltpu.CompilerParams(dimension_semantics=("parallel",), vmem_limit_bytes=VMEM_LIMIT),
        name="outproj",
    )(x, o_rwkv, o_swa, w_r, w_s, g)


def _ffn_kernel(h_ref, x1_ref, wg_ref, wu_ref, wd_ref, gf_ref, y_ref, acc_ref):
    j = pl.program_id(1)

    @pl.when(j == 0)
    def _():
        acc_ref[...] = jnp.zeros_like(acc_ref)

    h = h_ref[...]
    gate = jnp.dot(h, wg_ref[...], preferred_element_type=F32)
    up = jnp.dot(h, wu_ref[...], preferred_element_type=F32)
    act = (gate * jax.nn.sigmoid(gate) * up).astype(BF16)
    acc_ref[...] += jnp.dot(act, wd_ref[...], preferred_element_type=F32)

    @pl.when(j == pl.num_programs(1) - 1)
    def _():
        y_ref[...] = _rms(x1_ref[...] + acc_ref[...], gf_ref[...])


def _ffn(h2, x1, w_gate, w_up, w_down, g_final, *, tm, tf):
    m, d = x1.shape
    f = w_gate.shape[1]
    return pl.pallas_call(
        _ffn_kernel,
        grid=(m // tm, f // tf),
        in_specs=[pl.BlockSpec((tm, d), lambda i, j: (i, 0)),
                  pl.BlockSpec((tm, d), lambda i, j: (i, 0)),
                  pl.BlockSpec((d, tf), lambda i, j: (0, j)),
                  pl.BlockSpec((d, tf), lambda i, j: (0, j)),
                  pl.BlockSpec((tf, d), lambda i, j: (j, 0)),
                  pl.BlockSpec((1, d), lambda i, j: (0, 0))],
        out_specs=pl.BlockSpec((tm, d), lambda i, j: (i, 0)),
        out_shape=jax.ShapeDtypeStruct((m, d), F32),
        scratch_shapes=[pltpu.VMEM((tm, d), F32)],
        compiler_params=pltpu.CompilerParams(dimension_semantics=("parallel", "arbitrary"),
                                             vmem_limit_bytes=VMEM_LIMIT),
        name="ffn",
    )(h2, x1, w_gate, w_up, w_down, g_final)


def _group_matrices():
    lane = np.arange(LANES)
    key_head = lane % RWKV_HEADS
    val_head = lane // I_LO
    sum_key = (key_head[:, None] == key_head[None, :]).astype(np.float32)
    sum_val = (val_head[:, None] == val_head[None, :]).astype(np.float32)
    key_to_val = (key_head[:, None] == val_head[None, :]).astype(np.float32)
    return sum_key, sum_val, key_to_val


def _state_to_kernel(s):
    b = s.shape[0]
    s = s.reshape(b, RWKV_HEADS, I_GROUPS, I_LO, J_GROUPS, SUBLANES)
    return s.transpose(0, 4, 2, 5, 1, 3).reshape(b, HEAD_DIM, SUBLANES, LANES)


def _state_from_kernel(s):
    b = s.shape[0]
    s = s.reshape(b, J_GROUPS, I_GROUPS, SUBLANES, RWKV_HEADS, I_LO)
    return s.transpose(0, 4, 2, 5, 1, 3).reshape(b, RWKV_HEADS, HEAD_DIM, HEAD_DIM)


def _shift_to_cols(shift):
    c = RWKV_WIDTH
    return (_key_order(shift[:, 0:c]), _key_order(shift[:, c:2 * c]), _val_order(shift[:, 2 * c:3 * c]),
            jnp.pad(shift[:, 3 * c:], ((0, 0), (0, LORA_PAD - LORA_COLS))))


def _shift_from_proj(rows):
    return jnp.concatenate([_key_natural(rows[:, COL_R:COL_K]), _key_natural(rows[:, COL_K:COL_V]),
                            _val_natural(rows[:, COL_V:COL_SK]), rows[:, COL_L:COL_L + LORA_COLS]], axis=1)


def _layer_weights(norm_attn, w_in, mu_shift, w0, w2, a0, a2, g2, k_k, k_a, r_k, ln_x_w, ln_x_b, sinks, w_out,
                   norm_ffn, w_gate, w_up, w_down, norm_final):
    c = RWKV_WIDTH
    swa0 = RWKV_COLS
    w_proj = jnp.concatenate([
        w_in[:, swa0:swa0 + SWA_WIDTH],
        _key_order(w_in[:, 0:c]), _key_order(w_in[:, c:2 * c]), _val_order(w_in[:, 2 * c:3 * c]),
        w_in[:, swa0 + SWA_WIDTH:swa0 + SWA_WIDTH + 2 * KV_WIDTH],
        jnp.pad(w_in[:, 3 * c:RWKV_COLS], ((0, 0), (0, LORA_PAD - LORA_COLS))),
    ], axis=1).astype(BF16)
    row = lambda v: v.reshape(1, -1)
    zero = jnp.zeros((c,), F32)
    kvec = _key_order(jnp.stack([mu_shift[0:c], mu_shift[c:2 * c], w0, a0, k_k, k_a, r_k.reshape(-1), zero]))
    vvec = _val_order(jnp.stack([mu_shift[2 * c:3 * c], ln_x_w, ln_x_b] + [zero] * 5))
    mu_l = jnp.pad(mu_shift[3 * c:], (0, LORA_PAD - LORA_COLS)).reshape(1, -1)
    lora_wa = _key_order(jnp.concatenate([w2, a2], axis=0)).astype(BF16)
    g2p = jnp.pad(_val_order(g2), ((0, LORA_PAD - LANES - GATE_LORA), (0, 0))).astype(BF16)
    sum_key, sum_val, key_to_val = _group_matrices()
    prep_params = (kvec, vvec, mu_l, lora_wa, g2p, jnp.asarray(sum_key, BF16))
    post_params = (kvec, vvec, jnp.asarray(sum_val, BF16), jnp.asarray(key_to_val, BF16))
    w_out_r = w_out[:c].reshape(RWKV_HEADS, I_GROUPS, I_LO, -1).swapaxes(0, 1).reshape(c, -1)
    return dict(
        norm_attn=row(norm_attn), w_proj=w_proj, prep=prep_params, post=post_params, sinks=sinks,
        w_out_r=w_out_r.astype(BF16), w_out_s=w_out[c:].astype(BF16), norm_ffn=row(norm_ffn),
        w_gate=w_gate.astype(BF16), w_up=w_up.astype(BF16), w_down=w_down.astype(BF16), norm_final=row(norm_final))


def _layer(x, shift_prev, state0, kv_past, lw, *, tiles):
    bsz, t_len, d = x.shape
    m = bsz * t_len
    x2 = x.reshape(m, d)
    proj = _inproj(x2, lw["norm_attn"], lw["w_proj"], tm=tiles["tm_in"], tn=tiles["tn_in"])

    fix_cols = _shift_to_cols(shift_prev)
    if bsz == 1:
        fixes, period = fix_cols, None
    else:
        fixes = tuple(jnp.pad(f[:, None, :], ((0, 0), (0, t_len - 1), (0, 0))).reshape(m, -1) for f in fix_cols)
        period = t_len
    a, w, b, k, r, v, g = _prep(proj, fixes, lw["prep"], tm=tiles["tm_prep"], period=period)

    tok = lambda z: z.reshape(bsz, t_len, SUBLANES, LANES)
    o, s_fin = _rwkv_scan(tok(a), tok(w), tok(b), tok(k), tok(r), tok(v), _state_to_kernel(state0), tb=tiles["tb"],
                          bb=min(tiles["bb_scan"], bsz))
    o_rwkv = _post(o.reshape(m, RWKV_WIDTH), r, k, v, g, lw["post"], tm=tiles["tm_post"])

    kv_shape = (bsz, WINDOW, SWA_KV_HEADS, HEAD_DIM)
    if kv_past is None:
        o_swa = _attn_prompt(proj, lw["sinks"], sub_blocks=tiles["attn_sub"])
        new_k = proj[m - WINDOW:, COL_SK:COL_SV].reshape(kv_shape)
        new_v = proj[m - WINDOW:, COL_SV:COL_L].reshape(kv_shape)
    else:
        k_past, v_past = (z.reshape(bsz, WINDOW, KV_WIDTH) for z in kv_past)
        o_swa, new_k, new_v = _attn_decode(proj.reshape(bsz, t_len, PROJ_COLS), k_past, v_past, lw["sinks"],
                                           bt=min(tiles["bt_attn"], bsz))
        o_swa, new_k, new_v = o_swa.reshape(m, SWA_WIDTH), new_k.reshape(kv_shape), new_v.reshape(kv_shape)

    x1, h2 = _outproj(x2, o_rwkv, o_swa, lw["w_out_r"], lw["w_out_s"], lw["norm_ffn"], tm=tiles["tm_out"])
    y = _ffn(h2, x1, lw["w_gate"], lw["w_up"], lw["w_down"], lw["norm_final"], tm=tiles["tm_ffn"], tf=tiles["tf"])

    new_shift = _shift_from_proj(proj.reshape(bsz, t_len, PROJ_COLS)[:, -1])
    return y.reshape(bsz, t_len, d), _state_from_kernel(s_fin), new_shift, new_k, new_v


def _tiles(m, t_len):
    return dict(tm_in=min(1024, m), tn_in=1664, tm_prep=min(256, m), tb=min(64, t_len),
                tm_post=min(512, m), tm_out=min(256, m), tm_ffn=min(512, m), tf=512, bt_attn=16, attn_sub=2, bb_scan=4)


def kernel(x_prompt, x_sample, state_rwkv, state_shift, cache_swa_k, cache_swa_v, norm_attn, w_in, mu_shift, w0, w2, a0, a2, g2, k_k, k_a, r_k, ln_x_w, ln_x_b, sinks, w_out, norm_ffn, w_gate, w_up, w_down, norm_final):
    assert norm_attn.shape[0] == 1, "single trunk layer"
    lw = _layer_weights(norm_attn[0], w_in[0], mu_shift[0], w0[0], w2[0], a0[0], a2[0], g2[0], k_k[0], k_a[0],
                        r_k[0], ln_x_w[0], ln_x_b[0], sinks[0], w_out[0], norm_ffn[0], w_gate[0], w_up[0],
                        w_down[0], norm_final)
    bp, tp, _ = x_prompt.shape
    bs, ts, _ = x_sample.shape
    assert bp == 1 and tp % BLOCK == 0 and ts < SUBLANES
    yp, p_state, p_shift, p_k, p_v = _layer(
        x_prompt, jnp.zeros((bp, RWKV_COLS), F32), jnp.zeros((bp, RWKV_HEADS, HEAD_DIM, HEAD_DIM), F32),
        None, lw, tiles=_tiles(bp * tp, tp))
    ys, s_state, s_shift, s_k, s_v = _layer(
        x_sample, state_shift[0], state_rwkv[0], (cache_swa_k[0], cache_swa_v[0]), lw, tiles=_tiles(bs * ts, ts))
    return (yp, ys, p_state[None], p_shift[None], p_k[None], p_v[None],
            s_state[None], s_shift[None], s_k[None], s_v[None])
```

```python
import functools

import numpy as np
import jax
import jax.numpy as jnp
from jax import lax
from jax.experimental import pallas as pl
from jax.experimental.pallas import tpu as pltpu

F32 = jnp.float32
BF16 = jnp.bfloat16

D_MODEL = 2048
HEAD_DIM = 64
RWKV_WIDTH = 1024
RWKV_HEADS = 16
SWA_WIDTH = 1024
SWA_HEADS = 16
SWA_KV_HEADS = 4
SWA_GROUP = 4
KV_WIDTH = 256
WINDOW = 128
BLOCK = 128
DECAY_LORA = 64
AAA_LORA = 64
GATE_LORA = 160
LORA_COLS = DECAY_LORA + AAA_LORA + GATE_LORA
RWKV_COLS = 3 * RWKV_WIDTH + LORA_COLS
D_FF = 5632
RMS_EPS = 1e-5
GN_EPS = 64e-5

SUBLANES = 8
LANES = 128
I_LO = LANES // RWKV_HEADS


def _reorder_last(x, split, a, b):
    return x.reshape(x.shape[:-1] + split).swapaxes(a, b).reshape(x.shape)


def _key_order(x):
    return _reorder_last(x, (RWKV_HEADS, HEAD_DIM), -1, -2)


def _key_natural(x):
    return _reorder_last(x, (HEAD_DIM, RWKV_HEADS), -1, -2)


def _val_order(x):
    return _reorder_last(x, (RWKV_HEADS, HEAD_DIM // I_LO, I_LO), -3, -2)


def _val_natural(x):
    return _reorder_last(x, (HEAD_DIM // I_LO, RWKV_HEADS, I_LO), -3, -2)


J_GROUPS = HEAD_DIM // SUBLANES
I_GROUPS = HEAD_DIM // I_LO


def _sublane_allsum(x):
    for shift in (4, 2, 1):
        x = x + pltpu.roll(x, shift, 0)
    return x


def _sublane_sums(xs):
    sub = lax.broadcasted_iota(jnp.int32, (SUBLANES, LANES), 0)

    def combine(x, y, d):
        clear = (sub & d) == 0
        return jnp.where(clear, x, pltpu.roll(y, d, 0)) + jnp.where(clear, pltpu.roll(x, SUBLANES - d, 0), y)

    z = [combine(xs[0], xs[4], 4), combine(xs[2], xs[6], 4), combine(xs[1], xs[5], 4), combine(xs[3], xs[7], 4)]
    return combine(combine(z[0], z[1], 2), combine(z[2], z[3], 2), 1)


def _cumprod_leading(x):
    n, shift = x.shape[0], 1
    while shift < n:
        x = jnp.concatenate([x[:shift], x[shift:] * x[:-shift]], axis=0)
        shift *= 2
    return x


def _scan_kernel(a_ref, w_ref, b_ref, k_ref, r_ref, v_ref, s0_ref, o_ref, sT_ref, s_scr, sa_scr, src_scr, *tiles,
                 tb, bb):
    ti = pl.program_id(1)

    @pl.when(ti == 0)
    def _():
        s_scr[...] = s0_ref[...]

    gather_idx = (lax.broadcasted_iota(jnp.int32, (SUBLANES, LANES), 0) * RWKV_HEADS
                  + lax.broadcasted_iota(jnp.int32, (SUBLANES, LANES), 1) // I_LO)

    def gather(src_row):
        return jnp.take_along_axis(jnp.broadcast_to(src_row, (SUBLANES, LANES)), gather_idx, axis=1)

    A, B, K, R = range(4)
    tile = lambda bi, qi, parity: tiles[2 * qi + parity].at[bi]

    def prepare_block(bi):
        p = _cumprod_leading(w_ref[bi])
        p_inv = 1.0 / p
        p_prev = jnp.concatenate([jnp.ones_like(p[:1]), p[:-1]], axis=0)
        src_scr[bi, A] = a_ref[bi] * p_prev
        src_scr[bi, B] = b_ref[bi] * p_inv
        src_scr[bi, K] = k_ref[bi] * p_inv
        src_scr[bi, R] = r_ref[bi] * p
        return p[tb - 1]

    def build(bi, qi, t, parity):
        src_t = jnp.minimum(t, tb - 1)
        for jg in range(J_GROUPS):
            tile(bi, qi, parity)[jg] = gather(src_scr[bi, qi, src_t, pl.ds(jg, 1), :])

    def first_sa(bi):
        for qi in range(4):
            build(bi, qi, 0, 0)
        build(bi, A, 1, 1)
        sa0 = [None] * I_GROUPS
        for jg in range(J_GROUPS):
            a0 = tile(bi, A, 0)[jg]
            for ig in range(I_GROUPS):
                term = s_scr[bi, jg * I_GROUPS + ig] * a0
                sa0[ig] = term if sa0[ig] is None else sa0[ig] + term
        return [_sublane_allsum(x) for x in sa0]

    def step(bi, t, parity, sa):
        for qi in (B, K, R):
            build(bi, qi, t + 1, 1 - parity)
        build(bi, A, t + 2, parity)
        vb = [jnp.broadcast_to(v_ref[bi, t, pl.ds(ig, 1), :], (SUBLANES, LANES)) for ig in range(I_GROUPS)]
        out = [None] * I_GROUPS
        nxt = [None] * I_GROUPS
        for jg in range(J_GROUPS):
            b, k, r = (tile(bi, qi, parity)[jg] for qi in (B, K, R))
            a_next = tile(bi, A, 1 - parity)[jg]
            for ig in range(I_GROUPS):
                idx = jg * I_GROUPS + ig
                sj = s_scr[bi, idx] + sa[ig] * b + vb[ig] * k
                s_scr[bi, idx] = sj
                to, tn = sj * r, sj * a_next
                out[ig] = to if out[ig] is None else out[ig] + to
                nxt[ig] = tn if nxt[ig] is None else nxt[ig] + tn
        o_ref[bi, t] = _sublane_sums(out)
        sa_scr[bi, parity] = _sublane_sums(nxt)
        return [jnp.broadcast_to(sa_scr[bi, parity, pl.ds(ig, 1), :], (SUBLANES, LANES)) for ig in range(I_GROUPS)]

    def finish_block(bi, p_end):
        for jg in range(J_GROUPS):
            scale = gather(p_end[jg:jg + 1, :])
            for ig in range(I_GROUPS):
                idx = jg * I_GROUPS + ig
                s_scr[bi, idx] = s_scr[bi, idx] * scale

    p_ends = [prepare_block(bi) for bi in range(bb)]
    if tb <= SUBLANES:
        sas = [first_sa(bi) for bi in range(bb)]
        for t in range(tb):
            sas = [step(bi, t, t % 2, sas[bi]) for bi in range(bb)]
    else:
        for bi in range(bb):
            def two_steps(u, sa, bi=bi):
                return step(bi, 2 * u + 1, 1, step(bi, 2 * u, 0, sa))

            lax.fori_loop(0, tb // 2, two_steps, first_sa(bi))
    for bi in range(bb):
        finish_block(bi, p_ends[bi])

    @pl.when(ti == pl.num_programs(1) - 1)
    def _():
        sT_ref[...] = s_scr[...]


def _rwkv_scan(a, w, b, k, r, v, s0, *, tb, bb):
    bsz, t_len = a.shape[0], a.shape[1]
    assert t_len % tb == 0 and tb % 2 == 0 and bsz % bb == 0
    tok_spec = pl.BlockSpec((bb, tb, SUBLANES, LANES), lambda bi, ti: (bi, ti, 0, 0))
    st_spec = pl.BlockSpec((bb, HEAD_DIM, SUBLANES, LANES), lambda bi, ti: (bi, 0, 0, 0))
    return pl.pallas_call(
        functools.partial(_scan_kernel, tb=tb, bb=bb),
        grid=(bsz // bb, t_len // tb),
        in_specs=[tok_spec] * 6 + [st_spec],
        out_specs=[tok_spec, st_spec],
        out_shape=[jax.ShapeDtypeStruct(a.shape, F32), jax.ShapeDtypeStruct(s0.shape, F32)],
        scratch_shapes=([pltpu.VMEM((bb, HEAD_DIM, SUBLANES, LANES), F32), pltpu.VMEM((bb, 2, SUBLANES, LANES), F32),
                         pltpu.VMEM((bb, 4, tb, SUBLANES, LANES), F32)]
                        + [pltpu.VMEM((bb, J_GROUPS, SUBLANES, LANES), F32)] * 8),
        compiler_params=pltpu.CompilerParams(dimension_semantics=("parallel", "arbitrary"),
                                             vmem_limit_bytes=VMEM_LIMIT),
        name="rwkv_scan",
    )(a, w, b, k, r, v, s0)


PROJ_COLS = 4992
LORA_PAD = 384
COL_Q, COL_R, COL_K, COL_V, COL_SK, COL_SV, COL_L = 0, 1024, 2048, 3072, 4096, 4352, 4608
VMEM_LIMIT = 56 * 1024 * 1024


def _rms(x, g):
    return x * lax.rsqrt(jnp.mean(x * x, axis=-1, keepdims=True) + RMS_EPS) * g


def _inproj_kernel(x_ref, g_ref, w_ref, o_ref, xn_ref):
    @pl.when(pl.program_id(1) == 0)
    def _():
        xn_ref[...] = _rms(x_ref[...], g_ref[...]).astype(BF16)

    o_ref[...] = jnp.dot(xn_ref[...], w_ref[...], preferred_element_type=F32)


def _inproj(x, g, w, *, tm, tn):
    m, d = x.shape
    n = w.shape[1]
    return pl.pallas_call(
        _inproj_kernel,
        grid=(m // tm, n // tn),
        in_specs=[pl.BlockSpec((tm, d), lambda i, j: (i, 0)),
                  pl.BlockSpec((1, d), lambda i, j: (0, 0)),
                  pl.BlockSpec((d, tn), lambda i, j: (0, j))],
        out_specs=pl.BlockSpec((tm, tn), lambda i, j: (i, j)),
        out_shape=jax.ShapeDtypeStruct((m, n), F32),
        scratch_shapes=[pltpu.VMEM((tm, d), BF16)],
        compiler_params=pltpu.CompilerParams(dimension_semantics=("parallel", "arbitrary"),
                                             vmem_limit_bytes=VMEM_LIMIT),
        name="inproj",
    )(x, g, w)


def _tile_sum(x):
    parts = [x[:, c * LANES:(c + 1) * LANES] for c in range(RWKV_WIDTH // LANES)]
    while len(parts) > 1:
        parts = [parts[i] + parts[i + 1] for i in range(0, len(parts), 2)]
    return parts[0]


def _tile8(x):
    return jnp.concatenate([x] * (RWKV_WIDTH // LANES), axis=1)


def _group_sum(x, m):
    hi = x.astype(BF16)
    rest = x - hi.astype(F32)
    mid = rest.astype(BF16)
    lo = (rest - mid.astype(F32)).astype(BF16)
    dot = lambda a: jnp.dot(a, m, preferred_element_type=F32)
    return dot(hi) + dot(mid) + dot(lo)


def _lora_dot(x, w):
    return jnp.dot(x.astype(BF16), w, preferred_element_type=F32)


KV_MU_R, KV_MU_K, KV_W0, KV_A0, KV_KK, KV_KA, KV_RK = range(7)
VV_MU_V, VV_LN_W, VV_LN_B = range(3)


def _prep_kernel(pr_ref, pk_ref, pv_ref, pl_ref, fr_ref, fk_ref, fv_ref, fl_ref,
                 kvec_ref, vvec_ref, mul_ref, wa_ref, g2_ref, sumh_ref,
                 a_out, w_out, b_out, k_out, r_out, v_out, g_out,
                 cr_ref, ck_ref, cv_ref, cl_ref, *, period):
    krow = lambda i: kvec_ref[i:i + 1, :]

    def shifted(p_ref, fix_ref, carry_ref, mu):
        p = p_ref[...]
        rolled = pltpu.roll(p, 1, 0)
        rows = lax.broadcasted_iota(jnp.int32, p.shape, 0)
        if period is None:
            @pl.when(pl.program_id(0) == 0)
            def _():
                carry_ref[...] = fix_ref[...]

            prev = jnp.where(rows == 0, carry_ref[...], rolled)
            carry_ref[...] = p[p.shape[0] - 1:, :]
        else:
            prev = jnp.where(rows % period == 0, fix_ref[...], rolled)
        return p + mu * (prev - p)

    r = shifted(pr_ref, fr_ref, cr_ref, krow(KV_MU_R))
    k = shifted(pk_ref, fk_ref, ck_ref, krow(KV_MU_K))
    v = shifted(pv_ref, fv_ref, cv_ref, vvec_ref[VV_MU_V:VV_MU_V + 1, :])
    xl = shifted(pl_ref, fl_ref, cl_ref, mul_ref[...])

    x_wa = xl[:, :LANES]
    is_w = lax.broadcasted_iota(jnp.int32, x_wa.shape, 1) < DECAY_LORA
    zw = krow(KV_W0) + _lora_dot(jnp.where(is_w, jnp.tanh(x_wa), 0.0), wa_ref[...])
    za = krow(KV_A0) + _lora_dot(jnp.where(is_w, 0.0, x_wa), wa_ref[...])
    g = _lora_dot(jax.nn.sigmoid(xl[:, LANES:]), g2_ref[...])
    y = -zw
    softplus = jnp.maximum(y, 0.0) + jnp.log(1.0 + jnp.exp(-jnp.abs(y)))
    decay = jnp.exp(-jnp.exp(-softplus - 0.5))
    a_sig = jax.nn.sigmoid(za)

    kk = k * krow(KV_KK)
    ssq = _tile8(_group_sum(_tile_sum(kk * kk), sumh_ref[...]))
    kk = kk / jnp.maximum(jnp.sqrt(ssq), 1e-12)

    a_out[...] = -kk
    w_out[...] = decay
    b_out[...] = kk * a_sig
    k_out[...] = k * (1.0 + (a_sig - 1.0) * krow(KV_KA))
    r_out[...] = r
    v_out[...] = v
    g_out[...] = g


def _prep(proj, fixes, params, *, tm, period):
    m = proj.shape[0]
    c = RWKV_WIDTH
    grid = (m // tm,)

    def col(width, idx):
        return pl.BlockSpec((tm, width), lambda i: (i, idx))

    def full(arr):
        return pl.BlockSpec(arr.shape, lambda i: (0,) * arr.ndim)

    if period is None:
        fix_specs = [full(f) for f in fixes]
    else:
        fix_specs = [pl.BlockSpec((tm, f.shape[1]), lambda i: (i, 0)) for f in fixes]
    out_spec = pl.BlockSpec((tm, c), lambda i: (i, 0))
    return pl.pallas_call(
        functools.partial(_prep_kernel, period=period),
        grid=grid,
        in_specs=[col(c, COL_R // c), col(c, COL_K // c), col(c, COL_V // c), col(LORA_PAD, COL_L // LORA_PAD)]
        + fix_specs + [full(p) for p in params],
        out_specs=[out_spec] * 7,
        out_shape=[jax.ShapeDtypeStruct((m, c), F32)] * 7,
        scratch_shapes=[pltpu.VMEM((1, c), F32)] * 3 + [pltpu.VMEM((1, LORA_PAD), F32)],
        compiler_params=pltpu.CompilerParams(dimension_semantics=("arbitrary",), vmem_limit_bytes=VMEM_LIMIT),
        name="rwkv_prep",
    )(proj, proj, proj, proj, *fixes, *params)


def _rwkv_out(o, r, k, v, g, kvec_ref, vvec_ref, gsum_ref, kv_ref):
    inv_n = 1.0 / HEAD_DIM
    mean = _tile8(_group_sum(_tile_sum(o), gsum_ref[...])) * inv_n
    d = o - mean
    var = _tile8(_group_sum(_tile_sum(d * d), gsum_ref[...])) * inv_n
    normed = (d * lax.rsqrt(var + GN_EPS) * vvec_ref[VV_LN_W:VV_LN_W + 1, :]
              + vvec_ref[VV_LN_B:VV_LN_B + 1, :])
    rk = _tile8(_group_sum(_tile_sum(r * k * kvec_ref[KV_RK:KV_RK + 1, :]), kv_ref[...]))
    return ((normed + rk * v) * g).astype(BF16)


ALIBI_SLOPES = [2.0 ** (-8.0 * (i + 1) / SWA_HEADS) for i in range(SWA_HEADS)]
SCORE_SCALE = HEAD_DIM ** -0.5


LOG2E = 1.4426950408889634


def _attn_block_kernel(sink_ref, q_ref, kc_ref, vc_ref, kp_ref, vp_ref, o_ref, ot_scr, bias_scr, *, sub_blocks):
    n = pl.program_id(0)
    span = WINDOW + BLOCK

    @pl.when(n == 0)
    def _():
        si = lax.broadcasted_iota(jnp.int32, (span, BLOCK), 0)
        qi = lax.broadcasted_iota(jnp.int32, (span, BLOCK), 1)
        dist = qi - si + WINDOW
        valid = (dist >= 0) & (dist < WINDOW)
        distf = dist.astype(F32)
        for h in range(SWA_HEADS):
            bias = jnp.where(valid, (-ALIBI_SLOPES[h] * LOG2E) * distf, -jnp.inf)
            bias_scr[1, h] = bias
            bias_scr[0, h] = jnp.where(si >= WINDOW, bias, -jnp.inf)

    for sb in range(sub_blocks):
        rows = slice(sb * BLOCK, (sb + 1) * BLOCK)
        if sb == 0:
            k_prev, v_prev = kp_ref[...], vp_ref[...]
            table = jnp.where(n == 0, 0, 1)
        else:
            prev_rows = slice((sb - 1) * BLOCK, sb * BLOCK)
            k_prev, v_prev = kc_ref[prev_rows, :], vc_ref[prev_rows, :]
            table = 1
        k_span = jnp.concatenate([k_prev, kc_ref[rows, :]], axis=0)
        v_t = jnp.concatenate([v_prev, vc_ref[rows, :]], axis=0).T.astype(BF16)
        for kh in range(SWA_KV_HEADS):
            ks = k_span[:, kh * HEAD_DIM:(kh + 1) * HEAD_DIM].astype(BF16)
            heads = range(kh * SWA_GROUP, (kh + 1) * SWA_GROUP)
            qg = jnp.concatenate([q_ref[rows, h * HEAD_DIM:(h + 1) * HEAD_DIM] for h in heads], axis=0).astype(BF16)
            s_all = lax.dot_general(ks, qg, (((1,), (1,)), ((), ())), preferred_element_type=F32)
            probs, dens = [], []
            for gi, h in enumerate(heads):
                s = s_all[:, gi * BLOCK:(gi + 1) * BLOCK] * (SCORE_SCALE * LOG2E) + bias_scr[table, h]
                sink = sink_ref[h] * LOG2E
                m = jnp.maximum(jnp.max(s, axis=0, keepdims=True), sink)
                e = jnp.exp2(s - m)
                dens.append(jnp.sum(e, axis=0, keepdims=True) + jnp.exp2(sink - m))
                probs.append(e.astype(BF16))
            o_t = jnp.dot(v_t[kh * HEAD_DIM:(kh + 1) * HEAD_DIM, :], jnp.concatenate(probs, axis=1),
                          preferred_element_type=F32) / jnp.concatenate(dens, axis=1)
            for gi, h in enumerate(heads):
                ot_scr[sb, h * HEAD_DIM:(h + 1) * HEAD_DIM, :] = o_t[:, gi * BLOCK:(gi + 1) * BLOCK]
        o_ref[rows, :] = ot_scr[sb].T.astype(BF16)


def _attn_prompt(proj, sinks, *, sub_blocks):
    t = proj.shape[0]
    tq = sub_blocks * BLOCK
    cur = lambda width, idx: pl.BlockSpec((tq, width), lambda n: (n, idx))
    prev = lambda width, idx: pl.BlockSpec((BLOCK, width), lambda n: (jnp.maximum(n * sub_blocks - 1, 0), idx))
    return pl.pallas_call(
        functools.partial(_attn_block_kernel, sub_blocks=sub_blocks),
        grid=(t // tq,),
        in_specs=[pl.BlockSpec(memory_space=pltpu.SMEM),
                  cur(SWA_WIDTH, COL_Q // SWA_WIDTH), cur(KV_WIDTH, COL_SK // KV_WIDTH), cur(KV_WIDTH, COL_SV // KV_WIDTH),
                  prev(KV_WIDTH, COL_SK // KV_WIDTH), prev(KV_WIDTH, COL_SV // KV_WIDTH)],
        out_specs=pl.BlockSpec((tq, SWA_WIDTH), lambda n: (n, 0)),
        out_shape=jax.ShapeDtypeStruct((t, SWA_WIDTH), BF16),
        scratch_shapes=[pltpu.VMEM((sub_blocks, SWA_WIDTH, BLOCK), F32),
                        pltpu.VMEM((2, SWA_HEADS, WINDOW + BLOCK, BLOCK), F32)],
        compiler_params=pltpu.CompilerParams(dimension_semantics=("arbitrary",), vmem_limit_bytes=VMEM_LIMIT),
        name="swa_prompt",
    )(sinks, proj, proj, proj, proj, proj)


def _attn_decode_kernel(sink_ref, q_ref, kn_ref, vn_ref, kp_ref, vp_ref, o_ref, ko_ref, vo_ref, *, t_new, bt):
    span = WINDOW + SUBLANES
    rows = SWA_GROUP * SUBLANES
    row = lax.broadcasted_iota(jnp.int32, (bt, rows, span), 1)
    si = lax.broadcasted_iota(jnp.int32, (bt, rows, span), 2)
    dist = row % SUBLANES - si + WINDOW
    valid = (dist >= 0) & (dist < WINDOW)
    distf = dist.astype(F32)
    grp = row // SUBLANES
    grp_col = grp[:, :, :1]

    def per_group(tile_grp, vals):
        out = vals[-1]
        for gi in range(SWA_GROUP - 2, -1, -1):
            out = jnp.where(tile_grp == gi, vals[gi], out)
        return out

    pad = jnp.zeros((bt, SUBLANES - t_new, HEAD_DIM), F32)
    for kh in range(SWA_KV_HEADS):
        lanes = slice(kh * HEAD_DIM, (kh + 1) * HEAD_DIM)
        heads = range(kh * SWA_GROUP, (kh + 1) * SWA_GROUP)
        ks = jnp.concatenate([kp_ref[:, :, lanes], kn_ref[:, :, lanes], pad], axis=1).astype(BF16)
        vs = jnp.concatenate([vp_ref[:, :, lanes], vn_ref[:, :, lanes], pad], axis=1).astype(BF16)
        qg = jnp.concatenate([piece for h in heads for piece in (q_ref[:, :, h * HEAD_DIM:(h + 1) * HEAD_DIM], pad)],
                             axis=1).astype(BF16)
        s = jnp.einsum("bqd,bsd->bqs", qg, ks, preferred_element_type=F32) * SCORE_SCALE
        slope = per_group(grp, [ALIBI_SLOPES[h] for h in heads])
        s = jnp.where(valid, s - slope * distf, -jnp.inf)
        sink = per_group(grp_col, [sink_ref[h] for h in heads])
        m = jnp.maximum(jnp.max(s, axis=-1, keepdims=True), sink)
        e = jnp.exp(s - m)
        den = jnp.sum(e, axis=-1, keepdims=True) + jnp.exp(sink - m)
        og = jnp.einsum("bqs,bsd->bqd", (e / den).astype(BF16), vs, preferred_element_type=F32)
        for gi, h in enumerate(heads):
            o_ref[:, :, h * HEAD_DIM:(h + 1) * HEAD_DIM] = og[:, gi * SUBLANES:gi * SUBLANES + t_new, :].astype(BF16)
    ko_ref[:, :WINDOW - t_new, :] = kp_ref[:, t_new:, :]
    ko_ref[:, WINDOW - t_new:, :] = kn_ref[...]
    vo_ref[:, :WINDOW - t_new, :] = vp_ref[:, t_new:, :]
    vo_ref[:, WINDOW - t_new:, :] = vn_ref[...]


def _attn_decode(proj3, k_past, v_past, sinks, *, bt):
    bsz, t_new = proj3.shape[0], proj3.shape[1]
    new = lambda width, idx: pl.BlockSpec((bt, t_new, width), lambda b: (b, 0, idx))
    past = pl.BlockSpec((bt, WINDOW, KV_WIDTH), lambda b: (b, 0, 0))
    return pl.pallas_call(
        functools.partial(_attn_decode_kernel, t_new=t_new, bt=bt),
        grid=(bsz // bt,),
        in_specs=[pl.BlockSpec(memory_space=pltpu.SMEM),
                  new(SWA_WIDTH, COL_Q // SWA_WIDTH), new(KV_WIDTH, COL_SK // KV_WIDTH), new(KV_WIDTH, COL_SV // KV_WIDTH),
                  past, past],
        out_specs=[pl.BlockSpec((bt, t_new, SWA_WIDTH), lambda b: (b, 0, 0)), past, past],
        out_shape=[jax.ShapeDtypeStruct((bsz, t_new, SWA_WIDTH), BF16),
                   jax.ShapeDtypeStruct(k_past.shape, F32), jax.ShapeDtypeStruct(v_past.shape, F32)],
        compiler_params=pltpu.CompilerParams(dimension_semantics=("parallel",), vmem_limit_bytes=VMEM_LIMIT),
        name="swa_decode",
    )(sinks, proj3, proj3, proj3, k_past, v_past)


def _outproj_kernel(x_ref, o_ref, r_ref, k_ref, v_ref, g_ref, osw_ref, kvec_ref, vvec_ref, gsum_ref, kv_ref,
                    wr_ref, ws_ref, gn_ref, x1_ref, h2_ref):
    o_rwkv = _rwkv_out(o_ref[...], r_ref[...], k_ref[...], v_ref[...], g_ref[...], kvec_ref, vvec_ref, gsum_ref, kv_ref)
    x1 = (x_ref[...] + jnp.dot(o_rwkv, wr_ref[...], preferred_element_type=F32)
          + jnp.dot(osw_ref[...], ws_ref[...], preferred_element_type=F32))
    x1_ref[...] = x1
    h2_ref[...] = _rms(x1, gn_ref[...]).astype(BF16)


def _outproj(x, o, r, k, v, g, o_swa, post_params, w_r, w_s, g_norm, *, tm):
    m, d = x.shape
    row = lambda width: pl.BlockSpec((tm, width), lambda i: (i, 0))
    full = lambda arr: pl.BlockSpec(arr.shape, lambda i: (0, 0))
    consts = (*post_params, w_r, w_s, g_norm)
    return pl.pallas_call(
        _outproj_kernel,
        grid=(m // tm,),
        in_specs=[row(d)] + [row(RWKV_WIDTH)] * 5 + [row(SWA_WIDTH)] + [full(c) for c in consts],
        out_specs=[row(d), row(d)],
        out_shape=[jax.ShapeDtypeStruct((m, d), F32), jax.ShapeDtypeStruct((m, d), BF16)],
        compiler_params=pltpu.CompilerParams(dimension_semantics=("parallel",), vmem_limit_bytes=VMEM_LIMIT),
        name="outproj",
    )(x, o, r, k, v, g, o_swa, *consts)


def _ffn_kernel(h_ref, x1_ref, wg_ref, wu_ref, wd_ref, gf_ref, y_ref, acc_ref):
    j = pl.program_id(1)

    @pl.when(j == 0)
    def _():
        acc_ref[...] = jnp.zeros_like(acc_ref)

    h = h_ref[...]
    gate = jnp.dot(h, wg_ref[...], preferred_element_type=F32)
    up = jnp.dot(h, wu_ref[...], preferred_element_type=F32)
    act = (gate * jax.nn.sigmoid(gate) * up).astype(BF16)
    acc_ref[...] += jnp.dot(act, wd_ref[...], preferred_element_type=F32)

    @pl.when(j == pl.num_programs(1) - 1)
    def _():
        y_ref[...] = _rms(x1_ref[...] + acc_ref[...], gf_ref[...])


def _ffn(h2, x1, w_gate, w_up, w_down, g_final, *, tm, tf):
    m, d = x1.shape
    f = w_gate.shape[1]
    return pl.pallas_call(
        _ffn_kernel,
        grid=(m // tm, f // tf),
        in_specs=[pl.BlockSpec((tm, d), lambda i, j: (i, 0)),
                  pl.BlockSpec((tm, d), lambda i, j: (i, 0)),
                  pl.BlockSpec((d, tf), lambda i, j: (0, j)),
                  pl.BlockSpec((d, tf), lambda i, j: (0, j)),
                  pl.BlockSpec((tf, d), lambda i, j: (j, 0)),
                  pl.BlockSpec((1, d), lambda i, j: (0, 0))],
        out_specs=pl.BlockSpec((tm, d), lambda i, j: (i, 0)),
        out_shape=jax.ShapeDtypeStruct((m, d), F32),
        scratch_shapes=[pltpu.VMEM((tm, d), F32)],
        compiler_params=pltpu.CompilerParams(dimension_semantics=("parallel", "arbitrary"),
                                             vmem_limit_bytes=VMEM_LIMIT),
        name="ffn",
    )(h2, x1, w_gate, w_up, w_down, g_final)


def _group_matrices():
    lane = np.arange(LANES)
    key_head = lane % RWKV_HEADS
    val_head = lane // I_LO
    sum_key = (key_head[:, None] == key_head[None, :]).astype(np.float32)
    sum_val = (val_head[:, None] == val_head[None, :]).astype(np.float32)
    key_to_val = (key_head[:, None] == val_head[None, :]).astype(np.float32)
    return sum_key, sum_val, key_to_val


def _state_to_kernel(s):
    b = s.shape[0]
    s = s.reshape(b, RWKV_HEADS, I_GROUPS, I_LO, J_GROUPS, SUBLANES)
    return s.transpose(0, 4, 2, 5, 1, 3).reshape(b, HEAD_DIM, SUBLANES, LANES)


def _state_from_kernel(s):
    b = s.shape[0]
    s = s.reshape(b, J_GROUPS, I_GROUPS, SUBLANES, RWKV_HEADS, I_LO)
    return s.transpose(0, 4, 2, 5, 1, 3).reshape(b, RWKV_HEADS, HEAD_DIM, HEAD_DIM)


def _shift_to_cols(shift):
    c = RWKV_WIDTH
    return (_key_order(shift[:, 0:c]), _key_order(shift[:, c:2 * c]), _val_order(shift[:, 2 * c:3 * c]),
            jnp.pad(shift[:, 3 * c:], ((0, 0), (0, LORA_PAD - LORA_COLS))))


def _shift_from_proj(rows):
    return jnp.concatenate([_key_natural(rows[:, COL_R:COL_K]), _key_natural(rows[:, COL_K:COL_V]),
                            _val_natural(rows[:, COL_V:COL_SK]), rows[:, COL_L:COL_L + LORA_COLS]], axis=1)


def _layer_weights(norm_attn, w_in, mu_shift, w0, w2, a0, a2, g2, k_k, k_a, r_k, ln_x_w, ln_x_b, sinks, w_out,
                   norm_ffn, w_gate, w_up, w_down, norm_final):
    c = RWKV_WIDTH
    swa0 = RWKV_COLS
    w_proj = jnp.concatenate([
        w_in[:, swa0:swa0 + SWA_WIDTH],
        _key_order(w_in[:, 0:c]), _key_order(w_in[:, c:2 * c]), _val_order(w_in[:, 2 * c:3 * c]),
        w_in[:, swa0 + SWA_WIDTH:swa0 + SWA_WIDTH + 2 * KV_WIDTH],
        jnp.pad(w_in[:, 3 * c:RWKV_COLS], ((0, 0), (0, LORA_PAD - LORA_COLS))),
    ], axis=1).astype(BF16)
    row = lambda v: v.reshape(1, -1)
    zero = jnp.zeros((c,), F32)
    kvec = _key_order(jnp.stack([mu_shift[0:c], mu_shift[c:2 * c], w0, a0, k_k, k_a, r_k.reshape(-1), zero]))
    vvec = _val_order(jnp.stack([mu_shift[2 * c:3 * c], ln_x_w, ln_x_b] + [zero] * 5))
    mu_l = jnp.pad(mu_shift[3 * c:], (0, LORA_PAD - LORA_COLS)).reshape(1, -1)
    lora_wa = _key_order(jnp.concatenate([w2, a2], axis=0)).astype(BF16)
    g2p = jnp.pad(_val_order(g2), ((0, LORA_PAD - LANES - GATE_LORA), (0, 0))).astype(BF16)
    sum_key, sum_val, key_to_val = _group_matrices()
    prep_params = (kvec, vvec, mu_l, lora_wa, g2p, jnp.asarray(sum_key, BF16))
    post_params = (kvec, vvec, jnp.asarray(sum_val, BF16), jnp.asarray(key_to_val, BF16))
    w_out_r = w_out[:c].reshape(RWKV_HEADS, I_GROUPS, I_LO, -1).swapaxes(0, 1).reshape(c, -1)
    return dict(
        norm_attn=row(norm_attn), w_proj=w_proj, prep=prep_params, post=post_params, sinks=sinks,
        w_out_r=w_out_r.astype(BF16), w_out_s=w_out[c:].astype(BF16), norm_ffn=row(norm_ffn),
        w_gate=w_gate.astype(BF16), w_up=w_up.astype(BF16), w_down=w_down.astype(BF16), norm_final=row(norm_final))


def _layer(x, shift_prev, state0, kv_past, lw, *, tiles):
    bsz, t_len, d = x.shape
    m = bsz * t_len
    x2 = x.reshape(m, d)
    proj = _inproj(x2, lw["norm_attn"], lw["w_proj"], tm=tiles["tm_in"], tn=tiles["tn_in"])

    fix_cols = _shift_to_cols(shift_prev)
    if bsz == 1:
        fixes, period = fix_cols, None
    else:
        fixes = tuple(jnp.pad(f[:, None, :], ((0, 0), (0, t_len - 1), (0, 0))).reshape(m, -1) for f in fix_cols)
        period = t_len
    a, w, b, k, r, v, g = _prep(proj, fixes, lw["prep"], tm=tiles["tm_prep"], period=period)

    tok = lambda z: z.reshape(bsz, t_len, SUBLANES, LANES)
    o, s_fin = _rwkv_scan(tok(a), tok(w), tok(b), tok(k), tok(r), tok(v), _state_to_kernel(state0), tb=tiles["tb"],
                          bb=min(tiles["bb_scan"], bsz))

    kv_shape = (bsz, WINDOW, SWA_KV_HEADS, HEAD_DIM)
    if kv_past is None:
        o_swa = _attn_prompt(proj, lw["sinks"], sub_blocks=tiles["attn_sub"])
        new_k = proj[m - WINDOW:, COL_SK:COL_SV].reshape(kv_shape)
        new_v = proj[m - WINDOW:, COL_SV:COL_L].reshape(kv_shape)
    else:
        k_past, v_past = (z.reshape(bsz, WINDOW, KV_WIDTH) for z in kv_past)
        o_swa, new_k, new_v = _attn_decode(proj.reshape(bsz, t_len, PROJ_COLS), k_past, v_past, lw["sinks"],
                                           bt=min(tiles["bt_attn"], bsz))
        o_swa, new_k, new_v = o_swa.reshape(m, SWA_WIDTH), new_k.reshape(kv_shape), new_v.reshape(kv_shape)

    x1, h2 = _outproj(x2, o.reshape(m, RWKV_WIDTH), r, k, v, g, o_swa, lw["post"], lw["w_out_r"], lw["w_out_s"],
                      lw["norm_ffn"], tm=tiles["tm_out"])
    y = _ffn(h2, x1, lw["w_gate"], lw["w_up"], lw["w_down"], lw["norm_final"], tm=tiles["tm_ffn"], tf=tiles["tf"])

    new_shift = _shift_from_proj(proj.reshape(bsz, t_len, PROJ_COLS)[:, -1])
    return y.reshape(bsz, t_len, d), _state_from_kernel(s_fin), new_shift, new_k, new_v


def _tiles(m, t_len):
    return dict(tm_in=min(1024, m), tn_in=1664, tm_prep=min(256, m), tb=min(64, t_len),
                tm_out=min(256, m), tm_ffn=min(512, m), tf=512, bt_attn=16, attn_sub=2, bb_scan=4)


def kernel(x_prompt, x_sample, state_rwkv, state_shift, cache_swa_k, cache_swa_v, norm_attn, w_in, mu_shift, w0, w2, a0, a2, g2, k_k, k_a, r_k, ln_x_w, ln_x_b, sinks, w_out, norm_ffn, w_gate, w_up, w_down, norm_final):
    assert norm_attn.shape[0] == 1, "single trunk layer"
    lw = _layer_weights(norm_attn[0], w_in[0], mu_shift[0], w0[0], w2[0], a0[0], a2[0], g2[0], k_k[0], k_a[0],
                        r_k[0], ln_x_w[0], ln_x_b[0], sinks[0], w_out[0], norm_ffn[0], w_gate[0], w_up[0],
                        w_down[0], norm_final)
    bp, tp, _ = x_prompt.shape
    bs, ts, _ = x_sample.shape
    assert bp == 1 and tp % BLOCK == 0 and ts < SUBLANES
    yp, p_state, p_shift, p_k, p_v = _layer(
        x_prompt, jnp.zeros((bp, RWKV_COLS), F32), jnp.zeros((bp, RWKV_HEADS, HEAD_DIM, HEAD_DIM), F32),
        None, lw, tiles=_tiles(bp * tp, tp))
    ys, s_state, s_shift, s_k, s_v = _layer(
        x_sample, state_shift[0], state_rwkv[0], (cache_swa_k[0], cache_swa_v[0]), lw, tiles=_tiles(bs * ts, ts))
    return (yp, ys, p_state[None], p_shift[None], p_k[None], p_v[None],
            s_state[None], s_shift[None], s_k[None], s_v[None])
```

```python
import functools

import numpy as np
import jax
import jax.numpy as jnp
from jax import lax
from jax.experimental import pallas as pl
from jax.experimental.pallas import tpu as pltpu

F32 = jnp.float32
BF16 = jnp.bfloat16

D_MODEL = 2048
HEAD_DIM = 64
RWKV_WIDTH = 1024
RWKV_HEADS = 16
SWA_WIDTH = 1024
SWA_HEADS = 16
SWA_KV_HEADS = 4
SWA_GROUP = 4
KV_WIDTH = 256
WINDOW = 128
BLOCK = 128
DECAY_LORA = 64
AAA_LORA = 64
GATE_LORA = 160
LORA_COLS = DECAY_LORA + AAA_LORA + GATE_LORA
RWKV_COLS = 3 * RWKV_WIDTH + LORA_COLS
D_FF = 5632
RMS_EPS = 1e-5
GN_EPS = 64e-5

SUBLANES = 8
LANES = 128
I_LO = LANES // RWKV_HEADS


def _reorder_last(x, split, a, b):
    return x.reshape(x.shape[:-1] + split).swapaxes(a, b).reshape(x.shape)


def _key_order(x):
    return _reorder_last(x, (RWKV_HEADS, HEAD_DIM), -1, -2)


def _key_natural(x):
    return _reorder_last(x, (HEAD_DIM, RWKV_HEADS), -1, -2)


def _val_order(x):
    return _reorder_last(x, (RWKV_HEADS, HEAD_DIM // I_LO, I_LO), -3, -2)


def _val_natural(x):
    return _reorder_last(x, (HEAD_DIM // I_LO, RWKV_HEADS, I_LO), -3, -2)


J_GROUPS = HEAD_DIM // SUBLANES
I_GROUPS = HEAD_DIM // I_LO
SCAN_SUB = 64


def _sublane_allsum(x):
    for shift in (4, 2, 1):
        x = x + pltpu.roll(x, shift, 0)
    return x


def _sublane_sums(xs):
    sub = lax.broadcasted_iota(jnp.int32, (SUBLANES, LANES), 0)

    def combine(x, y, d):
        clear = (sub & d) == 0
        return jnp.where(clear, x, pltpu.roll(y, d, 0)) + jnp.where(clear, pltpu.roll(x, SUBLANES - d, 0), y)

    z = [combine(xs[0], xs[4], 4), combine(xs[2], xs[6], 4), combine(xs[1], xs[5], 4), combine(xs[3], xs[7], 4)]
    return combine(combine(z[0], z[1], 2), combine(z[2], z[3], 2), 1)


def _cumprod_leading(x):
    n, shift = x.shape[0], 1
    while shift < n:
        x = jnp.concatenate([x[:shift], x[shift:] * x[:-shift]], axis=0)
        shift *= 2
    return x


def _scan_kernel(a_ref, w_ref, b_ref, k_ref, r_ref, v_ref, s0_ref, o_ref, sT_ref, s_scr, sa_scr, src_scr, pend_scr,
                 *tiles, tb, sub, bb):
    ti = pl.program_id(1)
    n_sub = tb // sub

    @pl.when(ti == 0)
    def _():
        s_scr[...] = s0_ref[...]

    gather_idx = (lax.broadcasted_iota(jnp.int32, (SUBLANES, LANES), 0) * RWKV_HEADS
                  + lax.broadcasted_iota(jnp.int32, (SUBLANES, LANES), 1) // I_LO)

    def gather(src_row):
        return jnp.take_along_axis(jnp.broadcast_to(src_row, (SUBLANES, LANES)), gather_idx, axis=1)

    A, B, K, R = range(4)
    tile = lambda bi, qi, parity: tiles[2 * qi + parity].at[bi]

    def prepare(bi):
        p_end = None
        for n in range(n_sub):
            rows = slice(n * sub, (n + 1) * sub)
            p = _cumprod_leading(w_ref[bi, rows])
            p_inv = 1.0 / p
            first = jnp.ones_like(p[:1]) if p_end is None else p_end[None]
            p_prev = jnp.concatenate([first, p[:-1]], axis=0)
            src_scr[bi, A, rows] = a_ref[bi, rows] * p_prev
            src_scr[bi, B, rows] = b_ref[bi, rows] * p_inv
            src_scr[bi, K, rows] = k_ref[bi, rows] * p_inv
            src_scr[bi, R, rows] = r_ref[bi, rows] * p
            p_end = p[sub - 1]
            for jg in range(J_GROUPS):
                pend_scr[bi, n, jg] = gather(p_end[jg:jg + 1, :])

    def build(bi, qi, t, parity):
        src_t = jnp.minimum(t, tb - 1)
        for jg in range(J_GROUPS):
            tile(bi, qi, parity)[jg] = gather(src_scr[bi, qi, src_t, pl.ds(jg, 1), :])

    def first_sa(bi):
        for qi in range(4):
            build(bi, qi, 0, 0)
        build(bi, A, 1, 1)
        sa0 = [None] * I_GROUPS
        for jg in range(J_GROUPS):
            a0 = tile(bi, A, 0)[jg]
            for ig in range(I_GROUPS):
                term = s_scr[bi, jg * I_GROUPS + ig] * a0
                sa0[ig] = term if sa0[ig] is None else sa0[ig] + term
        return [_sublane_allsum(x) for x in sa0]

    def step(bi, t, parity, sa, rescale=None):
        for qi in (B, K, R):
            build(bi, qi, t + 1, 1 - parity)
        build(bi, A, t + 2, parity)
        vb = [jnp.broadcast_to(v_ref[bi, t, pl.ds(ig, 1), :], (SUBLANES, LANES)) for ig in range(I_GROUPS)]
        out = [None] * I_GROUPS
        nxt = [None] * I_GROUPS
        for jg in range(J_GROUPS):
            b, k, r = (tile(bi, qi, parity)[jg] for qi in (B, K, R))
            a_next = tile(bi, A, 1 - parity)[jg]
            for ig in range(I_GROUPS):
                idx = jg * I_GROUPS + ig
                s_old = s_scr[bi, idx]
                if rescale is not None:
                    s_old = s_old * pend_scr[bi, rescale, jg]
                sj = s_old + sa[ig] * b + vb[ig] * k
                s_scr[bi, idx] = sj
                to, tn = sj * r, sj * a_next
                out[ig] = to if out[ig] is None else out[ig] + to
                nxt[ig] = tn if nxt[ig] is None else nxt[ig] + tn
        o_ref[bi, t] = _sublane_sums(out)
        sa_scr[bi, parity] = _sublane_sums(nxt)
        return [jnp.broadcast_to(sa_scr[bi, parity, pl.ds(ig, 1), :], (SUBLANES, LANES)) for ig in range(I_GROUPS)]

    def finish(bi):
        for jg in range(J_GROUPS):
            for ig in range(I_GROUPS):
                idx = jg * I_GROUPS + ig
                s_scr[bi, idx] = s_scr[bi, idx] * pend_scr[bi, n_sub - 1, jg]

    for bi in range(bb):
        prepare(bi)
    if tb <= SUBLANES:
        sas = [first_sa(bi) for bi in range(bb)]
        for t in range(tb):
            sas = [step(bi, t, t % 2, sas[bi]) for bi in range(bb)]
    else:
        for bi in range(bb):
            sa = first_sa(bi)
            for n in range(n_sub):
                t0 = n * sub
                sa = step(bi, t0, 0, sa, rescale=n - 1 if n else None)
                sa = step(bi, t0 + 1, 1, sa)

                def two_steps(u, sa, bi=bi):
                    return step(bi, 2 * u + 1, 1, step(bi, 2 * u, 0, sa))

                sa = lax.fori_loop(t0 // 2 + 1, (t0 + sub) // 2, two_steps, sa)
    for bi in range(bb):
        finish(bi)

    @pl.when(ti == pl.num_programs(1) - 1)
    def _():
        sT_ref[...] = s_scr[...]


def _rwkv_scan(a, w, b, k, r, v, s0, *, tb, sub, bb):
    bsz, t_len = a.shape[0], a.shape[1]
    assert t_len % tb == 0 and tb % sub == 0 and sub % 2 == 0 and bsz % bb == 0
    tok_spec = pl.BlockSpec((bb, tb, SUBLANES, LANES), lambda bi, ti: (bi, ti, 0, 0))
    st_spec = pl.BlockSpec((bb, HEAD_DIM, SUBLANES, LANES), lambda bi, ti: (bi, 0, 0, 0))
    return pl.pallas_call(
        functools.partial(_scan_kernel, tb=tb, sub=sub, bb=bb),
        grid=(bsz // bb, t_len // tb),
        in_specs=[tok_spec] * 6 + [st_spec],
        out_specs=[tok_spec, st_spec],
        out_shape=[jax.ShapeDtypeStruct(a.shape, F32), jax.ShapeDtypeStruct(s0.shape, F32)],
        scratch_shapes=([pltpu.VMEM((bb, HEAD_DIM, SUBLANES, LANES), F32), pltpu.VMEM((bb, 2, SUBLANES, LANES), F32),
                         pltpu.VMEM((bb, 4, tb, SUBLANES, LANES), F32),
                         pltpu.VMEM((bb, tb // sub, J_GROUPS, SUBLANES, LANES), F32)]
                        + [pltpu.VMEM((bb, J_GROUPS, SUBLANES, LANES), F32)] * 8),
        compiler_params=pltpu.CompilerParams(dimension_semantics=("parallel", "arbitrary"),
                                             vmem_limit_bytes=VMEM_LIMIT),
        name="rwkv_scan",
    )(a, w, b, k, r, v, s0)


PROJ_COLS = 4992
LORA_PAD = 384
COL_Q, COL_R, COL_K, COL_V, COL_SK, COL_SV, COL_L = 0, 1024, 2048, 3072, 4096, 4352, 4608
VMEM_LIMIT = 56 * 1024 * 1024


def _rms(x, g):
    return x * lax.rsqrt(jnp.mean(x * x, axis=-1, keepdims=True) + RMS_EPS) * g


def _inproj_kernel(x_ref, g_ref, w_ref, o_ref, xn_ref):
    @pl.when(pl.program_id(1) == 0)
    def _():
        xn_ref[...] = _rms(x_ref[...], g_ref[...]).astype(BF16)

    o_ref[...] = jnp.dot(xn_ref[...], w_ref[...], preferred_element_type=F32)


def _inproj(x, g, w, *, tm, tn):
    m, d = x.shape
    n = w.shape[1]
    return pl.pallas_call(
        _inproj_kernel,
        grid=(m // tm, n // tn),
        in_specs=[pl.BlockSpec((tm, d), lambda i, j: (i, 0)),
                  pl.BlockSpec((1, d), lambda i, j: (0, 0)),
                  pl.BlockSpec((d, tn), lambda i, j: (0, j))],
        out_specs=pl.BlockSpec((tm, tn), lambda i, j: (i, j)),
        out_shape=jax.ShapeDtypeStruct((m, n), F32),
        scratch_shapes=[pltpu.VMEM((tm, d), BF16)],
        compiler_params=pltpu.CompilerParams(dimension_semantics=("parallel", "arbitrary"),
                                             vmem_limit_bytes=VMEM_LIMIT),
        name="inproj",
    )(x, g, w)


def _tile_sum(x):
    parts = [x[:, c * LANES:(c + 1) * LANES] for c in range(RWKV_WIDTH // LANES)]
    while len(parts) > 1:
        parts = [parts[i] + parts[i + 1] for i in range(0, len(parts), 2)]
    return parts[0]


def _tile8(x):
    return jnp.concatenate([x] * (RWKV_WIDTH // LANES), axis=1)


def _group_sum(x, m):
    hi = x.astype(BF16)
    rest = x - hi.astype(F32)
    mid = rest.astype(BF16)
    lo = (rest - mid.astype(F32)).astype(BF16)
    dot = lambda a: jnp.dot(a, m, preferred_element_type=F32)
    return dot(hi) + dot(mid) + dot(lo)


def _lora_dot(x, w):
    return jnp.dot(x.astype(BF16), w, preferred_element_type=F32)


KV_MU_R, KV_MU_K, KV_W0, KV_A0, KV_KK, KV_KA, KV_RK = range(7)
VV_MU_V, VV_LN_W, VV_LN_B = range(3)


def _prep_kernel(pr_ref, pk_ref, pv_ref, pl_ref, fr_ref, fk_ref, fv_ref, fl_ref,
                 kvec_ref, vvec_ref, mul_ref, wa_ref, g2_ref, sumh_ref,
                 a_out, w_out, b_out, k_out, r_out, v_out, g_out,
                 cr_ref, ck_ref, cv_ref, cl_ref, *, period):
    krow = lambda i: kvec_ref[i:i + 1, :]

    def shifted(p_ref, fix_ref, carry_ref, mu):
        p = p_ref[...]
        rolled = pltpu.roll(p, 1, 0)
        rows = lax.broadcasted_iota(jnp.int32, p.shape, 0)
        if period is None:
            @pl.when(pl.program_id(0) == 0)
            def _():
                carry_ref[...] = fix_ref[...]

            prev = jnp.where(rows == 0, carry_ref[...], rolled)
            carry_ref[...] = p[p.shape[0] - 1:, :]
        else:
            prev = jnp.where(rows % period == 0, fix_ref[...], rolled)
        return p + mu * (prev - p)

    r = shifted(pr_ref, fr_ref, cr_ref, krow(KV_MU_R))
    k = shifted(pk_ref, fk_ref, ck_ref, krow(KV_MU_K))
    v = shifted(pv_ref, fv_ref, cv_ref, vvec_ref[VV_MU_V:VV_MU_V + 1, :])
    xl = shifted(pl_ref, fl_ref, cl_ref, mul_ref[...])

    x_wa = xl[:, :LANES]
    is_w = lax.broadcasted_iota(jnp.int32, x_wa.shape, 1) < DECAY_LORA
    zw = krow(KV_W0) + _lora_dot(jnp.where(is_w, jnp.tanh(x_wa), 0.0), wa_ref[...])
    za = krow(KV_A0) + _lora_dot(jnp.where(is_w, 0.0, x_wa), wa_ref[...])
    g = _lora_dot(jax.nn.sigmoid(xl[:, LANES:]), g2_ref[...])
    y = -zw
    softplus = jnp.maximum(y, 0.0) + jnp.log(1.0 + jnp.exp(-jnp.abs(y)))
    decay = jnp.exp(-jnp.exp(-softplus - 0.5))
    a_sig = jax.nn.sigmoid(za)

    kk = k * krow(KV_KK)
    ssq = _tile8(_group_sum(_tile_sum(kk * kk), sumh_ref[...]))
    kk = kk / jnp.maximum(jnp.sqrt(ssq), 1e-12)

    a_out[...] = -kk
    w_out[...] = decay
    b_out[...] = kk * a_sig
    k_out[...] = k * (1.0 + (a_sig - 1.0) * krow(KV_KA))
    r_out[...] = r
    v_out[...] = v
    g_out[...] = g


def _prep(proj, fixes, params, *, tm, period):
    m = proj.shape[0]
    c = RWKV_WIDTH
    grid = (m // tm,)

    def col(width, idx):
        return pl.BlockSpec((tm, width), lambda i: (i, idx))

    def full(arr):
        return pl.BlockSpec(arr.shape, lambda i: (0,) * arr.ndim)

    if period is None:
        fix_specs = [full(f) for f in fixes]
    else:
        fix_specs = [pl.BlockSpec((tm, f.shape[1]), lambda i: (i, 0)) for f in fixes]
    out_spec = pl.BlockSpec((tm, c), lambda i: (i, 0))
    return pl.pallas_call(
        functools.partial(_prep_kernel, period=period),
        grid=grid,
        in_specs=[col(c, COL_R // c), col(c, COL_K // c), col(c, COL_V // c), col(LORA_PAD, COL_L // LORA_PAD)]
        + fix_specs + [full(p) for p in params],
        out_specs=[out_spec] * 7,
        out_shape=[jax.ShapeDtypeStruct((m, c), F32)] * 7,
        scratch_shapes=[pltpu.VMEM((1, c), F32)] * 3 + [pltpu.VMEM((1, LORA_PAD), F32)],
        compiler_params=pltpu.CompilerParams(dimension_semantics=("arbitrary",), vmem_limit_bytes=VMEM_LIMIT),
        name="rwkv_prep",
    )(proj, proj, proj, proj, *fixes, *params)


def _rwkv_out(o, r, k, v, g, kvec_ref, vvec_ref, gsum_ref, kv_ref):
    inv_n = 1.0 / HEAD_DIM
    mean = _tile8(_group_sum(_tile_sum(o), gsum_ref[...])) * inv_n
    d = o - mean
    var = _tile8(_group_sum(_tile_sum(d * d), gsum_ref[...])) * inv_n
    normed = (d * lax.rsqrt(var + GN_EPS) * vvec_ref[VV_LN_W:VV_LN_W + 1, :]
              + vvec_ref[VV_LN_B:VV_LN_B + 1, :])
    rk = _tile8(_group_sum(_tile_sum(r * k * kvec_ref[KV_RK:KV_RK + 1, :]), kv_ref[...]))
    return ((normed + rk * v) * g).astype(BF16)


ALIBI_SLOPES = [2.0 ** (-8.0 * (i + 1) / SWA_HEADS) for i in range(SWA_HEADS)]
SCORE_SCALE = HEAD_DIM ** -0.5


LOG2E = 1.4426950408889634


def _attn_block_kernel(sink_ref, q_ref, kc_ref, vc_ref, kp_ref, vp_ref, o_ref, ot_scr, bias_scr, *, sub_blocks):
    n = pl.program_id(0)
    span = WINDOW + BLOCK

    @pl.when(n == 0)
    def _():
        si = lax.broadcasted_iota(jnp.int32, (span, BLOCK), 0)
        qi = lax.broadcasted_iota(jnp.int32, (span, BLOCK), 1)
        dist = qi - si + WINDOW
        valid = (dist >= 0) & (dist < WINDOW)
        distf = dist.astype(F32)
        for h in range(SWA_HEADS):
            bias = jnp.where(valid, (-ALIBI_SLOPES[h] * LOG2E) * distf, -jnp.inf)
            bias_scr[1, h] = bias
            bias_scr[0, h] = jnp.where(si >= WINDOW, bias, -jnp.inf)

    for sb in range(sub_blocks):
        rows = slice(sb * BLOCK, (sb + 1) * BLOCK)
        if sb == 0:
            k_prev, v_prev = kp_ref[...], vp_ref[...]
            table = jnp.where(n == 0, 0, 1)
        else:
            prev_rows = slice((sb - 1) * BLOCK, sb * BLOCK)
            k_prev, v_prev = kc_ref[prev_rows, :], vc_ref[prev_rows, :]
            table = 1
        k_span = jnp.concatenate([k_prev, kc_ref[rows, :]], axis=0)
        v_t = jnp.concatenate([v_prev, vc_ref[rows, :]], axis=0).T.astype(BF16)
        for kh in range(SWA_KV_HEADS):
            ks = k_span[:, kh * HEAD_DIM:(kh + 1) * HEAD_DIM].astype(BF16)
            heads = range(kh * SWA_GROUP, (kh + 1) * SWA_GROUP)
            qg = jnp.concatenate([q_ref[rows, h * HEAD_DIM:(h + 1) * HEAD_DIM] for h in heads], axis=0).astype(BF16)
            s_all = lax.dot_general(ks, qg, (((1,), (1,)), ((), ())), preferred_element_type=F32)
            probs, dens = [], []
            for gi, h in enumerate(heads):
                s = s_all[:, gi * BLOCK:(gi + 1) * BLOCK] * (SCORE_SCALE * LOG2E) + bias_scr[table, h]
                sink = sink_ref[h] * LOG2E
                m = jnp.maximum(jnp.max(s, axis=0, keepdims=True), sink)
                e = jnp.exp2(s - m)
                dens.append(jnp.sum(e, axis=0, keepdims=True) + jnp.exp2(sink - m))
                probs.append(e.astype(BF16))
            o_t = jnp.dot(v_t[kh * HEAD_DIM:(kh + 1) * HEAD_DIM, :], jnp.concatenate(probs, axis=1),
                          preferred_element_type=F32) / jnp.concatenate(dens, axis=1)
            for gi, h in enumerate(heads):
                ot_scr[sb, h * HEAD_DIM:(h + 1) * HEAD_DIM, :] = o_t[:, gi * BLOCK:(gi + 1) * BLOCK]
        o_ref[rows, :] = ot_scr[sb].T.astype(BF16)


def _attn_prompt(proj, sinks, *, sub_blocks):
    t = proj.shape[0]
    tq = sub_blocks * BLOCK
    cur = lambda width, idx: pl.BlockSpec((tq, width), lambda n: (n, idx))
    prev = lambda width, idx: pl.BlockSpec((BLOCK, width), lambda n: (jnp.maximum(n * sub_blocks - 1, 0), idx))
    return pl.pallas_call(
        functools.partial(_attn_block_kernel, sub_blocks=sub_blocks),
        grid=(t // tq,),
        in_specs=[pl.BlockSpec(memory_space=pltpu.SMEM),
                  cur(SWA_WIDTH, COL_Q // SWA_WIDTH), cur(KV_WIDTH, COL_SK // KV_WIDTH), cur(KV_WIDTH, COL_SV // KV_WIDTH),
                  prev(KV_WIDTH, COL_SK // KV_WIDTH), prev(KV_WIDTH, COL_SV // KV_WIDTH)],
        out_specs=pl.BlockSpec((tq, SWA_WIDTH), lambda n: (n, 0)),
        out_shape=jax.ShapeDtypeStruct((t, SWA_WIDTH), BF16),
        scratch_shapes=[pltpu.VMEM((sub_blocks, SWA_WIDTH, BLOCK), F32),
                        pltpu.VMEM((2, SWA_HEADS, WINDOW + BLOCK, BLOCK), F32)],
        compiler_params=pltpu.CompilerParams(dimension_semantics=("arbitrary",), vmem_limit_bytes=VMEM_LIMIT),
        name="swa_prompt",
    )(sinks, proj, proj, proj, proj, proj)


def _attn_decode_kernel(sink_ref, q_ref, kn_ref, vn_ref, kp_ref, vp_ref, o_ref, ko_ref, vo_ref, *, t_new, bt):
    span = WINDOW + SUBLANES
    rows = SWA_GROUP * SUBLANES
    row = lax.broadcasted_iota(jnp.int32, (bt, rows, span), 1)
    si = lax.broadcasted_iota(jnp.int32, (bt, rows, span), 2)
    dist = row % SUBLANES - si + WINDOW
    valid = (dist >= 0) & (dist < WINDOW)
    distf = dist.astype(F32)
    grp = row // SUBLANES
    grp_col = grp[:, :, :1]

    def per_group(tile_grp, vals):
        out = vals[-1]
        for gi in range(SWA_GROUP - 2, -1, -1):
            out = jnp.where(tile_grp == gi, vals[gi], out)
        return out

    pad = jnp.zeros((bt, SUBLANES - t_new, HEAD_DIM), F32)
    for kh in range(SWA_KV_HEADS):
        lanes = slice(kh * HEAD_DIM, (kh + 1) * HEAD_DIM)
        heads = range(kh * SWA_GROUP, (kh + 1) * SWA_GROUP)
        ks = jnp.concatenate([kp_ref[:, :, lanes], kn_ref[:, :, lanes], pad], axis=1).astype(BF16)
        vs = jnp.concatenate([vp_ref[:, :, lanes], vn_ref[:, :, lanes], pad], axis=1).astype(BF16)
        qg = jnp.concatenate([piece for h in heads for piece in (q_ref[:, :, h * HEAD_DIM:(h + 1) * HEAD_DIM], pad)],
                             axis=1).astype(BF16)
        s = jnp.einsum("bqd,bsd->bqs", qg, ks, preferred_element_type=F32) * SCORE_SCALE
        slope = per_group(grp, [ALIBI_SLOPES[h] for h in heads])
        s = jnp.where(valid, s - slope * distf, -jnp.inf)
        sink = per_group(grp_col, [sink_ref[h] for h in heads])
        m = jnp.maximum(jnp.max(s, axis=-1, keepdims=True), sink)
        e = jnp.exp(s - m)
        den = jnp.sum(e, axis=-1, keepdims=True) + jnp.exp(sink - m)
        og = jnp.einsum("bqs,bsd->bqd", (e / den).astype(BF16), vs, preferred_element_type=F32)
        for gi, h in enumerate(heads):
            o_ref[:, :, h * HEAD_DIM:(h + 1) * HEAD_DIM] = og[:, gi * SUBLANES:gi * SUBLANES + t_new, :].astype(BF16)
    ko_ref[:, :WINDOW - t_new, :] = kp_ref[:, t_new:, :]
    ko_ref[:, WINDOW - t_new:, :] = kn_ref[...]
    vo_ref[:, :WINDOW - t_new, :] = vp_ref[:, t_new:, :]
    vo_ref[:, WINDOW - t_new:, :] = vn_ref[...]


def _attn_decode(proj3, k_past, v_past, sinks, *, bt):
    bsz, t_new = proj3.shape[0], proj3.shape[1]
    new = lambda width, idx: pl.BlockSpec((bt, t_new, width), lambda b: (b, 0, idx))
    past = pl.BlockSpec((bt, WINDOW, KV_WIDTH), lambda b: (b, 0, 0))
    return pl.pallas_call(
        functools.partial(_attn_decode_kernel, t_new=t_new, bt=bt),
        grid=(bsz // bt,),
        in_specs=[pl.BlockSpec(memory_space=pltpu.SMEM),
                  new(SWA_WIDTH, COL_Q // SWA_WIDTH), new(KV_WIDTH, COL_SK // KV_WIDTH), new(KV_WIDTH, COL_SV // KV_WIDTH),
                  past, past],
        out_specs=[pl.BlockSpec((bt, t_new, SWA_WIDTH), lambda b: (b, 0, 0)), past, past],
        out_shape=[jax.ShapeDtypeStruct((bsz, t_new, SWA_WIDTH), BF16),
                   jax.ShapeDtypeStruct(k_past.shape, F32), jax.ShapeDtypeStruct(v_past.shape, F32)],
        compiler_params=pltpu.CompilerParams(dimension_semantics=("parallel",), vmem_limit_bytes=VMEM_LIMIT),
        name="swa_decode",
    )(sinks, proj3, proj3, proj3, k_past, v_past)


def _outproj_kernel(x_ref, o_ref, r_ref, k_ref, v_ref, g_ref, osw_ref, kvec_ref, vvec_ref, gsum_ref, kv_ref,
                    wr_ref, ws_ref, gn_ref, x1_ref, h2_ref):
    o_rwkv = _rwkv_out(o_ref[...], r_ref[...], k_ref[...], v_ref[...], g_ref[...], kvec_ref, vvec_ref, gsum_ref, kv_ref)
    x1 = (x_ref[...] + jnp.dot(o_rwkv, wr_ref[...], preferred_element_type=F32)
          + jnp.dot(osw_ref[...], ws_ref[...], preferred_element_type=F32))
    x1_ref[...] = x1
    h2_ref[...] = _rms(x1, gn_ref[...]).astype(BF16)


def _outproj(x, o, r, k, v, g, o_swa, post_params, w_r, w_s, g_norm, *, tm):
    m, d = x.shape
    row = lambda width: pl.BlockSpec((tm, width), lambda i: (i, 0))
    full = lambda arr: pl.BlockSpec(arr.shape, lambda i: (0, 0))
    consts = (*post_params, w_r, w_s, g_norm)
    return pl.pallas_call(
        _outproj_kernel,
        grid=(m // tm,),
        in_specs=[row(d)] + [row(RWKV_WIDTH)] * 5 + [row(SWA_WIDTH)] + [full(c) for c in consts],
        out_specs=[row(d), row(d)],
        out_shape=[jax.ShapeDtypeStruct((m, d), F32), jax.ShapeDtypeStruct((m, d), BF16)],
        compiler_params=pltpu.CompilerParams(dimension_semantics=("parallel",), vmem_limit_bytes=VMEM_LIMIT),
        name="outproj",
    )(x, o, r, k, v, g, o_swa, *consts)


def _ffn_kernel(h_ref, x1_ref, wg_ref, wu_ref, wd_ref, gf_ref, y_ref, acc_ref):
    j = pl.program_id(1)

    @pl.when(j == 0)
    def _():
        acc_ref[...] = jnp.zeros_like(acc_ref)

    h = h_ref[...]
    gate = jnp.dot(h, wg_ref[...], preferred_element_type=F32)
    up = jnp.dot(h, wu_ref[...], preferred_element_type=F32)
    act = (gate * jax.nn.sigmoid(gate) * up).astype(BF16)
    acc_ref[...] += jnp.dot(act, wd_ref[...], preferred_element_type=F32)

    @pl.when(j == pl.num_programs(1) - 1)
    def _():
        y_ref[...] = _rms(x1_ref[...] + acc_ref[...], gf_ref[...])


def _ffn(h2, x1, w_gate, w_up, w_down, g_final, *, tm, tf):
    m, d = x1.shape
    f = w_gate.shape[1]
    return pl.pallas_call(
        _ffn_kernel,
        grid=(m // tm, f // tf),
        in_specs=[pl.BlockSpec((tm, d), lambda i, j: (i, 0)),
                  pl.BlockSpec((tm, d), lambda i, j: (i, 0)),
                  pl.BlockSpec((d, tf), lambda i, j: (0, j)),
                  pl.BlockSpec((d, tf), lambda i, j: (0, j)),
                  pl.BlockSpec((tf, d), lambda i, j: (j, 0)),
                  pl.BlockSpec((1, d), lambda i, j: (0, 0))],
        out_specs=pl.BlockSpec((tm, d), lambda i, j: (i, 0)),
        out_shape=jax.ShapeDtypeStruct((m, d), F32),
        scratch_shapes=[pltpu.VMEM((tm, d), F32)],
        compiler_params=pltpu.CompilerParams(dimension_semantics=("parallel", "arbitrary"),
                                             vmem_limit_bytes=VMEM_LIMIT),
        name="ffn",
    )(h2, x1, w_gate, w_up, w_down, g_final)


def _group_matrices():
    lane = np.arange(LANES)
    key_head = lane % RWKV_HEADS
    val_head = lane // I_LO
    sum_key = (key_head[:, None] == key_head[None, :]).astype(np.float32)
    sum_val = (val_head[:, None] == val_head[None, :]).astype(np.float32)
    key_to_val = (key_head[:, None] == val_head[None, :]).astype(np.float32)
    return sum_key, sum_val, key_to_val


def _state_to_kernel(s):
    b = s.shape[0]
    s = s.reshape(b, RWKV_HEADS, I_GROUPS, I_LO, J_GROUPS, SUBLANES)
    return s.transpose(0, 4, 2, 5, 1, 3).reshape(b, HEAD_DIM, SUBLANES, LANES)


def _state_from_kernel(s):
    b = s.shape[0]
    s = s.reshape(b, J_GROUPS, I_GROUPS, SUBLANES, RWKV_HEADS, I_LO)
    return s.transpose(0, 4, 2, 5, 1, 3).reshape(b, RWKV_HEADS, HEAD_DIM, HEAD_DIM)


def _shift_to_cols(shift):
    c = RWKV_WIDTH
    return (_key_order(shift[:, 0:c]), _key_order(shift[:, c:2 * c]), _val_order(shift[:, 2 * c:3 * c]),
            jnp.pad(shift[:, 3 * c:], ((0, 0), (0, LORA_PAD - LORA_COLS))))


def _shift_from_proj(rows):
    return jnp.concatenate([_key_natural(rows[:, COL_R:COL_K]), _key_natural(rows[:, COL_K:COL_V]),
                            _val_natural(rows[:, COL_V:COL_SK]), rows[:, COL_L:COL_L + LORA_COLS]], axis=1)


def _layer_weights(norm_attn, w_in, mu_shift, w0, w2, a0, a2, g2, k_k, k_a, r_k, ln_x_w, ln_x_b, sinks, w_out,
                   norm_ffn, w_gate, w_up, w_down, norm_final):
    c = RWKV_WIDTH
    swa0 = RWKV_COLS
    w_proj = jnp.concatenate([
        w_in[:, swa0:swa0 + SWA_WIDTH],
        _key_order(w_in[:, 0:c]), _key_order(w_in[:, c:2 * c]), _val_order(w_in[:, 2 * c:3 * c]),
        w_in[:, swa0 + SWA_WIDTH:swa0 + SWA_WIDTH + 2 * KV_WIDTH],
        jnp.pad(w_in[:, 3 * c:RWKV_COLS], ((0, 0), (0, LORA_PAD - LORA_COLS))),
    ], axis=1).astype(BF16)
    row = lambda v: v.reshape(1, -1)
    zero = jnp.zeros((c,), F32)
    kvec = _key_order(jnp.stack([mu_shift[0:c], mu_shift[c:2 * c], w0, a0, k_k, k_a, r_k.reshape(-1), zero]))
    vvec = _val_order(jnp.stack([mu_shift[2 * c:3 * c], ln_x_w, ln_x_b] + [zero] * 5))
    mu_l = jnp.pad(mu_shift[3 * c:], (0, LORA_PAD - LORA_COLS)).reshape(1, -1)
    lora_wa = _key_order(jnp.concatenate([w2, a2], axis=0)).astype(BF16)
    g2p = jnp.pad(_val_order(g2), ((0, LORA_PAD - LANES - GATE_LORA), (0, 0))).astype(BF16)
    sum_key, sum_val, key_to_val = _group_matrices()
    prep_params = (kvec, vvec, mu_l, lora_wa, g2p, jnp.asarray(sum_key, BF16))
    post_params = (kvec, vvec, jnp.asarray(sum_val, BF16), jnp.asarray(key_to_val, BF16))
    w_out_r = w_out[:c].reshape(RWKV_HEADS, I_GROUPS, I_LO, -1).swapaxes(0, 1).reshape(c, -1)
    return dict(
        norm_attn=row(norm_attn), w_proj=w_proj, prep=prep_params, post=post_params, sinks=sinks,
        w_out_r=w_out_r.astype(BF16), w_out_s=w_out[c:].astype(BF16), norm_ffn=row(norm_ffn),
        w_gate=w_gate.astype(BF16), w_up=w_up.astype(BF16), w_down=w_down.astype(BF16), norm_final=row(norm_final))


def _layer(x, shift_prev, state0, kv_past, lw, *, tiles):
    bsz, t_len, d = x.shape
    m = bsz * t_len
    x2 = x.reshape(m, d)
    proj = _inproj(x2, lw["norm_attn"], lw["w_proj"], tm=tiles["tm_in"], tn=tiles["tn_in"])

    fix_cols = _shift_to_cols(shift_prev)
    if bsz == 1:
        fixes, period = fix_cols, None
    else:
        fixes = tuple(jnp.pad(f[:, None, :], ((0, 0), (0, t_len - 1), (0, 0))).reshape(m, -1) for f in fix_cols)
        period = t_len
    a, w, b, k, r, v, g = _prep(proj, fixes, lw["prep"], tm=tiles["tm_prep"], period=period)

    tok = lambda z: z.reshape(bsz, t_len, SUBLANES, LANES)
    o, s_fin = _rwkv_scan(tok(a), tok(w), tok(b), tok(k), tok(r), tok(v), _state_to_kernel(state0), tb=tiles["tb"],
                          sub=min(SCAN_SUB, t_len), bb=min(tiles["bb_scan"], bsz))

    kv_shape = (bsz, WINDOW, SWA_KV_HEADS, HEAD_DIM)
    if kv_past is None:
        o_swa = _attn_prompt(proj, lw["sinks"], sub_blocks=tiles["attn_sub"])
        new_k = proj[m - WINDOW:, COL_SK:COL_SV].reshape(kv_shape)
        new_v = proj[m - WINDOW:, COL_SV:COL_L].reshape(kv_shape)
    else:
        k_past, v_past = (z.reshape(bsz, WINDOW, KV_WIDTH) for z in kv_past)
        o_swa, new_k, new_v = _attn_decode(proj.reshape(bsz, t_len, PROJ_COLS), k_past, v_past, lw["sinks"],
                                           bt=min(tiles["bt_attn"], bsz))
        o_swa, new_k, new_v = o_swa.reshape(m, SWA_WIDTH), new_k.reshape(kv_shape), new_v.reshape(kv_shape)

    x1, h2 = _outproj(x2, o.reshape(m, RWKV_WIDTH), r, k, v, g, o_swa, lw["post"], lw["w_out_r"], lw["w_out_s"],
                      lw["norm_ffn"], tm=tiles["tm_out"])
    y = _ffn(h2, x1, lw["w_gate"], lw["w_up"], lw["w_down"], lw["norm_final"], tm=tiles["tm_ffn"], tf=tiles["tf"])

    new_shift = _shift_from_proj(proj.reshape(bsz, t_len, PROJ_COLS)[:, -1])
    return y.reshape(bsz, t_len, d), _state_from_kernel(s_fin), new_shift, new_k, new_v


def _tiles(m, t_len):
    return dict(tm_in=min(1024, m), tn_in=1664, tm_prep=min(256, m), tb=min(256, t_len),
                tm_out=min(256, m), tm_ffn=min(512, m), tf=512, bt_attn=16, attn_sub=2, bb_scan=4)


def kernel(x_prompt, x_sample, state_rwkv, state_shift, cache_swa_k, cache_swa_v, norm_attn, w_in, mu_shift, w0, w2, a0, a2, g2, k_k, k_a, r_k, ln_x_w, ln_x_b, sinks, w_out, norm_ffn, w_gate, w_up, w_down, norm_final):
    assert norm_attn.shape[0] == 1, "single trunk layer"
    lw = _layer_weights(norm_attn[0], w_in[0], mu_shift[0], w0[0], w2[0], a0[0], a2[0], g2[0], k_k[0], k_a[0],
                        r_k[0], ln_x_w[0], ln_x_b[0], sinks[0], w_out[0], norm_ffn[0], w_gate[0], w_up[0],
                        w_down[0], norm_final)
    bp, tp, _ = x_prompt.shape
    bs, ts, _ = x_sample.shape
    assert bp == 1 and tp % BLOCK == 0 and ts < SUBLANES
    yp, p_state, p_shift, p_k, p_v = _layer(
        x_prompt, jnp.zeros((bp, RWKV_COLS), F32), jnp.zeros((bp, RWKV_HEADS, HEAD_DIM, HEAD_DIM), F32),
        None, lw, tiles=_tiles(bp * tp, tp))
    ys, s_state, s_shift, s_k, s_v = _layer(
        x_sample, state_shift[0], state_rwkv[0], (cache_swa_k[0], cache_swa_v[0]), lw, tiles=_tiles(bs * ts, ts))
    return (yp, ys, p_state[None], p_shift[None], p_k[None], p_v[None],
            s_state[None], s_shift[None], s_k[None], s_v[None])
```

```python
import functools

import numpy as np
import jax
import jax.numpy as jnp
from jax import lax
from jax.experimental import pallas as pl
from jax.experimental.pallas import tpu as pltpu

F32 = jnp.float32
BF16 = jnp.bfloat16

D_MODEL = 2048
HEAD_DIM = 64
RWKV_WIDTH = 1024
RWKV_HEADS = 16
SWA_WIDTH = 1024
SWA_HEADS = 16
SWA_KV_HEADS = 4
SWA_GROUP = 4
KV_WIDTH = 256
WINDOW = 128
BLOCK = 128
DECAY_LORA = 64
AAA_LORA = 64
GATE_LORA = 160
LORA_COLS = DECAY_LORA + AAA_LORA + GATE_LORA
RWKV_COLS = 3 * RWKV_WIDTH + LORA_COLS
D_FF = 5632
RMS_EPS = 1e-5
GN_EPS = 64e-5

SUBLANES = 8
LANES = 128
I_LO = LANES // RWKV_HEADS


def _reorder_last(x, split, a, b):
    return x.reshape(x.shape[:-1] + split).swapaxes(a, b).reshape(x.shape)


def _key_order(x):
    return _reorder_last(x, (RWKV_HEADS, HEAD_DIM), -1, -2)


def _key_natural(x):
    return _reorder_last(x, (HEAD_DIM, RWKV_HEADS), -1, -2)


def _val_order(x):
    return _reorder_last(x, (RWKV_HEADS, HEAD_DIM // I_LO, I_LO), -3, -2)


def _val_natural(x):
    return _reorder_last(x, (HEAD_DIM // I_LO, RWKV_HEADS, I_LO), -3, -2)


J_GROUPS = HEAD_DIM // SUBLANES
I_GROUPS = HEAD_DIM // I_LO
SCAN_SUB = 64


def _sublane_allsum(x):
    for shift in (4, 2, 1):
        x = x + pltpu.roll(x, shift, 0)
    return x


def _sublane_sums(xs):
    sub = lax.broadcasted_iota(jnp.int32, (SUBLANES, LANES), 0)

    def combine(x, y, d):
        clear = (sub & d) == 0
        return jnp.where(clear, x, pltpu.roll(y, d, 0)) + jnp.where(clear, pltpu.roll(x, SUBLANES - d, 0), y)

    z = [combine(xs[0], xs[4], 4), combine(xs[2], xs[6], 4), combine(xs[1], xs[5], 4), combine(xs[3], xs[7], 4)]
    return combine(combine(z[0], z[1], 2), combine(z[2], z[3], 2), 1)


def _cumprod_leading(x):
    n, shift = x.shape[0], 1
    while shift < n:
        x = jnp.concatenate([x[:shift], x[shift:] * x[:-shift]], axis=0)
        shift *= 2
    return x


def _scan_kernel(a_ref, w_ref, b_ref, k_ref, r_ref, v_ref, s0_ref, o_ref, sT_ref, s_scr, sa_scr, src_scr, pend_scr,
                 *tiles, tb, sub, bb):
    ti = pl.program_id(1)
    n_sub = tb // sub

    @pl.when(ti == 0)
    def _():
        s_scr[...] = s0_ref[...]

    gather_idx = (lax.broadcasted_iota(jnp.int32, (SUBLANES, LANES), 0) * RWKV_HEADS
                  + lax.broadcasted_iota(jnp.int32, (SUBLANES, LANES), 1) // I_LO)

    def gather(src_row):
        return jnp.take_along_axis(jnp.broadcast_to(src_row, (SUBLANES, LANES)), gather_idx, axis=1)

    A, B, K, R = range(4)
    tile = lambda bi, qi, parity: tiles[2 * qi + parity].at[bi]

    def prepare(bi):
        p_end = None
        for n in range(n_sub):
            rows = slice(n * sub, (n + 1) * sub)
            p = _cumprod_leading(w_ref[bi, rows])
            p_inv = 1.0 / p
            first = jnp.ones_like(p[:1]) if p_end is None else p_end[None]
            p_prev = jnp.concatenate([first, p[:-1]], axis=0)
            src_scr[bi, A, rows] = a_ref[bi, rows] * p_prev
            src_scr[bi, B, rows] = b_ref[bi, rows] * p_inv
            src_scr[bi, K, rows] = k_ref[bi, rows] * p_inv
            src_scr[bi, R, rows] = r_ref[bi, rows] * p
            p_end = p[sub - 1]
            for jg in range(J_GROUPS):
                pend_scr[bi, n, jg] = gather(p_end[jg:jg + 1, :])

    def build(bi, qi, t, parity):
        src_t = jnp.minimum(t, tb - 1)
        for jg in range(J_GROUPS):
            tile(bi, qi, parity)[jg] = gather(src_scr[bi, qi, src_t, pl.ds(jg, 1), :])

    def first_sa(bi):
        for qi in range(4):
            build(bi, qi, 0, 0)
        build(bi, A, 1, 1)
        sa0 = [None] * I_GROUPS
        for jg in range(J_GROUPS):
            a0 = tile(bi, A, 0)[jg]
            for ig in range(I_GROUPS):
                term = s_scr[bi, jg * I_GROUPS + ig] * a0
                sa0[ig] = term if sa0[ig] is None else sa0[ig] + term
        return [_sublane_allsum(x) for x in sa0]

    def step(bi, t, parity, sa, rescale=None):
        for qi in (B, K, R):
            build(bi, qi, t + 1, 1 - parity)
        build(bi, A, t + 2, parity)
        vb = [jnp.broadcast_to(v_ref[bi, t, pl.ds(ig, 1), :], (SUBLANES, LANES)) for ig in range(I_GROUPS)]
        out = [None] * I_GROUPS
        nxt = [None] * I_GROUPS
        for jg in range(J_GROUPS):
            b, k, r = (tile(bi, qi, parity)[jg] for qi in (B, K, R))
            a_next = tile(bi, A, 1 - parity)[jg]
            for ig in range(I_GROUPS):
                idx = jg * I_GROUPS + ig
                s_old = s_scr[bi, idx]
                if rescale is not None:
                    s_old = s_old * pend_scr[bi, rescale, jg]
                sj = s_old + sa[ig] * b + vb[ig] * k
                s_scr[bi, idx] = sj
                to, tn = sj * r, sj * a_next
                out[ig] = to if out[ig] is None else out[ig] + to
                nxt[ig] = tn if nxt[ig] is None else nxt[ig] + tn
        o_ref[bi, t] = _sublane_sums(out)
        sa_scr[bi, parity] = _sublane_sums(nxt)
        return [jnp.broadcast_to(sa_scr[bi, parity, pl.ds(ig, 1), :], (SUBLANES, LANES)) for ig in range(I_GROUPS)]

    def finish(bi):
        for jg in range(J_GROUPS):
            for ig in range(I_GROUPS):
                idx = jg * I_GROUPS + ig
                s_scr[bi, idx] = s_scr[bi, idx] * pend_scr[bi, n_sub - 1, jg]

    for bi in range(bb):
        prepare(bi)
    if tb <= SUBLANES:
        sas = [first_sa(bi) for bi in range(bb)]
        for t in range(tb):
            sas = [step(bi, t, t % 2, sas[bi]) for bi in range(bb)]
    else:
        for bi in range(bb):
            sa = first_sa(bi)
            for n in range(n_sub):
                t0 = n * sub
                sa = step(bi, t0, 0, sa, rescale=n - 1 if n else None)
                sa = step(bi, t0 + 1, 1, sa)

                def two_steps(u, sa, bi=bi):
                    return step(bi, 2 * u + 1, 1, step(bi, 2 * u, 0, sa))

                sa = lax.fori_loop(t0 // 2 + 1, (t0 + sub) // 2, two_steps, sa)
    for bi in range(bb):
        finish(bi)

    @pl.when(ti == pl.num_programs(1) - 1)
    def _():
        sT_ref[...] = s_scr[...]


def _rwkv_scan(a, w, b, k, r, v, s0, *, tb, sub, bb):
    bsz, t_len = a.shape[0], a.shape[1]
    assert t_len % tb == 0 and tb % sub == 0 and sub % 2 == 0 and bsz % bb == 0
    tok_spec = pl.BlockSpec((bb, tb, SUBLANES, LANES), lambda bi, ti: (bi, ti, 0, 0))
    st_spec = pl.BlockSpec((bb, HEAD_DIM, SUBLANES, LANES), lambda bi, ti: (bi, 0, 0, 0))
    return pl.pallas_call(
        functools.partial(_scan_kernel, tb=tb, sub=sub, bb=bb),
        grid=(bsz // bb, t_len // tb),
        in_specs=[tok_spec] * 6 + [st_spec],
        out_specs=[tok_spec, st_spec],
        out_shape=[jax.ShapeDtypeStruct(a.shape, F32), jax.ShapeDtypeStruct(s0.shape, F32)],
        scratch_shapes=([pltpu.VMEM((bb, HEAD_DIM, SUBLANES, LANES), F32), pltpu.VMEM((bb, 2, SUBLANES, LANES), F32),
                         pltpu.VMEM((bb, 4, tb, SUBLANES, LANES), F32),
                         pltpu.VMEM((bb, tb // sub, J_GROUPS, SUBLANES, LANES), F32)]
                        + [pltpu.VMEM((bb, J_GROUPS, SUBLANES, LANES), F32)] * 8),
        compiler_params=pltpu.CompilerParams(dimension_semantics=("parallel", "arbitrary"),
                                             vmem_limit_bytes=VMEM_LIMIT),
        name="rwkv_scan",
    )(a, w, b, k, r, v, s0)


PROJ_COLS = 4992
LORA_PAD = 384
COL_Q, COL_R, COL_K, COL_V, COL_SK, COL_SV, COL_L = 0, 1024, 2048, 3072, 4096, 4352, 4608
VMEM_LIMIT = 56 * 1024 * 1024
CAST_ROWS = 256


def _rms(x, g):
    return x * lax.rsqrt(jnp.mean(x * x, axis=-1, keepdims=True) + RMS_EPS) * g


def _inproj_kernel(x_ref, g_ref, w_ref, o_ref, xn_ref):
    @pl.when(pl.program_id(1) == 0)
    def _():
        xn_ref[...] = _rms(x_ref[...], g_ref[...]).astype(BF16)

    o_ref[...] = jnp.dot(xn_ref[...], w_ref[...], preferred_element_type=F32)


def _inproj(x, g, w, *, tm, tn):
    m, d = x.shape
    n = w.shape[1]
    return pl.pallas_call(
        _inproj_kernel,
        grid=(m // tm, n // tn),
        in_specs=[pl.BlockSpec((tm, d), lambda i, j: (i, 0)),
                  pl.BlockSpec((1, d), lambda i, j: (0, 0)),
                  pl.BlockSpec((d, tn), lambda i, j: (0, j))],
        out_specs=pl.BlockSpec((tm, tn), lambda i, j: (i, j)),
        out_shape=jax.ShapeDtypeStruct((m, n), F32),
        scratch_shapes=[pltpu.VMEM((tm, d), BF16)],
        compiler_params=pltpu.CompilerParams(dimension_semantics=("parallel", "arbitrary"),
                                             vmem_limit_bytes=VMEM_LIMIT),
        name="inproj",
    )(x, g, w)


def _cast_kernel(x_ref, o_ref):
    o_ref[...] = x_ref[...].astype(o_ref.dtype)


def _to_bf16(w, *, rows):
    r, c = w.shape
    assert r % rows == 0
    spec = pl.BlockSpec((rows, c), lambda i: (i, 0))
    return pl.pallas_call(
        _cast_kernel, grid=(r // rows,), in_specs=[spec], out_specs=spec,
        out_shape=jax.ShapeDtypeStruct((r, c), BF16),
        compiler_params=pltpu.CompilerParams(dimension_semantics=("parallel",), vmem_limit_bytes=VMEM_LIMIT),
        name="cast_bf16",
    )(w)


def _tile_sum(x):
    parts = [x[:, c * LANES:(c + 1) * LANES] for c in range(RWKV_WIDTH // LANES)]
    while len(parts) > 1:
        parts = [parts[i] + parts[i + 1] for i in range(0, len(parts), 2)]
    return parts[0]


def _tile8(x):
    return jnp.concatenate([x] * (RWKV_WIDTH // LANES), axis=1)


def _group_sum(x, m):
    hi = x.astype(BF16)
    rest = x - hi.astype(F32)
    mid = rest.astype(BF16)
    lo = (rest - mid.astype(F32)).astype(BF16)
    dot = lambda a: jnp.dot(a, m, preferred_element_type=F32)
    return dot(hi) + dot(mid) + dot(lo)


def _lora_dot(x, w):
    return jnp.dot(x.astype(BF16), w, preferred_element_type=F32)


KV_MU_R, KV_MU_K, KV_W0, KV_A0, KV_KK, KV_KA, KV_RK = range(7)
VV_MU_V, VV_LN_W, VV_LN_B = range(3)


def _prep_kernel(pr_ref, pk_ref, pv_ref, pl_ref, fr_ref, fk_ref, fv_ref, fl_ref,
                 kvec_ref, vvec_ref, mul_ref, wa_ref, g2_ref, sumh_ref,
                 a_out, w_out, b_out, k_out, r_out, v_out, g_out,
                 cr_ref, ck_ref, cv_ref, cl_ref, *, period):
    krow = lambda i: kvec_ref[i:i + 1, :]

    def shifted(p_ref, fix_ref, carry_ref, mu):
        p = p_ref[...]
        rolled = pltpu.roll(p, 1, 0)
        rows = lax.broadcasted_iota(jnp.int32, p.shape, 0)
        if period is None:
            @pl.when(pl.program_id(0) == 0)
            def _():
                carry_ref[...] = fix_ref[...]

            prev = jnp.where(rows == 0, carry_ref[...], rolled)
            carry_ref[...] = p[p.shape[0] - 1:, :]
        else:
            prev = jnp.where(rows % period == 0, fix_ref[...], rolled)
        return p + mu * (prev - p)

    r = shifted(pr_ref, fr_ref, cr_ref, krow(KV_MU_R))
    k = shifted(pk_ref, fk_ref, ck_ref, krow(KV_MU_K))
    v = shifted(pv_ref, fv_ref, cv_ref, vvec_ref[VV_MU_V:VV_MU_V + 1, :])
    xl = shifted(pl_ref, fl_ref, cl_ref, mul_ref[...])

    x_wa = xl[:, :LANES]
    is_w = lax.broadcasted_iota(jnp.int32, x_wa.shape, 1) < DECAY_LORA
    zw = krow(KV_W0) + _lora_dot(jnp.where(is_w, jnp.tanh(x_wa), 0.0), wa_ref[...])
    za = krow(KV_A0) + _lora_dot(jnp.where(is_w, 0.0, x_wa), wa_ref[...])
    g = _lora_dot(jax.nn.sigmoid(xl[:, LANES:]), g2_ref[...])
    y = -zw
    softplus = jnp.maximum(y, 0.0) + jnp.log(1.0 + jnp.exp(-jnp.abs(y)))
    decay = jnp.exp(-jnp.exp(-softplus - 0.5))
    a_sig = jax.nn.sigmoid(za)

    kk = k * krow(KV_KK)
    ssq = _tile8(_group_sum(_tile_sum(kk * kk), sumh_ref[...]))
    kk = kk / jnp.maximum(jnp.sqrt(ssq), 1e-12)

    a_out[...] = -kk
    w_out[...] = decay
    b_out[...] = kk * a_sig
    k_out[...] = k * (1.0 + (a_sig - 1.0) * krow(KV_KA))
    r_out[...] = r
    v_out[...] = v
    g_out[...] = g


def _prep(proj, fixes, params, *, tm, period):
    m = proj.shape[0]
    c = RWKV_WIDTH
    grid = (m // tm,)

    def col(width, idx):
        return pl.BlockSpec((tm, width), lambda i: (i, idx))

    def full(arr):
        return pl.BlockSpec(arr.shape, lambda i: (0,) * arr.ndim)

    if period is None:
        fix_specs = [full(f) for f in fixes]
    else:
        fix_specs = [pl.BlockSpec((tm, f.shape[1]), lambda i: (i, 0)) for f in fixes]
    out_spec = pl.BlockSpec((tm, c), lambda i: (i, 0))
    return pl.pallas_call(
        functools.partial(_prep_kernel, period=period),
        grid=grid,
        in_specs=[col(c, COL_R // c), col(c, COL_K // c), col(c, COL_V // c), col(LORA_PAD, COL_L // LORA_PAD)]
        + fix_specs + [full(p) for p in params],
        out_specs=[out_spec] * 7,
        out_shape=[jax.ShapeDtypeStruct((m, c), F32)] * 7,
        scratch_shapes=[pltpu.VMEM((1, c), F32)] * 3 + [pltpu.VMEM((1, LORA_PAD), F32)],
        compiler_params=pltpu.CompilerParams(dimension_semantics=("arbitrary",), vmem_limit_bytes=VMEM_LIMIT),
        name="rwkv_prep",
    )(proj, proj, proj, proj, *fixes, *params)


def _rwkv_out(o, r, k, v, g, kvec_ref, vvec_ref, gsum_ref, kv_ref):
    inv_n = 1.0 / HEAD_DIM
    mean = _tile8(_group_sum(_tile_sum(o), gsum_ref[...])) * inv_n
    d = o - mean
    var = _tile8(_group_sum(_tile_sum(d * d), gsum_ref[...])) * inv_n
    normed = (d * lax.rsqrt(var + GN_EPS) * vvec_ref[VV_LN_W:VV_LN_W + 1, :]
              + vvec_ref[VV_LN_B:VV_LN_B + 1, :])
    rk = _tile8(_group_sum(_tile_sum(r * k * kvec_ref[KV_RK:KV_RK + 1, :]), kv_ref[...]))
    return ((normed + rk * v) * g).astype(BF16)


ALIBI_SLOPES = [2.0 ** (-8.0 * (i + 1) / SWA_HEADS) for i in range(SWA_HEADS)]
SCORE_SCALE = HEAD_DIM ** -0.5


LOG2E = 1.4426950408889634


def _attn_block_kernel(sink_ref, q_ref, kc_ref, vc_ref, kp_ref, vp_ref, o_ref, ot_scr, bias_scr, *, sub_blocks):
    n = pl.program_id(0)
    span = WINDOW + BLOCK

    @pl.when(n == 0)
    def _():
        si = lax.broadcasted_iota(jnp.int32, (span, BLOCK), 0)
        qi = lax.broadcasted_iota(jnp.int32, (span, BLOCK), 1)
        dist = qi - si + WINDOW
        valid = (dist >= 0) & (dist < WINDOW)
        distf = dist.astype(F32)
        for h in range(SWA_HEADS):
            bias = jnp.where(valid, (-ALIBI_SLOPES[h] * LOG2E) * distf, -jnp.inf)
            bias_scr[1, h] = bias
            bias_scr[0, h] = jnp.where(si >= WINDOW, bias, -jnp.inf)

    for sb in range(sub_blocks):
        rows = slice(sb * BLOCK, (sb + 1) * BLOCK)
        if sb == 0:
            k_prev, v_prev = kp_ref[...], vp_ref[...]
            table = jnp.where(n == 0, 0, 1)
        else:
            prev_rows = slice((sb - 1) * BLOCK, sb * BLOCK)
            k_prev, v_prev = kc_ref[prev_rows, :], vc_ref[prev_rows, :]
            table = 1
        k_span = jnp.concatenate([k_prev, kc_ref[rows, :]], axis=0)
        v_t = jnp.concatenate([v_prev, vc_ref[rows, :]], axis=0).T.astype(BF16)
        for kh in range(SWA_KV_HEADS):
            ks = k_span[:, kh * HEAD_DIM:(kh + 1) * HEAD_DIM].astype(BF16)
            heads = range(kh * SWA_GROUP, (kh + 1) * SWA_GROUP)
            qg = jnp.concatenate([q_ref[rows, h * HEAD_DIM:(h + 1) * HEAD_DIM] for h in heads], axis=0).astype(BF16)
            s_all = lax.dot_general(ks, qg, (((1,), (1,)), ((), ())), preferred_element_type=F32)
            probs, dens = [], []
            for gi, h in enumerate(heads):
                s = s_all[:, gi * BLOCK:(gi + 1) * BLOCK] * (SCORE_SCALE * LOG2E) + bias_scr[table, h]
                sink = sink_ref[h] * LOG2E
                m = jnp.maximum(jnp.max(s, axis=0, keepdims=True), sink)
                e = jnp.exp2(s - m)
                dens.append(jnp.sum(e, axis=0, keepdims=True) + jnp.exp2(sink - m))
                probs.append(e.astype(BF16))
            o_t = jnp.dot(v_t[kh * HEAD_DIM:(kh + 1) * HEAD_DIM, :], jnp.concatenate(probs, axis=1),
                          preferred_element_type=F32) / jnp.concatenate(dens, axis=1)
            for gi, h in enumerate(heads):
                ot_scr[sb, h * HEAD_DIM:(h + 1) * HEAD_DIM, :] = o_t[:, gi * BLOCK:(gi + 1) * BLOCK]
        o_ref[rows, :] = ot_scr[sb].T.astype(BF16)


def _attn_prompt(proj, sinks, *, sub_blocks):
    t = proj.shape[0]
    tq = sub_blocks * BLOCK
    cur = lambda width, idx: pl.BlockSpec((tq, width), lambda n: (n, idx))
    prev = lambda width, idx: pl.BlockSpec((BLOCK, width), lambda n: (jnp.maximum(n * sub_blocks - 1, 0), idx))
    return pl.pallas_call(
        functools.partial(_attn_block_kernel, sub_blocks=sub_blocks),
        grid=(t // tq,),
        in_specs=[pl.BlockSpec(memory_space=pltpu.SMEM),
                  cur(SWA_WIDTH, COL_Q // SWA_WIDTH), cur(KV_WIDTH, COL_SK // KV_WIDTH), cur(KV_WIDTH, COL_SV // KV_WIDTH),
                  prev(KV_WIDTH, COL_SK // KV_WIDTH), prev(KV_WIDTH, COL_SV // KV_WIDTH)],
        out_specs=pl.BlockSpec((tq, SWA_WIDTH), lambda n: (n, 0)),
        out_shape=jax.ShapeDtypeStruct((t, SWA_WIDTH), BF16),
        scratch_shapes=[pltpu.VMEM((sub_blocks, SWA_WIDTH, BLOCK), F32),
                        pltpu.VMEM((2, SWA_HEADS, WINDOW + BLOCK, BLOCK), F32)],
        compiler_params=pltpu.CompilerParams(dimension_semantics=("arbitrary",), vmem_limit_bytes=VMEM_LIMIT),
        name="swa_prompt",
    )(sinks, proj, proj, proj, proj, proj)


def _attn_decode_kernel(sink_ref, q_ref, kn_ref, vn_ref, kp_ref, vp_ref, o_ref, ko_ref, vo_ref, *, t_new, bt):
    span = WINDOW + SUBLANES
    rows = SWA_GROUP * SUBLANES
    row = lax.broadcasted_iota(jnp.int32, (bt, rows, span), 1)
    si = lax.broadcasted_iota(jnp.int32, (bt, rows, span), 2)
    dist = row % SUBLANES - si + WINDOW
    valid = (dist >= 0) & (dist < WINDOW)
    distf = dist.astype(F32)
    grp = row // SUBLANES
    grp_col = grp[:, :, :1]

    def per_group(tile_grp, vals):
        out = vals[-1]
        for gi in range(SWA_GROUP - 2, -1, -1):
            out = jnp.where(tile_grp == gi, vals[gi], out)
        return out

    pad = jnp.zeros((bt, SUBLANES - t_new, HEAD_DIM), F32)
    for kh in range(SWA_KV_HEADS):
        lanes = slice(kh * HEAD_DIM, (kh + 1) * HEAD_DIM)
        heads = range(kh * SWA_GROUP, (kh + 1) * SWA_GROUP)
        ks = jnp.concatenate([kp_ref[:, :, lanes], kn_ref[:, :, lanes], pad], axis=1).astype(BF16)
        vs = jnp.concatenate([vp_ref[:, :, lanes], vn_ref[:, :, lanes], pad], axis=1).astype(BF16)
        qg = jnp.concatenate([piece for h in heads for piece in (q_ref[:, :, h * HEAD_DIM:(h + 1) * HEAD_DIM], pad)],
                             axis=1).astype(BF16)
        s = jnp.einsum("bqd,bsd->bqs", qg, ks, preferred_element_type=F32) * SCORE_SCALE
        slope = per_group(grp, [ALIBI_SLOPES[h] for h in heads])
        s = jnp.where(valid, s - slope * distf, -jnp.inf)
        sink = per_group(grp_col, [sink_ref[h] for h in heads])
        m = jnp.maximum(jnp.max(s, axis=-1, keepdims=True), sink)
        e = jnp.exp(s - m)
        den = jnp.sum(e, axis=-1, keepdims=True) + jnp.exp(sink - m)
        og = jnp.einsum("bqs,bsd->bqd", (e / den).astype(BF16), vs, preferred_element_type=F32)
        for gi, h in enumerate(heads):
            o_ref[:, :, h * HEAD_DIM:(h + 1) * HEAD_DIM] = og[:, gi * SUBLANES:gi * SUBLANES + t_new, :].astype(BF16)
    ko_ref[:, :WINDOW - t_new, :] = kp_ref[:, t_new:, :]
    ko_ref[:, WINDOW - t_new:, :] = kn_ref[...]
    vo_ref[:, :WINDOW - t_new, :] = vp_ref[:, t_new:, :]
    vo_ref[:, WINDOW - t_new:, :] = vn_ref[...]


def _attn_decode(proj3, k_past, v_past, sinks, *, bt):
    bsz, t_new = proj3.shape[0], proj3.shape[1]
    new = lambda width, idx: pl.BlockSpec((bt, t_new, width), lambda b: (b, 0, idx))
    past = pl.BlockSpec((bt, WINDOW, KV_WIDTH), lambda b: (b, 0, 0))
    return pl.pallas_call(
        functools.partial(_attn_decode_kernel, t_new=t_new, bt=bt),
        grid=(bsz // bt,),
        in_specs=[pl.BlockSpec(memory_space=pltpu.SMEM),
                  new(SWA_WIDTH, COL_Q // SWA_WIDTH), new(KV_WIDTH, COL_SK // KV_WIDTH), new(KV_WIDTH, COL_SV // KV_WIDTH),
                  past, past],
        out_specs=[pl.BlockSpec((bt, t_new, SWA_WIDTH), lambda b: (b, 0, 0)), past, past],
        out_shape=[jax.ShapeDtypeStruct((bsz, t_new, SWA_WIDTH), BF16),
                   jax.ShapeDtypeStruct(k_past.shape, F32), jax.ShapeDtypeStruct(v_past.shape, F32)],
        compiler_params=pltpu.CompilerParams(dimension_semantics=("parallel",), vmem_limit_bytes=VMEM_LIMIT),
        name="swa_decode",
    )(sinks, proj3, proj3, proj3, k_past, v_past)


def _outproj_kernel(x_ref, o_ref, r_ref, k_ref, v_ref, g_ref, osw_ref, kvec_ref, vvec_ref, gsum_ref, kv_ref,
                    wr_ref, ws_ref, gn_ref, x1_ref, h2_ref):
    o_rwkv = _rwkv_out(o_ref[...], r_ref[...], k_ref[...], v_ref[...], g_ref[...], kvec_ref, vvec_ref, gsum_ref, kv_ref)
    x1 = (x_ref[...] + jnp.dot(o_rwkv, wr_ref[...], preferred_element_type=F32)
          + jnp.dot(osw_ref[...], ws_ref[...], preferred_element_type=F32))
    x1_ref[...] = x1
    h2_ref[...] = _rms(x1, gn_ref[...]).astype(BF16)


def _outproj(x, o, r, k, v, g, o_swa, post_params, w_r, w_s, g_norm, *, tm):
    m, d = x.shape
    row = lambda width: pl.BlockSpec((tm, width), lambda i: (i, 0))
    full = lambda arr: pl.BlockSpec(arr.shape, lambda i: (0, 0))
    consts = (*post_params, w_r, w_s, g_norm)
    return pl.pallas_call(
        _outproj_kernel,
        grid=(m // tm,),
        in_specs=[row(d)] + [row(RWKV_WIDTH)] * 5 + [row(SWA_WIDTH)] + [full(c) for c in consts],
        out_specs=[row(d), row(d)],
        out_shape=[jax.ShapeDtypeStruct((m, d), F32), jax.ShapeDtypeStruct((m, d), BF16)],
        compiler_params=pltpu.CompilerParams(dimension_semantics=("parallel",), vmem_limit_bytes=VMEM_LIMIT),
        name="outproj",
    )(x, o, r, k, v, g, o_swa, *consts)


def _ffn_kernel(h_ref, x1_ref, wg_ref, wu_ref, wd_ref, gf_ref, y_ref, acc_ref):
    j = pl.program_id(1)

    @pl.when(j == 0)
    def _():
        acc_ref[...] = jnp.zeros_like(acc_ref)

    h = h_ref[...]
    gate = jnp.dot(h, wg_ref[...], preferred_element_type=F32)
    up = jnp.dot(h, wu_ref[...], preferred_element_type=F32)
    act = (gate * jax.nn.sigmoid(gate) * up).astype(BF16)
    acc_ref[...] += jnp.dot(act, wd_ref[...], preferred_element_type=F32)

    @pl.when(j == pl.num_programs(1) - 1)
    def _():
        y_ref[...] = _rms(x1_ref[...] + acc_ref[...], gf_ref[...])


def _ffn(h2, x1, w_gate, w_up, w_down, g_final, *, tm, tf):
    m, d = x1.shape
    f = w_gate.shape[1]
    return pl.pallas_call(
        _ffn_kernel,
        grid=(m // tm, f // tf),
        in_specs=[pl.BlockSpec((tm, d), lambda i, j: (i, 0)),
                  pl.BlockSpec((tm, d), lambda i, j: (i, 0)),
                  pl.BlockSpec((d, tf), lambda i, j: (0, j)),
                  pl.BlockSpec((d, tf), lambda i, j: (0, j)),
                  pl.BlockSpec((tf, d), lambda i, j: (j, 0)),
                  pl.BlockSpec((1, d), lambda i, j: (0, 0))],
        out_specs=pl.BlockSpec((tm, d), lambda i, j: (i, 0)),
        out_shape=jax.ShapeDtypeStruct((m, d), F32),
        scratch_shapes=[pltpu.VMEM((tm, d), F32)],
        compiler_params=pltpu.CompilerParams(dimension_semantics=("parallel", "arbitrary"),
                                             vmem_limit_bytes=VMEM_LIMIT),
        name="ffn",
    )(h2, x1, w_gate, w_up, w_down, g_final)


def _group_matrices():
    lane = np.arange(LANES)
    key_head = lane % RWKV_HEADS
    val_head = lane // I_LO
    sum_key = (key_head[:, None] == key_head[None, :]).astype(np.float32)
    sum_val = (val_head[:, None] == val_head[None, :]).astype(np.float32)
    key_to_val = (key_head[:, None] == val_head[None, :]).astype(np.float32)
    return sum_key, sum_val, key_to_val


def _state_to_kernel(s):
    b = s.shape[0]
    s = s.reshape(b, RWKV_HEADS, I_GROUPS, I_LO, J_GROUPS, SUBLANES)
    return s.transpose(0, 4, 2, 5, 1, 3).reshape(b, HEAD_DIM, SUBLANES, LANES)


def _state_from_kernel(s):
    b = s.shape[0]
    s = s.reshape(b, J_GROUPS, I_GROUPS, SUBLANES, RWKV_HEADS, I_LO)
    return s.transpose(0, 4, 2, 5, 1, 3).reshape(b, RWKV_HEADS, HEAD_DIM, HEAD_DIM)


def _shift_to_cols(shift):
    c = RWKV_WIDTH
    return (_key_order(shift[:, 0:c]), _key_order(shift[:, c:2 * c]), _val_order(shift[:, 2 * c:3 * c]),
            jnp.pad(shift[:, 3 * c:], ((0, 0), (0, LORA_PAD - LORA_COLS))))


def _shift_from_proj(rows):
    return jnp.concatenate([_key_natural(rows[:, COL_R:COL_K]), _key_natural(rows[:, COL_K:COL_V]),
                            _val_natural(rows[:, COL_V:COL_SK]), rows[:, COL_L:COL_L + LORA_COLS]], axis=1)


def _layer_weights(norm_attn, w_in, mu_shift, w0, w2, a0, a2, g2, k_k, k_a, r_k, ln_x_w, ln_x_b, sinks, w_out,
                   norm_ffn, w_gate, w_up, w_down, norm_final):
    c = RWKV_WIDTH
    swa0 = RWKV_COLS
    w_proj = jnp.concatenate([
        w_in[:, swa0:swa0 + SWA_WIDTH],
        _key_order(w_in[:, 0:c]), _key_order(w_in[:, c:2 * c]), _val_order(w_in[:, 2 * c:3 * c]),
        w_in[:, swa0 + SWA_WIDTH:swa0 + SWA_WIDTH + 2 * KV_WIDTH],
        jnp.pad(w_in[:, 3 * c:RWKV_COLS], ((0, 0), (0, LORA_PAD - LORA_COLS))),
    ], axis=1).astype(BF16)
    row = lambda v: v.reshape(1, -1)
    zero = jnp.zeros((c,), F32)
    kvec = _key_order(jnp.stack([mu_shift[0:c], mu_shift[c:2 * c], w0, a0, k_k, k_a, r_k.reshape(-1), zero]))
    vvec = _val_order(jnp.stack([mu_shift[2 * c:3 * c], ln_x_w, ln_x_b] + [zero] * 5))
    mu_l = jnp.pad(mu_shift[3 * c:], (0, LORA_PAD - LORA_COLS)).reshape(1, -1)
    lora_wa = _key_order(jnp.concatenate([w2, a2], axis=0)).astype(BF16)
    g2p = jnp.pad(_val_order(g2), ((0, LORA_PAD - LANES - GATE_LORA), (0, 0))).astype(BF16)
    sum_key, sum_val, key_to_val = _group_matrices()
    prep_params = (kvec, vvec, mu_l, lora_wa, g2p, jnp.asarray(sum_key, BF16))
    post_params = (kvec, vvec, jnp.asarray(sum_val, BF16), jnp.asarray(key_to_val, BF16))
    w_out_r = w_out[:c].reshape(RWKV_HEADS, I_GROUPS, I_LO, -1).swapaxes(0, 1).reshape(c, -1)
    return dict(
        norm_attn=row(norm_attn), w_proj=w_proj, prep=prep_params, post=post_params, sinks=sinks,
        w_out_r=w_out_r.astype(BF16), w_out_s=w_out[c:].astype(BF16), norm_ffn=row(norm_ffn),
        w_gate=_to_bf16(w_gate, rows=CAST_ROWS), w_up=_to_bf16(w_up, rows=CAST_ROWS),
        w_down=_to_bf16(w_down, rows=CAST_ROWS), norm_final=row(norm_final))


def _layer(x, shift_prev, state0, kv_past, lw, *, tiles):
    bsz, t_len, d = x.shape
    m = bsz * t_len
    x2 = x.reshape(m, d)
    proj = _inproj(x2, lw["norm_attn"], lw["w_proj"], tm=tiles["tm_in"], tn=tiles["tn_in"])

    fix_cols = _shift_to_cols(shift_prev)
    if bsz == 1:
        fixes, period = fix_cols, None
    else:
        fixes = tuple(jnp.pad(f[:, None, :], ((0, 0), (0, t_len - 1), (0, 0))).reshape(m, -1) for f in fix_cols)
        period = t_len
    a, w, b, k, r, v, g = _prep(proj, fixes, lw["prep"], tm=tiles["tm_prep"], period=period)

    tok = lambda z: z.reshape(bsz, t_len, SUBLANES, LANES)
    o, s_fin = _rwkv_scan(tok(a), tok(w), tok(b), tok(k), tok(r), tok(v), _state_to_kernel(state0), tb=tiles["tb"],
                          sub=min(SCAN_SUB, t_len), bb=min(tiles["bb_scan"], bsz))

    kv_shape = (bsz, WINDOW, SWA_KV_HEADS, HEAD_DIM)
    if kv_past is None:
        o_swa = _attn_prompt(proj, lw["sinks"], sub_blocks=tiles["attn_sub"])
        new_k = proj[m - WINDOW:, COL_SK:COL_SV].reshape(kv_shape)
        new_v = proj[m - WINDOW:, COL_SV:COL_L].reshape(kv_shape)
    else:
        k_past, v_past = (z.reshape(bsz, WINDOW, KV_WIDTH) for z in kv_past)
        o_swa, new_k, new_v = _attn_decode(proj.reshape(bsz, t_len, PROJ_COLS), k_past, v_past, lw["sinks"],
                                           bt=min(tiles["bt_attn"], bsz))
        o_swa, new_k, new_v = o_swa.reshape(m, SWA_WIDTH), new_k.reshape(kv_shape), new_v.reshape(kv_shape)

    x1, h2 = _outproj(x2, o.reshape(m, RWKV_WIDTH), r, k, v, g, o_swa, lw["post"], lw["w_out_r"], lw["w_out_s"],
                      lw["norm_ffn"], tm=tiles["tm_out"])
    y = _ffn(h2, x1, lw["w_gate"], lw["w_up"], lw["w_down"], lw["norm_final"], tm=tiles["tm_ffn"], tf=tiles["tf"])

    new_shift = _shift_from_proj(proj.reshape(bsz, t_len, PROJ_COLS)[:, -1])
    return y.reshape(bsz, t_len, d), _state_from_kernel(s_fin), new_shift, new_k, new_v


def _tiles(m, t_len):
    return dict(tm_in=min(1024, m), tn_in=1664, tm_prep=min(256, m), tb=min(256, t_len),
                tm_out=min(256, m), tm_ffn=min(512, m), tf=512, bt_attn=16, attn_sub=4, bb_scan=4)


def kernel(x_prompt, x_sample, state_rwkv, state_shift, cache_swa_k, cache_swa_v, norm_attn, w_in, mu_shift, w0, w2, a0, a2, g2, k_k, k_a, r_k, ln_x_w, ln_x_b, sinks, w_out, norm_ffn, w_gate, w_up, w_down, norm_final):
    assert norm_attn.shape[0] == 1, "single trunk layer"
    lw = _layer_weights(norm_attn[0], w_in[0], mu_shift[0], w0[0], w2[0], a0[0], a2[0], g2[0], k_k[0], k_a[0],
                        r_k[0], ln_x_w[0], ln_x_b[0], sinks[0], w_out[0], norm_ffn[0], w_gate[0], w_up[0],
                        w_down[0], norm_final)
    bp, tp, _ = x_prompt.shape
    bs, ts, _ = x_sample.shape
    assert bp == 1 and tp % BLOCK == 0 and ts < SUBLANES
    yp, p_state, p_shift, p_k, p_v = _layer(
        x_prompt, jnp.zeros((bp, RWKV_COLS), F32), jnp.zeros((bp, RWKV_HEADS, HEAD_DIM, HEAD_DIM), F32),
        None, lw, tiles=_tiles(bp * tp, tp))
    ys, s_state, s_shift, s_k, s_v = _layer(
        x_sample, state_shift[0], state_rwkv[0], (cache_swa_k[0], cache_swa_v[0]), lw, tiles=_tiles(bs * ts, ts))
    return (yp, ys, p_state[None], p_shift[None], p_k[None], p_v[None],
            s_state[None], s_shift[None], s_k[None], s_v[None])
```

```python
import functools

import numpy as np
import jax
import jax.numpy as jnp
from jax import lax
from jax.experimental import pallas as pl
from jax.experimental.pallas import tpu as pltpu

F32 = jnp.float32
BF16 = jnp.bfloat16

D_MODEL = 2048
HEAD_DIM = 64
RWKV_WIDTH = 1024
RWKV_HEADS = 16
SWA_WIDTH = 1024
SWA_HEADS = 16
SWA_KV_HEADS = 4
SWA_GROUP = 4
KV_WIDTH = 256
WINDOW = 128
BLOCK = 128
DECAY_LORA = 64
AAA_LORA = 64
GATE_LORA = 160
LORA_COLS = DECAY_LORA + AAA_LORA + GATE_LORA
RWKV_COLS = 3 * RWKV_WIDTH + LORA_COLS
D_FF = 5632
RMS_EPS = 1e-5
GN_EPS = 64e-5

SUBLANES = 8
LANES = 128
I_LO = LANES // RWKV_HEADS


def _reorder_last(x, split, a, b):
    return x.reshape(x.shape[:-1] + split).swapaxes(a, b).reshape(x.shape)


def _key_order(x):
    return _reorder_last(x, (RWKV_HEADS, HEAD_DIM), -1, -2)


def _key_natural(x):
    return _reorder_last(x, (HEAD_DIM, RWKV_HEADS), -1, -2)


def _val_order(x):
    return _reorder_last(x, (RWKV_HEADS, HEAD_DIM // I_LO, I_LO), -3, -2)


def _val_natural(x):
    return _reorder_last(x, (HEAD_DIM // I_LO, RWKV_HEADS, I_LO), -3, -2)


J_GROUPS = HEAD_DIM // SUBLANES
I_GROUPS = HEAD_DIM // I_LO
SCAN_SUB = 64


def _sublane_allsum(x):
    for shift in (4, 2, 1):
        x = x + pltpu.roll(x, shift, 0)
    return x


def _sublane_sums(xs):
    sub = lax.broadcasted_iota(jnp.int32, (SUBLANES, LANES), 0)

    def combine(x, y, d):
        clear = (sub & d) == 0
        return jnp.where(clear, x, pltpu.roll(y, d, 0)) + jnp.where(clear, pltpu.roll(x, SUBLANES - d, 0), y)

    z = [combine(xs[0], xs[4], 4), combine(xs[2], xs[6], 4), combine(xs[1], xs[5], 4), combine(xs[3], xs[7], 4)]
    return combine(combine(z[0], z[1], 2), combine(z[2], z[3], 2), 1)


def _cumprod_leading(x):
    n, shift = x.shape[0], 1
    while shift < n:
        x = jnp.concatenate([x[:shift], x[shift:] * x[:-shift]], axis=0)
        shift *= 2
    return x


def _scan_kernel(a_ref, w_ref, b_ref, k_ref, r_ref, v_ref, s0_ref, o_ref, sT_ref, s_scr, sa_scr, src_scr, pend_scr,
                 *tiles, tb, sub, bb):
    ti = pl.program_id(1)
    n_sub = tb // sub

    @pl.when(ti == 0)
    def _():
        s_scr[...] = s0_ref[...]

    gather_idx = (lax.broadcasted_iota(jnp.int32, (SUBLANES, LANES), 0) * RWKV_HEADS
                  + lax.broadcasted_iota(jnp.int32, (SUBLANES, LANES), 1) // I_LO)

    def gather(src_row):
        return jnp.take_along_axis(jnp.broadcast_to(src_row, (SUBLANES, LANES)), gather_idx, axis=1)

    A, B, K, R = range(4)
    tile = lambda bi, qi, parity: tiles[2 * qi + parity].at[bi]

    def prepare(bi):
        p_end = None
        for n in range(n_sub):
            rows = slice(n * sub, (n + 1) * sub)
            p = _cumprod_leading(w_ref[bi, rows])
            p_inv = 1.0 / p
            first = jnp.ones_like(p[:1]) if p_end is None else p_end[None]
            p_prev = jnp.concatenate([first, p[:-1]], axis=0)
            src_scr[bi, A, rows] = a_ref[bi, rows] * p_prev
            src_scr[bi, B, rows] = b_ref[bi, rows] * p_inv
            src_scr[bi, K, rows] = k_ref[bi, rows] * p_inv
            src_scr[bi, R, rows] = r_ref[bi, rows] * p
            p_end = p[sub - 1]
            for jg in range(J_GROUPS):
                pend_scr[bi, n, jg] = gather(p_end[jg:jg + 1, :])

    def build(bi, qi, t, parity):
        src_t = jnp.minimum(t, tb - 1)
        for jg in range(J_GROUPS):
            tile(bi, qi, parity)[jg] = gather(src_scr[bi, qi, src_t, pl.ds(jg, 1), :])

    def first_sa(bi):
        for qi in range(4):
            build(bi, qi, 0, 0)
        build(bi, A, 1, 1)
        sa0 = [None] * I_GROUPS
        for jg in range(J_GROUPS):
            a0 = tile(bi, A, 0)[jg]
            for ig in range(I_GROUPS):
                term = s_scr[bi, jg * I_GROUPS + ig] * a0
                sa0[ig] = term if sa0[ig] is None else sa0[ig] + term
        return [_sublane_allsum(x) for x in sa0]

    def step(bi, t, parity, sa, rescale=None):
        for qi in (B, K, R):
            build(bi, qi, t + 1, 1 - parity)
        build(bi, A, t + 2, parity)
        vb = [jnp.broadcast_to(v_ref[bi, t, pl.ds(ig, 1), :], (SUBLANES, LANES)) for ig in range(I_GROUPS)]
        out = [None] * I_GROUPS
        nxt = [None] * I_GROUPS
        for jg in range(J_GROUPS):
            b, k, r = (tile(bi, qi, parity)[jg] for qi in (B, K, R))
            a_next = tile(bi, A, 1 - parity)[jg]
            for ig in range(I_GROUPS):
                idx = jg * I_GROUPS + ig
                s_old = s_scr[bi, idx]
                if rescale is not None:
                    s_old = s_old * pend_scr[bi, rescale, jg]
                sj = s_old + sa[ig] * b + vb[ig] * k
                s_scr[bi, idx] = sj
                to, tn = sj * r, sj * a_next
                out[ig] = to if out[ig] is None else out[ig] + to
                nxt[ig] = tn if nxt[ig] is None else nxt[ig] + tn
        o_ref[bi, t] = _sublane_sums(out)
        sa_scr[bi, parity] = _sublane_sums(nxt)
        return [jnp.broadcast_to(sa_scr[bi, parity, pl.ds(ig, 1), :], (SUBLANES, LANES)) for ig in range(I_GROUPS)]

    def finish(bi):
        for jg in range(J_GROUPS):
            for ig in range(I_GROUPS):
                idx = jg * I_GROUPS + ig
                s_scr[bi, idx] = s_scr[bi, idx] * pend_scr[bi, n_sub - 1, jg]

    for bi in range(bb):
        prepare(bi)
    if tb <= SUBLANES:
        sas = [first_sa(bi) for bi in range(bb)]
        for t in range(tb):
            sas = [step(bi, t, t % 2, sas[bi]) for bi in range(bb)]
    else:
        for bi in range(bb):
            sa = first_sa(bi)
            for n in range(n_sub):
                t0 = n * sub
                sa = step(bi, t0, 0, sa, rescale=n - 1 if n else None)
                sa = step(bi, t0 + 1, 1, sa)

                def two_steps(u, sa, bi=bi):
                    return step(bi, 2 * u + 1, 1, step(bi, 2 * u, 0, sa))

                sa = lax.fori_loop(t0 // 2 + 1, (t0 + sub) // 2, two_steps, sa)
    for bi in range(bb):
        finish(bi)

    @pl.when(ti == pl.num_programs(1) - 1)
    def _():
        sT_ref[...] = s_scr[...]


def _rwkv_scan(a, w, b, k, r, v, s0, *, tb, sub, bb):
    bsz, t_len = a.shape[0], a.shape[1]
    assert t_len % tb == 0 and tb % sub == 0 and sub % 2 == 0 and bsz % bb == 0
    tok_spec = pl.BlockSpec((bb, tb, SUBLANES, LANES), lambda bi, ti: (bi, ti, 0, 0))
    st_spec = pl.BlockSpec((bb, HEAD_DIM, SUBLANES, LANES), lambda bi, ti: (bi, 0, 0, 0))
    return pl.pallas_call(
        functools.partial(_scan_kernel, tb=tb, sub=sub, bb=bb),
        grid=(bsz // bb, t_len // tb),
        in_specs=[tok_spec] * 6 + [st_spec],
        out_specs=[tok_spec, st_spec],
        out_shape=[jax.ShapeDtypeStruct(a.shape, F32), jax.ShapeDtypeStruct(s0.shape, F32)],
        scratch_shapes=([pltpu.VMEM((bb, HEAD_DIM, SUBLANES, LANES), F32), pltpu.VMEM((bb, 2, SUBLANES, LANES), F32),
                         pltpu.VMEM((bb, 4, tb, SUBLANES, LANES), F32),
                         pltpu.VMEM((bb, tb // sub, J_GROUPS, SUBLANES, LANES), F32)]
                        + [pltpu.VMEM((bb, J_GROUPS, SUBLANES, LANES), F32)] * 8),
        compiler_params=pltpu.CompilerParams(dimension_semantics=("parallel", "arbitrary"),
                                             vmem_limit_bytes=VMEM_LIMIT),
        name="rwkv_scan",
    )(a, w, b, k, r, v, s0)


PROJ_COLS = 4992
LORA_PAD = 384
COL_Q, COL_R, COL_K, COL_V, COL_SK, COL_SV, COL_L = 0, 1024, 2048, 3072, 4096, 4352, 4608
VMEM_LIMIT = 56 * 1024 * 1024


def _rms(x, g):
    return x * lax.rsqrt(jnp.mean(x * x, axis=-1, keepdims=True) + RMS_EPS) * g


def _inproj_kernel(x_ref, g_ref, w_ref, o_ref, xn_ref):
    @pl.when(pl.program_id(1) == 0)
    def _():
        xn_ref[...] = _rms(x_ref[...], g_ref[...]).astype(BF16)

    o_ref[...] = jnp.dot(xn_ref[...], w_ref[...], preferred_element_type=F32)


def _inproj(x, g, w, *, tm, tn):
    m, d = x.shape
    n = w.shape[1]
    return pl.pallas_call(
        _inproj_kernel,
        grid=(m // tm, n // tn),
        in_specs=[pl.BlockSpec((tm, d), lambda i, j: (i, 0)),
                  pl.BlockSpec((1, d), lambda i, j: (0, 0)),
                  pl.BlockSpec((d, tn), lambda i, j: (0, j))],
        out_specs=pl.BlockSpec((tm, tn), lambda i, j: (i, j)),
        out_shape=jax.ShapeDtypeStruct((m, n), F32),
        scratch_shapes=[pltpu.VMEM((tm, d), BF16)],
        compiler_params=pltpu.CompilerParams(dimension_semantics=("parallel", "arbitrary"),
                                             vmem_limit_bytes=VMEM_LIMIT),
        name="inproj",
    )(x, g, w)


def _tile_sum(x):
    parts = [x[:, c * LANES:(c + 1) * LANES] for c in range(RWKV_WIDTH // LANES)]
    while len(parts) > 1:
        parts = [parts[i] + parts[i + 1] for i in range(0, len(parts), 2)]
    return parts[0]


def _tile8(x):
    return jnp.concatenate([x] * (RWKV_WIDTH // LANES), axis=1)


def _group_sum(x, m):
    hi = x.astype(BF16)
    rest = x - hi.astype(F32)
    mid = rest.astype(BF16)
    lo = (rest - mid.astype(F32)).astype(BF16)
    dot = lambda a: jnp.dot(a, m, preferred_element_type=F32)
    return dot(hi) + dot(mid) + dot(lo)


def _lora_dot(x, w):
    return jnp.dot(x.astype(BF16), w, preferred_element_type=F32)


KV_MU_R, KV_MU_K, KV_W0, KV_A0, KV_KK, KV_KA, KV_RK = range(7)
VV_MU_V, VV_LN_W, VV_LN_B = range(3)


def _prep_kernel(pr_ref, pk_ref, pv_ref, pl_ref, fr_ref, fk_ref, fv_ref, fl_ref,
                 kvec_ref, vvec_ref, mul_ref, wa_ref, g2_ref, sumh_ref,
                 a_out, w_out, b_out, k_out, r_out, v_out, g_out,
                 cr_ref, ck_ref, cv_ref, cl_ref, *, period):
    krow = lambda i: kvec_ref[i:i + 1, :]

    def shifted(p_ref, fix_ref, carry_ref, mu):
        p = p_ref[...]
        rolled = pltpu.roll(p, 1, 0)
        rows = lax.broadcasted_iota(jnp.int32, p.shape, 0)
        if period is None:
            @pl.when(pl.program_id(0) == 0)
            def _():
                carry_ref[...] = fix_ref[...]

            prev = jnp.where(rows == 0, carry_ref[...], rolled)
            carry_ref[...] = p[p.shape[0] - 1:, :]
        else:
            prev = jnp.where(rows % period == 0, fix_ref[...], rolled)
        return p + mu * (prev - p)

    r = shifted(pr_ref, fr_ref, cr_ref, krow(KV_MU_R))
    k = shifted(pk_ref, fk_ref, ck_ref, krow(KV_MU_K))
    v = shifted(pv_ref, fv_ref, cv_ref, vvec_ref[VV_MU_V:VV_MU_V + 1, :])
    xl = shifted(pl_ref, fl_ref, cl_ref, mul_ref[...])

    x_wa = xl[:, :LANES]
    is_w = lax.broadcasted_iota(jnp.int32, x_wa.shape, 1) < DECAY_LORA
    zw = krow(KV_W0) + _lora_dot(jnp.where(is_w, jnp.tanh(x_wa), 0.0), wa_ref[...])
    za = krow(KV_A0) + _lora_dot(jnp.where(is_w, 0.0, x_wa), wa_ref[...])
    g = _lora_dot(jax.nn.sigmoid(xl[:, LANES:]), g2_ref[...])
    y = -zw
    softplus = jnp.maximum(y, 0.0) + jnp.log(1.0 + jnp.exp(-jnp.abs(y)))
    decay = jnp.exp(-jnp.exp(-softplus - 0.5))
    a_sig = jax.nn.sigmoid(za)

    kk = k * krow(KV_KK)
    ssq = _tile8(_group_sum(_tile_sum(kk * kk), sumh_ref[...]))
    kk = kk / jnp.maximum(jnp.sqrt(ssq), 1e-12)

    a_out[...] = -kk
    w_out[...] = decay
    b_out[...] = kk * a_sig
    k_out[...] = k * (1.0 + (a_sig - 1.0) * krow(KV_KA))
    r_out[...] = r
    v_out[...] = v
    g_out[...] = g


def _prep(proj, fixes, params, *, tm, period):
    m = proj.shape[0]
    c = RWKV_WIDTH
    grid = (m // tm,)

    def col(width, idx):
        return pl.BlockSpec((tm, width), lambda i: (i, idx))

    def full(arr):
        return pl.BlockSpec(arr.shape, lambda i: (0,) * arr.ndim)

    if period is None:
        fix_specs = [full(f) for f in fixes]
    else:
        fix_specs = [pl.BlockSpec((tm, f.shape[1]), lambda i: (i, 0)) for f in fixes]
    out_spec = pl.BlockSpec((tm, c), lambda i: (i, 0))
    return pl.pallas_call(
        functools.partial(_prep_kernel, period=period),
        grid=grid,
        in_specs=[col(c, COL_R // c), col(c, COL_K // c), col(c, COL_V // c), col(LORA_PAD, COL_L // LORA_PAD)]
        + fix_specs + [full(p) for p in params],
        out_specs=[out_spec] * 7,
        out_shape=[jax.ShapeDtypeStruct((m, c), F32)] * 7,
        scratch_shapes=[pltpu.VMEM((1, c), F32)] * 3 + [pltpu.VMEM((1, LORA_PAD), F32)],
        compiler_params=pltpu.CompilerParams(dimension_semantics=("arbitrary",), vmem_limit_bytes=VMEM_LIMIT),
        name="rwkv_prep",
    )(proj, proj, proj, proj, *fixes, *params)


def _rwkv_out(o, r, k, v, g, kvec_ref, vvec_ref, gsum_ref, kv_ref):
    inv_n = 1.0 / HEAD_DIM
    mean = _tile8(_group_sum(_tile_sum(o), gsum_ref[...])) * inv_n
    d = o - mean
    var = _tile8(_group_sum(_tile_sum(d * d), gsum_ref[...])) * inv_n
    normed = (d * lax.rsqrt(var + GN_EPS) * vvec_ref[VV_LN_W:VV_LN_W + 1, :]
              + vvec_ref[VV_LN_B:VV_LN_B + 1, :])
    rk = _tile8(_group_sum(_tile_sum(r * k * kvec_ref[KV_RK:KV_RK + 1, :]), kv_ref[...]))
    return ((normed + rk * v) * g).astype(BF16)


ALIBI_SLOPES = [2.0 ** (-8.0 * (i + 1) / SWA_HEADS) for i in range(SWA_HEADS)]
SCORE_SCALE = HEAD_DIM ** -0.5


LOG2E = 1.4426950408889634


def _attn_block_kernel(sink_ref, q_ref, kc_ref, vc_ref, kp_ref, vp_ref, o_ref, ot_scr, bias_scr, *, sub_blocks):
    n = pl.program_id(0)
    span = WINDOW + BLOCK

    @pl.when(n == 0)
    def _():
        si = lax.broadcasted_iota(jnp.int32, (span, BLOCK), 0)
        qi = lax.broadcasted_iota(jnp.int32, (span, BLOCK), 1)
        dist = qi - si + WINDOW
        valid = (dist >= 0) & (dist < WINDOW)
        distf = dist.astype(F32)
        for h in range(SWA_HEADS):
            bias = jnp.where(valid, (-ALIBI_SLOPES[h] * LOG2E) * distf, -jnp.inf)
            bias_scr[1, h] = bias
            bias_scr[0, h] = jnp.where(si >= WINDOW, bias, -jnp.inf)

    for sb in range(sub_blocks):
        rows = slice(sb * BLOCK, (sb + 1) * BLOCK)
        if sb == 0:
            k_prev, v_prev = kp_ref[...], vp_ref[...]
            table = jnp.where(n == 0, 0, 1)
        else:
            prev_rows = slice((sb - 1) * BLOCK, sb * BLOCK)
            k_prev, v_prev = kc_ref[prev_rows, :], vc_ref[prev_rows, :]
            table = 1
        k_span = jnp.concatenate([k_prev, kc_ref[rows, :]], axis=0)
        v_t = jnp.concatenate([v_prev, vc_ref[rows, :]], axis=0).T.astype(BF16)
        for kh in range(SWA_KV_HEADS):
            ks = k_span[:, kh * HEAD_DIM:(kh + 1) * HEAD_DIM].astype(BF16)
            heads = range(kh * SWA_GROUP, (kh + 1) * SWA_GROUP)
            qg = jnp.concatenate([q_ref[rows, h * HEAD_DIM:(h + 1) * HEAD_DIM] for h in heads], axis=0).astype(BF16)
            s_all = lax.dot_general(ks, qg, (((1,), (1,)), ((), ())), preferred_element_type=F32)
            probs, dens = [], []
            for gi, h in enumerate(heads):
                s = s_all[:, gi * BLOCK:(gi + 1) * BLOCK] * (SCORE_SCALE * LOG2E) + bias_scr[table, h]
                sink = sink_ref[h] * LOG2E
                m = jnp.maximum(jnp.max(s, axis=0, keepdims=True), sink)
                e = jnp.exp2(s - m)
                dens.append(jnp.sum(e, axis=0, keepdims=True) + jnp.exp2(sink - m))
                probs.append(e.astype(BF16))
            o_t = jnp.dot(v_t[kh * HEAD_DIM:(kh + 1) * HEAD_DIM, :], jnp.concatenate(probs, axis=1),
                          preferred_element_type=F32) / jnp.concatenate(dens, axis=1)
            for gi, h in enumerate(heads):
                ot_scr[sb, h * HEAD_DIM:(h + 1) * HEAD_DIM, :] = o_t[:, gi * BLOCK:(gi + 1) * BLOCK]
        o_ref[rows, :] = ot_scr[sb].T.astype(BF16)


def _attn_prompt(proj, sinks, *, sub_blocks):
    t = proj.shape[0]
    tq = sub_blocks * BLOCK
    cur = lambda width, idx: pl.BlockSpec((tq, width), lambda n: (n, idx))
    prev = lambda width, idx: pl.BlockSpec((BLOCK, width), lambda n: (jnp.maximum(n * sub_blocks - 1, 0), idx))
    return pl.pallas_call(
        functools.partial(_attn_block_kernel, sub_blocks=sub_blocks),
        grid=(t // tq,),
        in_specs=[pl.BlockSpec(memory_space=pltpu.SMEM),
                  cur(SWA_WIDTH, COL_Q // SWA_WIDTH), cur(KV_WIDTH, COL_SK // KV_WIDTH), cur(KV_WIDTH, COL_SV // KV_WIDTH),
                  prev(KV_WIDTH, COL_SK // KV_WIDTH), prev(KV_WIDTH, COL_SV // KV_WIDTH)],
        out_specs=pl.BlockSpec((tq, SWA_WIDTH), lambda n: (n, 0)),
        out_shape=jax.ShapeDtypeStruct((t, SWA_WIDTH), BF16),
        scratch_shapes=[pltpu.VMEM((sub_blocks, SWA_WIDTH, BLOCK), F32),
                        pltpu.VMEM((2, SWA_HEADS, WINDOW + BLOCK, BLOCK), F32)],
        compiler_params=pltpu.CompilerParams(dimension_semantics=("arbitrary",), vmem_limit_bytes=VMEM_LIMIT),
        name="swa_prompt",
    )(sinks, proj, proj, proj, proj, proj)


def _attn_decode_kernel(sink_ref, q_ref, kn_ref, vn_ref, kp_ref, vp_ref, o_ref, ko_ref, vo_ref, *, t_new, bt):
    span = WINDOW + SUBLANES
    rows = SWA_GROUP * SUBLANES
    row = lax.broadcasted_iota(jnp.int32, (bt, rows, span), 1)
    si = lax.broadcasted_iota(jnp.int32, (bt, rows, span), 2)
    dist = row % SUBLANES - si + WINDOW
    valid = (dist >= 0) & (dist < WINDOW)
    distf = dist.astype(F32)
    grp = row // SUBLANES
    grp_col = grp[:, :, :1]

    def per_group(tile_grp, vals):
        out = vals[-1]
        for gi in range(SWA_GROUP - 2, -1, -1):
            out = jnp.where(tile_grp == gi, vals[gi], out)
        return out

    pad = jnp.zeros((bt, SUBLANES - t_new, HEAD_DIM), F32)
    for kh in range(SWA_KV_HEADS):
        lanes = slice(kh * HEAD_DIM, (kh + 1) * HEAD_DIM)
        heads = range(kh * SWA_GROUP, (kh + 1) * SWA_GROUP)
        ks = jnp.concatenate([kp_ref[:, :, lanes], kn_ref[:, :, lanes], pad], axis=1).astype(BF16)
        vs = jnp.concatenate([vp_ref[:, :, lanes], vn_ref[:, :, lanes], pad], axis=1).astype(BF16)
        qg = jnp.concatenate([piece for h in heads for piece in (q_ref[:, :, h * HEAD_DIM:(h + 1) * HEAD_DIM], pad)],
                             axis=1).astype(BF16)
        s = jnp.einsum("bqd,bsd->bqs", qg, ks, preferred_element_type=F32) * SCORE_SCALE
        slope = per_group(grp, [ALIBI_SLOPES[h] for h in heads])
        s = jnp.where(valid, s - slope * distf, -jnp.inf)
        sink = per_group(grp_col, [sink_ref[h] for h in heads])
        m = jnp.maximum(jnp.max(s, axis=-1, keepdims=True), sink)
        e = jnp.exp(s - m)
        den = jnp.sum(e, axis=-1, keepdims=True) + jnp.exp(sink - m)
        og = jnp.einsum("bqs,bsd->bqd", (e / den).astype(BF16), vs, preferred_element_type=F32)
        for gi, h in enumerate(heads):
            o_ref[:, :, h * HEAD_DIM:(h + 1) * HEAD_DIM] = og[:, gi * SUBLANES:gi * SUBLANES + t_new, :].astype(BF16)
    ko_ref[:, :WINDOW - t_new, :] = kp_ref[:, t_new:, :]
    ko_ref[:, WINDOW - t_new:, :] = kn_ref[...]
    vo_ref[:, :WINDOW - t_new, :] = vp_ref[:, t_new:, :]
    vo_ref[:, WINDOW - t_new:, :] = vn_ref[...]


def _attn_decode(proj3, k_past, v_past, sinks, *, bt):
    bsz, t_new = proj3.shape[0], proj3.shape[1]
    new = lambda width, idx: pl.BlockSpec((bt, t_new, width), lambda b: (b, 0, idx))
    past = pl.BlockSpec((bt, WINDOW, KV_WIDTH), lambda b: (b, 0, 0))
    return pl.pallas_call(
        functools.partial(_attn_decode_kernel, t_new=t_new, bt=bt),
        grid=(bsz // bt,),
        in_specs=[pl.BlockSpec(memory_space=pltpu.SMEM),
                  new(SWA_WIDTH, COL_Q // SWA_WIDTH), new(KV_WIDTH, COL_SK // KV_WIDTH), new(KV_WIDTH, COL_SV // KV_WIDTH),
                  past, past],
        out_specs=[pl.BlockSpec((bt, t_new, SWA_WIDTH), lambda b: (b, 0, 0)), past, past],
        out_shape=[jax.ShapeDtypeStruct((bsz, t_new, SWA_WIDTH), BF16),
                   jax.ShapeDtypeStruct(k_past.shape, F32), jax.ShapeDtypeStruct(v_past.shape, F32)],
        compiler_params=pltpu.CompilerParams(dimension_semantics=("parallel",), vmem_limit_bytes=VMEM_LIMIT),
        name="swa_decode",
    )(sinks, proj3, proj3, proj3, k_past, v_past)


def _outproj_kernel(x_ref, o_ref, r_ref, k_ref, v_ref, g_ref, osw_ref, kvec_ref, vvec_ref, gsum_ref, kv_ref,
                    wr_ref, ws_ref, gn_ref, x1_ref, h2_ref):
    o_rwkv = _rwkv_out(o_ref[...], r_ref[...], k_ref[...], v_ref[...], g_ref[...], kvec_ref, vvec_ref, gsum_ref, kv_ref)
    x1 = (x_ref[...] + jnp.dot(o_rwkv, wr_ref[...], preferred_element_type=F32)
          + jnp.dot(osw_ref[...], ws_ref[...], preferred_element_type=F32))
    x1_ref[...] = x1
    h2_ref[...] = _rms(x1, gn_ref[...]).astype(BF16)


def _outproj(x, o, r, k, v, g, o_swa, post_params, w_r, w_s, g_norm, *, tm):
    m, d = x.shape
    row = lambda width: pl.BlockSpec((tm, width), lambda i: (i, 0))
    full = lambda arr: pl.BlockSpec(arr.shape, lambda i: (0, 0))
    consts = (*post_params, w_r, w_s, g_norm)
    return pl.pallas_call(
        _outproj_kernel,
        grid=(m // tm,),
        in_specs=[row(d)] + [row(RWKV_WIDTH)] * 5 + [row(SWA_WIDTH)] + [full(c) for c in consts],
        out_specs=[row(d), row(d)],
        out_shape=[jax.ShapeDtypeStruct((m, d), F32), jax.ShapeDtypeStruct((m, d), BF16)],
        compiler_params=pltpu.CompilerParams(dimension_semantics=("parallel",), vmem_limit_bytes=VMEM_LIMIT),
        name="outproj",
    )(x, o, r, k, v, g, o_swa, *consts)


def _ffn_kernel(h_ref, x1_ref, wg_ref, wu_ref, wd_ref, gf_ref, y_ref, acc_ref):
    j = pl.program_id(1)

    @pl.when(j == 0)
    def _():
        acc_ref[...] = jnp.zeros_like(acc_ref)

    h = h_ref[...]
    gate = jnp.dot(h, wg_ref[...], preferred_element_type=F32)
    up = jnp.dot(h, wu_ref[...], preferred_element_type=F32)
    act = (gate * jax.nn.sigmoid(gate) * up).astype(BF16)
    acc_ref[...] += jnp.dot(act, wd_ref[...], preferred_element_type=F32)

    @pl.when(j == pl.num_programs(1) - 1)
    def _():
        y_ref[...] = _rms(x1_ref[...] + acc_ref[...], gf_ref[...])


def _ffn(h2, x1, w_gate, w_up, w_down, g_final, *, tm, tf):
    m, d = x1.shape
    f = w_gate.shape[1]
    return pl.pallas_call(
        _ffn_kernel,
        grid=(m // tm, f // tf),
        in_specs=[pl.BlockSpec((tm, d), lambda i, j: (i, 0)),
                  pl.BlockSpec((tm, d), lambda i, j: (i, 0)),
                  pl.BlockSpec((d, tf), lambda i, j: (0, j)),
                  pl.BlockSpec((d, tf), lambda i, j: (0, j)),
                  pl.BlockSpec((tf, d), lambda i, j: (j, 0)),
                  pl.BlockSpec((1, d), lambda i, j: (0, 0))],
        out_specs=pl.BlockSpec((tm, d), lambda i, j: (i, 0)),
        out_shape=jax.ShapeDtypeStruct((m, d), F32),
        scratch_shapes=[pltpu.VMEM((tm, d), F32)],
        compiler_params=pltpu.CompilerParams(dimension_semantics=("parallel", "arbitrary"),
                                             vmem_limit_bytes=VMEM_LIMIT),
        name="ffn",
    )(h2, x1, w_gate, w_up, w_down, g_final)


def _group_matrices():
    lane = np.arange(LANES)
    key_head = lane % RWKV_HEADS
    val_head = lane // I_LO
    sum_key = (key_head[:, None] == key_head[None, :]).astype(np.float32)
    sum_val = (val_head[:, None] == val_head[None, :]).astype(np.float32)
    key_to_val = (key_head[:, None] == val_head[None, :]).astype(np.float32)
    return sum_key, sum_val, key_to_val


def _state_to_kernel(s):
    b = s.shape[0]
    s = s.reshape(b, RWKV_HEADS, I_GROUPS, I_LO, J_GROUPS, SUBLANES)
    return s.transpose(0, 4, 2, 5, 1, 3).reshape(b, HEAD_DIM, SUBLANES, LANES)


def _state_from_kernel(s):
    b = s.shape[0]
    s = s.reshape(b, J_GROUPS, I_GROUPS, SUBLANES, RWKV_HEADS, I_LO)
    return s.transpose(0, 4, 2, 5, 1, 3).reshape(b, RWKV_HEADS, HEAD_DIM, HEAD_DIM)


def _shift_to_cols(shift):
    c = RWKV_WIDTH
    return (_key_order(shift[:, 0:c]), _key_order(shift[:, c:2 * c]), _val_order(shift[:, 2 * c:3 * c]),
            jnp.pad(shift[:, 3 * c:], ((0, 0), (0, LORA_PAD - LORA_COLS))))


def _shift_from_proj(rows):
    return jnp.concatenate([_key_natural(rows[:, COL_R:COL_K]), _key_natural(rows[:, COL_K:COL_V]),
                            _val_natural(rows[:, COL_V:COL_SK]), rows[:, COL_L:COL_L + LORA_COLS]], axis=1)


def _layer_weights(norm_attn, w_in, mu_shift, w0, w2, a0, a2, g2, k_k, k_a, r_k, ln_x_w, ln_x_b, sinks, w_out,
                   norm_ffn, w_gate, w_up, w_down, norm_final):
    c = RWKV_WIDTH
    swa0 = RWKV_COLS
    w_proj = jnp.concatenate([
        w_in[:, swa0:swa0 + SWA_WIDTH],
        _key_order(w_in[:, 0:c]), _key_order(w_in[:, c:2 * c]), _val_order(w_in[:, 2 * c:3 * c]),
        w_in[:, swa0 + SWA_WIDTH:swa0 + SWA_WIDTH + 2 * KV_WIDTH],
        jnp.pad(w_in[:, 3 * c:RWKV_COLS], ((0, 0), (0, LORA_PAD - LORA_COLS))),
    ], axis=1).astype(BF16)
    row = lambda v: v.reshape(1, -1)
    zero = jnp.zeros((c,), F32)
    kvec = _key_order(jnp.stack([mu_shift[0:c], mu_shift[c:2 * c], w0, a0, k_k, k_a, r_k.reshape(-1), zero]))
    vvec = _val_order(jnp.stack([mu_shift[2 * c:3 * c], ln_x_w, ln_x_b] + [zero] * 5))
    mu_l = jnp.pad(mu_shift[3 * c:], (0, LORA_PAD - LORA_COLS)).reshape(1, -1)
    lora_wa = _key_order(jnp.concatenate([w2, a2], axis=0)).astype(BF16)
    g2p = jnp.pad(_val_order(g2), ((0, LORA_PAD - LANES - GATE_LORA), (0, 0))).astype(BF16)
    sum_key, sum_val, key_to_val = _group_matrices()
    prep_params = (kvec, vvec, mu_l, lora_wa, g2p, jnp.asarray(sum_key, BF16))
    post_params = (kvec, vvec, jnp.asarray(sum_val, BF16), jnp.asarray(key_to_val, BF16))
    w_out_r = w_out[:c].reshape(RWKV_HEADS, I_GROUPS, I_LO, -1).swapaxes(0, 1).reshape(c, -1)
    return dict(
        norm_attn=row(norm_attn), w_proj=w_proj, prep=prep_params, post=post_params, sinks=sinks,
        w_out_r=w_out_r.astype(BF16), w_out_s=w_out[c:].astype(BF16), norm_ffn=row(norm_ffn),
        w_gate=w_gate.astype(BF16), w_up=w_up.astype(BF16), w_down=w_down.astype(BF16), norm_final=row(norm_final))


def _layer(x, shift_prev, state0, kv_past, lw, *, tiles):
    bsz, t_len, d = x.shape
    m = bsz * t_len
    x2 = x.reshape(m, d)
    proj = _inproj(x2, lw["norm_attn"], lw["w_proj"], tm=tiles["tm_in"], tn=tiles["tn_in"])

    fix_cols = _shift_to_cols(shift_prev)
    if bsz == 1:
        fixes, period = fix_cols, None
    else:
        fixes = tuple(jnp.pad(f[:, None, :], ((0, 0), (0, t_len - 1), (0, 0))).reshape(m, -1) for f in fix_cols)
        period = t_len
    a, w, b, k, r, v, g = _prep(proj, fixes, lw["prep"], tm=tiles["tm_prep"], period=period)

    tok = lambda z: z.reshape(bsz, t_len, SUBLANES, LANES)
    o, s_fin = _rwkv_scan(tok(a), tok(w), tok(b), tok(k), tok(r), tok(v), _state_to_kernel(state0), tb=tiles["tb"],
                          sub=min(SCAN_SUB, t_len), bb=min(tiles["bb_scan"], bsz))

    kv_shape = (bsz, WINDOW, SWA_KV_HEADS, HEAD_DIM)
    if kv_past is None:
        o_swa = _attn_prompt(proj, lw["sinks"], sub_blocks=tiles["attn_sub"])
        new_k = proj[m - WINDOW:, COL_SK:COL_SV].reshape(kv_shape)
        new_v = proj[m - WINDOW:, COL_SV:COL_L].reshape(kv_shape)
    else:
        k_past, v_past = (z.reshape(bsz, WINDOW, KV_WIDTH) for z in kv_past)
        o_swa, new_k, new_v = _attn_decode(proj.reshape(bsz, t_len, PROJ_COLS), k_past, v_past, lw["sinks"],
                                           bt=min(tiles["bt_attn"], bsz))
        o_swa, new_k, new_v = o_swa.reshape(m, SWA_WIDTH), new_k.reshape(kv_shape), new_v.reshape(kv_shape)

    x1, h2 = _outproj(x2, o.reshape(m, RWKV_WIDTH), r, k, v, g, o_swa, lw["post"], lw["w_out_r"], lw["w_out_s"],
                      lw["norm_ffn"], tm=tiles["tm_out"])
    y = _ffn(h2, x1, lw["w_gate"], lw["w_up"], lw["w_down"], lw["norm_final"], tm=tiles["tm_ffn"], tf=tiles["tf"])

    new_shift = _shift_from_proj(proj.reshape(bsz, t_len, PROJ_COLS)[:, -1])
    return y.reshape(bsz, t_len, d), _state_from_kernel(s_fin), new_shift, new_k, new_v


def _tiles(m, t_len):
    return dict(tm_in=min(1024, m), tn_in=1664, tm_prep=min(512, m), tb=min(256, t_len),
                tm_out=min(256, m), tm_ffn=min(512, m), tf=512, bt_attn=16, attn_sub=4, bb_scan=4)


def kernel(x_prompt, x_sample, state_rwkv, state_shift, cache_swa_k, cache_swa_v, norm_attn, w_in, mu_shift, w0, w2, a0, a2, g2, k_k, k_a, r_k, ln_x_w, ln_x_b, sinks, w_out, norm_ffn, w_gate, w_up, w_down, norm_final):
    assert norm_attn.shape[0] == 1, "single trunk layer"
    lw = _layer_weights(norm_attn[0], w_in[0], mu_shift[0], w0[0], w2[0], a0[0], a2[0], g2[0], k_k[0], k_a[0],
                        r_k[0], ln_x_w[0], ln_x_b[0], sinks[0], w_out[0], norm_ffn[0], w_gate[0], w_up[0],
                        w_down[0], norm_final)
    bp, tp, _ = x_prompt.shape
    bs, ts, _ = x_sample.shape
    assert bp == 1 and tp % BLOCK == 0 and ts < SUBLANES
    yp, p_state, p_shift, p_k, p_v = _layer(
        x_prompt, jnp.zeros((bp, RWKV_COLS), F32), jnp.zeros((bp, RWKV_HEADS, HEAD_DIM, HEAD_DIM), F32),
        None, lw, tiles=_tiles(bp * tp, tp))
    ys, s_state, s_shift, s_k, s_v = _layer(
        x_sample, state_shift[0], state_rwkv[0], (cache_swa_k[0], cache_swa_v[0]), lw, tiles=_tiles(bs * ts, ts))
    return (yp, ys, p_state[None], p_shift[None], p_k[None], p_v[None],
            s_state[None], s_shift[None], s_k[None], s_v[None])
```

```python
import functools

import numpy as np
import jax
import jax.numpy as jnp
from jax import lax
from jax.experimental import pallas as pl
from jax.experimental.pallas import tpu as pltpu

F32 = jnp.float32
BF16 = jnp.bfloat16

D_MODEL = 2048
HEAD_DIM = 64
RWKV_WIDTH = 1024
RWKV_HEADS = 16
SWA_WIDTH = 1024
SWA_HEADS = 16
SWA_KV_HEADS = 4
SWA_GROUP = 4
KV_WIDTH = 256
WINDOW = 128
BLOCK = 128
DECAY_LORA = 64
AAA_LORA = 64
GATE_LORA = 160
LORA_COLS = DECAY_LORA + AAA_LORA + GATE_LORA
RWKV_COLS = 3 * RWKV_WIDTH + LORA_COLS
D_FF = 5632
RMS_EPS = 1e-5
GN_EPS = 64e-5

SUBLANES = 8
LANES = 128
I_LO = LANES // RWKV_HEADS


def _reorder_last(x, split, a, b):
    return x.reshape(x.shape[:-1] + split).swapaxes(a, b).reshape(x.shape)


def _key_order(x):
    return _reorder_last(x, (RWKV_HEADS, HEAD_DIM), -1, -2)


def _key_natural(x):
    return _reorder_last(x, (HEAD_DIM, RWKV_HEADS), -1, -2)


def _val_order(x):
    return _reorder_last(x, (RWKV_HEADS, HEAD_DIM // I_LO, I_LO), -3, -2)


def _val_natural(x):
    return _reorder_last(x, (HEAD_DIM // I_LO, RWKV_HEADS, I_LO), -3, -2)


J_GROUPS = HEAD_DIM // SUBLANES
I_GROUPS = HEAD_DIM // I_LO
SCAN_SUB = 64


def _sublane_allsum(x):
    for shift in (4, 2, 1):
        x = x + pltpu.roll(x, shift, 0)
    return x


def _sublane_sums(xs):
    sub = lax.broadcasted_iota(jnp.int32, (SUBLANES, LANES), 0)

    def combine(x, y, d):
        clear = (sub & d) == 0
        return jnp.where(clear, x, pltpu.roll(y, d, 0)) + jnp.where(clear, pltpu.roll(x, SUBLANES - d, 0), y)

    z = [combine(xs[0], xs[4], 4), combine(xs[2], xs[6], 4), combine(xs[1], xs[5], 4), combine(xs[3], xs[7], 4)]
    return combine(combine(z[0], z[1], 2), combine(z[2], z[3], 2), 1)


def _cumprod_leading(x):
    n, shift = x.shape[0], 1
    while shift < n:
        x = jnp.concatenate([x[:shift], x[shift:] * x[:-shift]], axis=0)
        shift *= 2
    return x


def _scan_kernel(a_ref, w_ref, b_ref, k_ref, r_ref, v_ref, s0_ref, o_ref, sT_ref, s_scr, sa_scr, src_scr, pend_scr,
                 *tiles, tb, sub, bb):
    ti = pl.program_id(1)
    n_sub = tb // sub

    @pl.when(ti == 0)
    def _():
        s_scr[...] = s0_ref[...]

    gather_idx = (lax.broadcasted_iota(jnp.int32, (SUBLANES, LANES), 0) * RWKV_HEADS
                  + lax.broadcasted_iota(jnp.int32, (SUBLANES, LANES), 1) // I_LO)

    def gather(src_row):
        return jnp.take_along_axis(jnp.broadcast_to(src_row, (SUBLANES, LANES)), gather_idx, axis=1)

    A, B, K, R = range(4)
    tile = lambda bi, qi, parity: tiles[2 * qi + parity].at[bi]

    def prepare(bi):
        p_end = None
        for n in range(n_sub):
            rows = slice(n * sub, (n + 1) * sub)
            p = _cumprod_leading(w_ref[bi, rows])
            p_inv = 1.0 / p
            first = jnp.ones_like(p[:1]) if p_end is None else p_end[None]
            p_prev = jnp.concatenate([first, p[:-1]], axis=0)
            src_scr[bi, A, rows] = a_ref[bi, rows] * p_prev
            src_scr[bi, B, rows] = b_ref[bi, rows] * p_inv
            src_scr[bi, K, rows] = k_ref[bi, rows] * p_inv
            src_scr[bi, R, rows] = r_ref[bi, rows] * p
            p_end = p[sub - 1]
            for jg in range(J_GROUPS):
                pend_scr[bi, n, jg] = gather(p_end[jg:jg + 1, :])

    def build(bi, qi, t, parity):
        src_t = jnp.minimum(t, tb - 1)
        for jg in range(J_GROUPS):
            tile(bi, qi, parity)[jg] = gather(src_scr[bi, qi, src_t, pl.ds(jg, 1), :])

    def first_sa(bi):
        for qi in range(4):
            build(bi, qi, 0, 0)
        build(bi, A, 1, 1)
        sa0 = [None] * I_GROUPS
        for jg in range(J_GROUPS):
            a0 = tile(bi, A, 0)[jg]
            for ig in range(I_GROUPS):
                term = s_scr[bi, jg * I_GROUPS + ig] * a0
                sa0[ig] = term if sa0[ig] is None else sa0[ig] + term
        return [_sublane_allsum(x) for x in sa0]

    def step(bi, t, parity, sa, rescale=None):
        for qi in (B, K, R):
            build(bi, qi, t + 1, 1 - parity)
        build(bi, A, t + 2, parity)
        vb = [jnp.broadcast_to(v_ref[bi, t, pl.ds(ig, 1), :], (SUBLANES, LANES)) for ig in range(I_GROUPS)]
        out = [None] * I_GROUPS
        nxt = [None] * I_GROUPS
        for jg in range(J_GROUPS):
            b, k, r = (tile(bi, qi, parity)[jg] for qi in (B, K, R))
            a_next = tile(bi, A, 1 - parity)[jg]
            for ig in range(I_GROUPS):
                idx = jg * I_GROUPS + ig
                s_old = s_scr[bi, idx]
                if rescale is not None:
                    s_old = s_old * pend_scr[bi, rescale, jg]
                sj = s_old + sa[ig] * b + vb[ig] * k
                s_scr[bi, idx] = sj
                to, tn = sj * r, sj * a_next
                out[ig] = to if out[ig] is None else out[ig] + to
                nxt[ig] = tn if nxt[ig] is None else nxt[ig] + tn
        o_ref[bi, t] = _sublane_sums(out)
        sa_scr[bi, parity] = _sublane_sums(nxt)
        return [jnp.broadcast_to(sa_scr[bi, parity, pl.ds(ig, 1), :], (SUBLANES, LANES)) for ig in range(I_GROUPS)]

    def finish(bi):
        for jg in range(J_GROUPS):
            for ig in range(I_GROUPS):
                idx = jg * I_GROUPS + ig
                s_scr[bi, idx] = s_scr[bi, idx] * pend_scr[bi, n_sub - 1, jg]

    for bi in range(bb):
        prepare(bi)
    if tb <= SUBLANES:
        sas = [first_sa(bi) for bi in range(bb)]
        for t in range(tb):
            sas = [step(bi, t, t % 2, sas[bi]) for bi in range(bb)]
    else:
        for bi in range(bb):
            sa = first_sa(bi)
            for n in range(n_sub):
                t0 = n * sub
                sa = step(bi, t0, 0, sa, rescale=n - 1 if n else None)
                sa = step(bi, t0 + 1, 1, sa)

                def two_steps(u, sa, bi=bi):
                    return step(bi, 2 * u + 1, 1, step(bi, 2 * u, 0, sa))

                sa = lax.fori_loop(t0 // 2 + 1, (t0 + sub) // 2, two_steps, sa)
    for bi in range(bb):
        finish(bi)

    @pl.when(ti == pl.num_programs(1) - 1)
    def _():
        sT_ref[...] = s_scr[...]


def _rwkv_scan(a, w, b, k, r, v, s0, *, tb, sub, bb):
    bsz, t_len = a.shape[0], a.shape[1]
    assert t_len % tb == 0 and tb % sub == 0 and sub % 2 == 0 and bsz % bb == 0
    tok_spec = pl.BlockSpec((bb, tb, SUBLANES, LANES), lambda bi, ti: (bi, ti, 0, 0))
    st_spec = pl.BlockSpec((bb, HEAD_DIM, SUBLANES, LANES), lambda bi, ti: (bi, 0, 0, 0))
    return pl.pallas_call(
        functools.partial(_scan_kernel, tb=tb, sub=sub, bb=bb),
        grid=(bsz // bb, t_len // tb),
        in_specs=[tok_spec] * 6 + [st_spec],
        out_specs=[tok_spec, st_spec],
        out_shape=[jax.ShapeDtypeStruct(a.shape, F32), jax.ShapeDtypeStruct(s0.shape, F32)],
        scratch_shapes=([pltpu.VMEM((bb, HEAD_DIM, SUBLANES, LANES), F32), pltpu.VMEM((bb, 2, SUBLANES, LANES), F32),
                         pltpu.VMEM((bb, 4, tb, SUBLANES, LANES), F32),
                         pltpu.VMEM((bb, tb // sub, J_GROUPS, SUBLANES, LANES), F32)]
                        + [pltpu.VMEM((bb, J_GROUPS, SUBLANES, LANES), F32)] * 8),
        compiler_params=pltpu.CompilerParams(dimension_semantics=("parallel", "arbitrary"),
                                             vmem_limit_bytes=VMEM_LIMIT),
        name="rwkv_scan",
    )(a, w, b, k, r, v, s0)


PROJ_COLS = 4992
LORA_PAD = 384
COL_Q, COL_R, COL_K, COL_V, COL_SK, COL_SV, COL_L = 0, 1024, 2048, 3072, 4096, 4352, 4608
VMEM_LIMIT = 56 * 1024 * 1024
FFN_VMEM_LIMIT = 60000 * 1024


def _rms(x, g):
    return x * lax.rsqrt(jnp.mean(x * x, axis=-1, keepdims=True) + RMS_EPS) * g


def _inproj_kernel(x_ref, g_ref, w_ref, o_ref, xn_ref):
    @pl.when(pl.program_id(1) == 0)
    def _():
        xn_ref[...] = _rms(x_ref[...], g_ref[...]).astype(BF16)

    o_ref[...] = jnp.dot(xn_ref[...], w_ref[...], preferred_element_type=F32)


def _inproj(x, g, w, *, tm, tn):
    m, d = x.shape
    n = w.shape[1]
    return pl.pallas_call(
        _inproj_kernel,
        grid=(m // tm, n // tn),
        in_specs=[pl.BlockSpec((tm, d), lambda i, j: (i, 0)),
                  pl.BlockSpec((1, d), lambda i, j: (0, 0)),
                  pl.BlockSpec((d, tn), lambda i, j: (0, j))],
        out_specs=pl.BlockSpec((tm, tn), lambda i, j: (i, j)),
        out_shape=jax.ShapeDtypeStruct((m, n), F32),
        scratch_shapes=[pltpu.VMEM((tm, d), BF16)],
        compiler_params=pltpu.CompilerParams(dimension_semantics=("parallel", "arbitrary"),
                                             vmem_limit_bytes=VMEM_LIMIT),
        name="inproj",
    )(x, g, w)


def _tile_sum(x):
    parts = [x[:, c * LANES:(c + 1) * LANES] for c in range(RWKV_WIDTH // LANES)]
    while len(parts) > 1:
        parts = [parts[i] + parts[i + 1] for i in range(0, len(parts), 2)]
    return parts[0]


def _tile8(x):
    return jnp.concatenate([x] * (RWKV_WIDTH // LANES), axis=1)


def _group_sum(x, m):
    hi = x.astype(BF16)
    rest = x - hi.astype(F32)
    mid = rest.astype(BF16)
    lo = (rest - mid.astype(F32)).astype(BF16)
    dot = lambda a: jnp.dot(a, m, preferred_element_type=F32)
    return dot(hi) + dot(mid) + dot(lo)


def _lora_dot(x, w):
    return jnp.dot(x.astype(BF16), w, preferred_element_type=F32)


KV_MU_R, KV_MU_K, KV_W0, KV_A0, KV_KK, KV_KA, KV_RK = range(7)
VV_MU_V, VV_LN_W, VV_LN_B = range(3)


def _prep_kernel(pr_ref, pk_ref, pv_ref, pl_ref, fr_ref, fk_ref, fv_ref, fl_ref,
                 kvec_ref, vvec_ref, mul_ref, wa_ref, g2_ref, sumh_ref,
                 a_out, w_out, b_out, k_out, r_out, v_out, g_out,
                 cr_ref, ck_ref, cv_ref, cl_ref, *, period):
    krow = lambda i: kvec_ref[i:i + 1, :]

    def shifted(p_ref, fix_ref, carry_ref, mu):
        p = p_ref[...]
        rolled = pltpu.roll(p, 1, 0)
        rows = lax.broadcasted_iota(jnp.int32, p.shape, 0)
        if period is None:
            @pl.when(pl.program_id(0) == 0)
            def _():
                carry_ref[...] = fix_ref[...]

            prev = jnp.where(rows == 0, carry_ref[...], rolled)
            carry_ref[...] = p[p.shape[0] - 1:, :]
        else:
            prev = jnp.where(rows % period == 0, fix_ref[...], rolled)
        return p + mu * (prev - p)

    r = shifted(pr_ref, fr_ref, cr_ref, krow(KV_MU_R))
    k = shifted(pk_ref, fk_ref, ck_ref, krow(KV_MU_K))
    v = shifted(pv_ref, fv_ref, cv_ref, vvec_ref[VV_MU_V:VV_MU_V + 1, :])
    xl = shifted(pl_ref, fl_ref, cl_ref, mul_ref[...])

    x_wa = xl[:, :LANES]
    is_w = lax.broadcasted_iota(jnp.int32, x_wa.shape, 1) < DECAY_LORA
    zw = krow(KV_W0) + _lora_dot(jnp.where(is_w, jnp.tanh(x_wa), 0.0), wa_ref[...])
    za = krow(KV_A0) + _lora_dot(jnp.where(is_w, 0.0, x_wa), wa_ref[...])
    g = _lora_dot(jax.nn.sigmoid(xl[:, LANES:]), g2_ref[...])
    y = -zw
    softplus = jnp.maximum(y, 0.0) + jnp.log(1.0 + jnp.exp(-jnp.abs(y)))
    decay = jnp.exp(-jnp.exp(-softplus - 0.5))
    a_sig = jax.nn.sigmoid(za)

    kk = k * krow(KV_KK)
    ssq = _tile8(_group_sum(_tile_sum(kk * kk), sumh_ref[...]))
    kk = kk / jnp.maximum(jnp.sqrt(ssq), 1e-12)

    a_out[...] = -kk
    w_out[...] = decay
    b_out[...] = kk * a_sig
    k_out[...] = k * (1.0 + (a_sig - 1.0) * krow(KV_KA))
    r_out[...] = r
    v_out[...] = v
    g_out[...] = g


def _prep(proj, fixes, params, *, tm, period):
    m = proj.shape[0]
    c = RWKV_WIDTH
    grid = (m // tm,)

    def col(width, idx):
        return pl.BlockSpec((tm, width), lambda i: (i, idx))

    def full(arr):
        return pl.BlockSpec(arr.shape, lambda i: (0,) * arr.ndim)

    if period is None:
        fix_specs = [full(f) for f in fixes]
    else:
        fix_specs = [pl.BlockSpec((tm, f.shape[1]), lambda i: (i, 0)) for f in fixes]
    out_spec = pl.BlockSpec((tm, c), lambda i: (i, 0))
    return pl.pallas_call(
        functools.partial(_prep_kernel, period=period),
        grid=grid,
        in_specs=[col(c, COL_R // c), col(c, COL_K // c), col(c, COL_V // c), col(LORA_PAD, COL_L // LORA_PAD)]
        + fix_specs + [full(p) for p in params],
        out_specs=[out_spec] * 7,
        out_shape=[jax.ShapeDtypeStruct((m, c), F32)] * 7,
        scratch_shapes=[pltpu.VMEM((1, c), F32)] * 3 + [pltpu.VMEM((1, LORA_PAD), F32)],
        compiler_params=pltpu.CompilerParams(dimension_semantics=("arbitrary",), vmem_limit_bytes=VMEM_LIMIT),
        name="rwkv_prep",
    )(proj, proj, proj, proj, *fixes, *params)


def _rwkv_out(o, r, k, v, g, kvec_ref, vvec_ref, gsum_ref, kv_ref):
    inv_n = 1.0 / HEAD_DIM
    mean = _tile8(_group_sum(_tile_sum(o), gsum_ref[...])) * inv_n
    d = o - mean
    var = _tile8(_group_sum(_tile_sum(d * d), gsum_ref[...])) * inv_n
    normed = (d * lax.rsqrt(var + GN_EPS) * vvec_ref[VV_LN_W:VV_LN_W + 1, :]
              + vvec_ref[VV_LN_B:VV_LN_B + 1, :])
    rk = _tile8(_group_sum(_tile_sum(r * k * kvec_ref[KV_RK:KV_RK + 1, :]), kv_ref[...]))
    return ((normed + rk * v) * g).astype(BF16)


ALIBI_SLOPES = [2.0 ** (-8.0 * (i + 1) / SWA_HEADS) for i in range(SWA_HEADS)]
SCORE_SCALE = HEAD_DIM ** -0.5


LOG2E = 1.4426950408889634


def _attn_block_kernel(sink_ref, q_ref, kc_ref, vc_ref, kp_ref, vp_ref, o_ref, ot_scr, bias_scr, *, sub_blocks):
    n = pl.program_id(0)
    span = WINDOW + BLOCK

    @pl.when(n == 0)
    def _():
        si = lax.broadcasted_iota(jnp.int32, (span, BLOCK), 0)
        qi = lax.broadcasted_iota(jnp.int32, (span, BLOCK), 1)
        dist = qi - si + WINDOW
        valid = (dist >= 0) & (dist < WINDOW)
        distf = dist.astype(F32)
        for h in range(SWA_HEADS):
            bias = jnp.where(valid, (-ALIBI_SLOPES[h] * LOG2E) * distf, -jnp.inf)
            bias_scr[1, h] = bias
            bias_scr[0, h] = jnp.where(si >= WINDOW, bias, -jnp.inf)

    for sb in range(sub_blocks):
        rows = slice(sb * BLOCK, (sb + 1) * BLOCK)
        if sb == 0:
            k_prev, v_prev = kp_ref[...], vp_ref[...]
            table = jnp.where(n == 0, 0, 1)
        else:
            prev_rows = slice((sb - 1) * BLOCK, sb * BLOCK)
            k_prev, v_prev = kc_ref[prev_rows, :], vc_ref[prev_rows, :]
            table = 1
        k_span = jnp.concatenate([k_prev, kc_ref[rows, :]], axis=0)
        v_t = jnp.concatenate([v_prev, vc_ref[rows, :]], axis=0).T.astype(BF16)
        for kh in range(SWA_KV_HEADS):
            ks = k_span[:, kh * HEAD_DIM:(kh + 1) * HEAD_DIM].astype(BF16)
            heads = range(kh * SWA_GROUP, (kh + 1) * SWA_GROUP)
            qg = jnp.concatenate([q_ref[rows, h * HEAD_DIM:(h + 1) * HEAD_DIM] for h in heads], axis=0).astype(BF16)
            s_all = lax.dot_general(ks, qg, (((1,), (1,)), ((), ())), preferred_element_type=F32)
            probs, dens = [], []
            for gi, h in enumerate(heads):
                s = s_all[:, gi * BLOCK:(gi + 1) * BLOCK] * (SCORE_SCALE * LOG2E) + bias_scr[table, h]
                sink = sink_ref[h] * LOG2E
                m = jnp.maximum(jnp.max(s, axis=0, keepdims=True), sink)
                e = jnp.exp2(s - m)
                dens.append(jnp.sum(e, axis=0, keepdims=True) + jnp.exp2(sink - m))
                probs.append(e.astype(BF16))
            o_t = jnp.dot(v_t[kh * HEAD_DIM:(kh + 1) * HEAD_DIM, :], jnp.concatenate(probs, axis=1),
                          preferred_element_type=F32) / jnp.concatenate(dens, axis=1)
            for gi, h in enumerate(heads):
                ot_scr[sb, h * HEAD_DIM:(h + 1) * HEAD_DIM, :] = o_t[:, gi * BLOCK:(gi + 1) * BLOCK]
        o_ref[rows, :] = ot_scr[sb].T.astype(BF16)


def _attn_prompt(proj, sinks, *, sub_blocks):
    t = proj.shape[0]
    tq = sub_blocks * BLOCK
    cur = lambda width, idx: pl.BlockSpec((tq, width), lambda n: (n, idx))
    prev = lambda width, idx: pl.BlockSpec((BLOCK, width), lambda n: (jnp.maximum(n * sub_blocks - 1, 0), idx))
    return pl.pallas_call(
        functools.partial(_attn_block_kernel, sub_blocks=sub_blocks),
        grid=(t // tq,),
        in_specs=[pl.BlockSpec(memory_space=pltpu.SMEM),
                  cur(SWA_WIDTH, COL_Q // SWA_WIDTH), cur(KV_WIDTH, COL_SK // KV_WIDTH), cur(KV_WIDTH, COL_SV // KV_WIDTH),
                  prev(KV_WIDTH, COL_SK // KV_WIDTH), prev(KV_WIDTH, COL_SV // KV_WIDTH)],
        out_specs=pl.BlockSpec((tq, SWA_WIDTH), lambda n: (n, 0)),
        out_shape=jax.ShapeDtypeStruct((t, SWA_WIDTH), BF16),
        scratch_shapes=[pltpu.VMEM((sub_blocks, SWA_WIDTH, BLOCK), F32),
                        pltpu.VMEM((2, SWA_HEADS, WINDOW + BLOCK, BLOCK), F32)],
        compiler_params=pltpu.CompilerParams(dimension_semantics=("arbitrary",), vmem_limit_bytes=VMEM_LIMIT),
        name="swa_prompt",
    )(sinks, proj, proj, proj, proj, proj)


def _attn_decode_kernel(sink_ref, q_ref, kn_ref, vn_ref, kp_ref, vp_ref, o_ref, ko_ref, vo_ref, *, t_new, bt):
    span = WINDOW + SUBLANES
    rows = SWA_GROUP * SUBLANES
    row = lax.broadcasted_iota(jnp.int32, (bt, rows, span), 1)
    si = lax.broadcasted_iota(jnp.int32, (bt, rows, span), 2)
    dist = row % SUBLANES - si + WINDOW
    valid = (dist >= 0) & (dist < WINDOW)
    distf = dist.astype(F32)
    grp = row // SUBLANES
    grp_col = grp[:, :, :1]

    def per_group(tile_grp, vals):
        out = vals[-1]
        for gi in range(SWA_GROUP - 2, -1, -1):
            out = jnp.where(tile_grp == gi, vals[gi], out)
        return out

    pad = jnp.zeros((bt, SUBLANES - t_new, HEAD_DIM), F32)
    for kh in range(SWA_KV_HEADS):
        lanes = slice(kh * HEAD_DIM, (kh + 1) * HEAD_DIM)
        heads = range(kh * SWA_GROUP, (kh + 1) * SWA_GROUP)
        ks = jnp.concatenate([kp_ref[:, :, lanes], kn_ref[:, :, lanes], pad], axis=1).astype(BF16)
        vs = jnp.concatenate([vp_ref[:, :, lanes], vn_ref[:, :, lanes], pad], axis=1).astype(BF16)
        qg = jnp.concatenate([piece for h in heads for piece in (q_ref[:, :, h * HEAD_DIM:(h + 1) * HEAD_DIM], pad)],
                             axis=1).astype(BF16)
        s = jnp.einsum("bqd,bsd->bqs", qg, ks, preferred_element_type=F32) * SCORE_SCALE
        slope = per_group(grp, [ALIBI_SLOPES[h] for h in heads])
        s = jnp.where(valid, s - slope * distf, -jnp.inf)
        sink = per_group(grp_col, [sink_ref[h] for h in heads])
        m = jnp.maximum(jnp.max(s, axis=-1, keepdims=True), sink)
        e = jnp.exp(s - m)
        den = jnp.sum(e, axis=-1, keepdims=True) + jnp.exp(sink - m)
        og = jnp.einsum("bqs,bsd->bqd", (e / den).astype(BF16), vs, preferred_element_type=F32)
        for gi, h in enumerate(heads):
            o_ref[:, :, h * HEAD_DIM:(h + 1) * HEAD_DIM] = og[:, gi * SUBLANES:gi * SUBLANES + t_new, :].astype(BF16)
    ko_ref[:, :WINDOW - t_new, :] = kp_ref[:, t_new:, :]
    ko_ref[:, WINDOW - t_new:, :] = kn_ref[...]
    vo_ref[:, :WINDOW - t_new, :] = vp_ref[:, t_new:, :]
    vo_ref[:, WINDOW - t_new:, :] = vn_ref[...]


def _attn_decode(proj3, k_past, v_past, sinks, *, bt):
    bsz, t_new = proj3.shape[0], proj3.shape[1]
    new = lambda width, idx: pl.BlockSpec((bt, t_new, width), lambda b: (b, 0, idx))
    past = pl.BlockSpec((bt, WINDOW, KV_WIDTH), lambda b: (b, 0, 0))
    return pl.pallas_call(
        functools.partial(_attn_decode_kernel, t_new=t_new, bt=bt),
        grid=(bsz // bt,),
        in_specs=[pl.BlockSpec(memory_space=pltpu.SMEM),
                  new(SWA_WIDTH, COL_Q // SWA_WIDTH), new(KV_WIDTH, COL_SK // KV_WIDTH), new(KV_WIDTH, COL_SV // KV_WIDTH),
                  past, past],
        out_specs=[pl.BlockSpec((bt, t_new, SWA_WIDTH), lambda b: (b, 0, 0)), past, past],
        out_shape=[jax.ShapeDtypeStruct((bsz, t_new, SWA_WIDTH), BF16),
                   jax.ShapeDtypeStruct(k_past.shape, F32), jax.ShapeDtypeStruct(v_past.shape, F32)],
        compiler_params=pltpu.CompilerParams(dimension_semantics=("parallel",), vmem_limit_bytes=VMEM_LIMIT),
        name="swa_decode",
    )(sinks, proj3, proj3, proj3, k_past, v_past)


def _outproj_kernel(x_ref, o_ref, r_ref, k_ref, v_ref, g_ref, osw_ref, kvec_ref, vvec_ref, gsum_ref, kv_ref,
                    wr_ref, ws_ref, gn_ref, x1_ref, h2_ref):
    o_rwkv = _rwkv_out(o_ref[...], r_ref[...], k_ref[...], v_ref[...], g_ref[...], kvec_ref, vvec_ref, gsum_ref, kv_ref)
    x1 = (x_ref[...] + jnp.dot(o_rwkv, wr_ref[...], preferred_element_type=F32)
          + jnp.dot(osw_ref[...], ws_ref[...], preferred_element_type=F32))
    x1_ref[...] = x1
    h2_ref[...] = _rms(x1, gn_ref[...]).astype(BF16)


def _outproj(x, o, r, k, v, g, o_swa, post_params, w_r, w_s, g_norm, *, tm):
    m, d = x.shape
    row = lambda width: pl.BlockSpec((tm, width), lambda i: (i, 0))
    full = lambda arr: pl.BlockSpec(arr.shape, lambda i: (0, 0))
    consts = (*post_params, w_r, w_s, g_norm)
    return pl.pallas_call(
        _outproj_kernel,
        grid=(m // tm,),
        in_specs=[row(d)] + [row(RWKV_WIDTH)] * 5 + [row(SWA_WIDTH)] + [full(c) for c in consts],
        out_specs=[row(d), row(d)],
        out_shape=[jax.ShapeDtypeStruct((m, d), F32), jax.ShapeDtypeStruct((m, d), BF16)],
        compiler_params=pltpu.CompilerParams(dimension_semantics=("parallel",), vmem_limit_bytes=VMEM_LIMIT),
        name="outproj",
    )(x, o, r, k, v, g, o_swa, *consts)


def _ffn_kernel(h_ref, x1_ref, wg_ref, wu_ref, wd_ref, gf_ref, y_ref, acc_ref):
    j = pl.program_id(1)

    @pl.when(j == 0)
    def _():
        acc_ref[...] = jnp.zeros_like(acc_ref)

    h = h_ref[...]
    gate = jnp.dot(h, wg_ref[...], preferred_element_type=F32)
    up = jnp.dot(h, wu_ref[...], preferred_element_type=F32)
    act = (gate * jax.nn.sigmoid(gate) * up).astype(BF16)
    acc_ref[...] += jnp.dot(act, wd_ref[...], preferred_element_type=F32)

    @pl.when(j == pl.num_programs(1) - 1)
    def _():
        y_ref[...] = _rms(x1_ref[...] + acc_ref[...], gf_ref[...])


def _ffn(h2, x1, w_gate, w_up, w_down, g_final, *, tm, tf):
    m, d = x1.shape
    f = w_gate.shape[1]
    once = dict(pipeline_mode=pl.Buffered(1))
    return pl.pallas_call(
        _ffn_kernel,
        grid=(m // tm, f // tf),
        in_specs=[pl.BlockSpec((tm, d), lambda i, j: (i, 0)),
                  pl.BlockSpec((tm, d), lambda i, j: (i, 0), **once),
                  pl.BlockSpec((d, tf), lambda i, j: (0, j)),
                  pl.BlockSpec((d, tf), lambda i, j: (0, j)),
                  pl.BlockSpec((tf, d), lambda i, j: (j, 0)),
                  pl.BlockSpec((1, d), lambda i, j: (0, 0))],
        out_specs=pl.BlockSpec((tm, d), lambda i, j: (i, 0), **once),
        out_shape=jax.ShapeDtypeStruct((m, d), F32),
        scratch_shapes=[pltpu.VMEM((tm, d), F32)],
        compiler_params=pltpu.CompilerParams(dimension_semantics=("parallel", "arbitrary"),
                                             vmem_limit_bytes=FFN_VMEM_LIMIT),
        name="ffn",
    )(h2, x1, w_gate, w_up, w_down, g_final)


def _group_matrices():
    lane = np.arange(LANES)
    key_head = lane % RWKV_HEADS
    val_head = lane // I_LO
    sum_key = (key_head[:, None] == key_head[None, :]).astype(np.float32)
    sum_val = (val_head[:, None] == val_head[None, :]).astype(np.float32)
    key_to_val = (key_head[:, None] == val_head[None, :]).astype(np.float32)
    return sum_key, sum_val, key_to_val


def _state_to_kernel(s):
    b = s.shape[0]
    s = s.reshape(b, RWKV_HEADS, I_GROUPS, I_LO, J_GROUPS, SUBLANES)
    return s.transpose(0, 4, 2, 5, 1, 3).reshape(b, HEAD_DIM, SUBLANES, LANES)


def _state_from_kernel(s):
    b = s.shape[0]
    s = s.reshape(b, J_GROUPS, I_GROUPS, SUBLANES, RWKV_HEADS, I_LO)
    return s.transpose(0, 4, 2, 5, 1, 3).reshape(b, RWKV_HEADS, HEAD_DIM, HEAD_DIM)


def _shift_to_cols(shift):
    c = RWKV_WIDTH
    return (_key_order(shift[:, 0:c]), _key_order(shift[:, c:2 * c]), _val_order(shift[:, 2 * c:3 * c]),
            jnp.pad(shift[:, 3 * c:], ((0, 0), (0, LORA_PAD - LORA_COLS))))


def _shift_from_proj(rows):
    return jnp.concatenate([_key_natural(rows[:, COL_R:COL_K]), _key_natural(rows[:, COL_K:COL_V]),
                            _val_natural(rows[:, COL_V:COL_SK]), rows[:, COL_L:COL_L + LORA_COLS]], axis=1)


def _layer_weights(norm_attn, w_in, mu_shift, w0, w2, a0, a2, g2, k_k, k_a, r_k, ln_x_w, ln_x_b, sinks, w_out,
                   norm_ffn, w_gate, w_up, w_down, norm_final):
    c = RWKV_WIDTH
    swa0 = RWKV_COLS
    w_proj = jnp.concatenate([
        w_in[:, swa0:swa0 + SWA_WIDTH],
        _key_order(w_in[:, 0:c]), _key_order(w_in[:, c:2 * c]), _val_order(w_in[:, 2 * c:3 * c]),
        w_in[:, swa0 + SWA_WIDTH:swa0 + SWA_WIDTH + 2 * KV_WIDTH],
        jnp.pad(w_in[:, 3 * c:RWKV_COLS], ((0, 0), (0, LORA_PAD - LORA_COLS))),
    ], axis=1).astype(BF16)
    row = lambda v: v.reshape(1, -1)
    zero = jnp.zeros((c,), F32)
    kvec = _key_order(jnp.stack([mu_shift[0:c], mu_shift[c:2 * c], w0, a0, k_k, k_a, r_k.reshape(-1), zero]))
    vvec = _val_order(jnp.stack([mu_shift[2 * c:3 * c], ln_x_w, ln_x_b] + [zero] * 5))
    mu_l = jnp.pad(mu_shift[3 * c:], (0, LORA_PAD - LORA_COLS)).reshape(1, -1)
    lora_wa = _key_order(jnp.concatenate([w2, a2], axis=0)).astype(BF16)
    g2p = jnp.pad(_val_order(g2), ((0, LORA_PAD - LANES - GATE_LORA), (0, 0))).astype(BF16)
    sum_key, sum_val, key_to_val = _group_matrices()
    prep_params = (kvec, vvec, mu_l, lora_wa, g2p, jnp.asarray(sum_key, BF16))
    post_params = (kvec, vvec, jnp.asarray(sum_val, BF16), jnp.asarray(key_to_val, BF16))
    w_out_r = w_out[:c].reshape(RWKV_HEADS, I_GROUPS, I_LO, -1).swapaxes(0, 1).reshape(c, -1)
    return dict(
        norm_attn=row(norm_attn), w_proj=w_proj, prep=prep_params, post=post_params, sinks=sinks,
        w_out_r=w_out_r.astype(BF16), w_out_s=w_out[c:].astype(BF16), norm_ffn=row(norm_ffn),
        w_gate=w_gate.astype(BF16), w_up=w_up.astype(BF16), w_down=w_down.astype(BF16), norm_final=row(norm_final))


def _layer(x, shift_prev, state0, kv_past, lw, *, tiles):
    bsz, t_len, d = x.shape
    m = bsz * t_len
    x2 = x.reshape(m, d)
    proj = _inproj(x2, lw["norm_attn"], lw["w_proj"], tm=tiles["tm_in"], tn=tiles["tn_in"])

    fix_cols = _shift_to_cols(shift_prev)
    if bsz == 1:
        fixes, period = fix_cols, None
    else:
        fixes = tuple(jnp.pad(f[:, None, :], ((0, 0), (0, t_len - 1), (0, 0))).reshape(m, -1) for f in fix_cols)
        period = t_len
    a, w, b, k, r, v, g = _prep(proj, fixes, lw["prep"], tm=tiles["tm_prep"], period=period)

    tok = lambda z: z.reshape(bsz, t_len, SUBLANES, LANES)
    o, s_fin = _rwkv_scan(tok(a), tok(w), tok(b), tok(k), tok(r), tok(v), _state_to_kernel(state0), tb=tiles["tb"],
                          sub=min(SCAN_SUB, t_len), bb=min(tiles["bb_scan"], bsz))

    kv_shape = (bsz, WINDOW, SWA_KV_HEADS, HEAD_DIM)
    if kv_past is None:
        o_swa = _attn_prompt(proj, lw["sinks"], sub_blocks=tiles["attn_sub"])
        new_k = proj[m - WINDOW:, COL_SK:COL_SV].reshape(kv_shape)
        new_v = proj[m - WINDOW:, COL_SV:COL_L].reshape(kv_shape)
    else:
        k_past, v_past = (z.reshape(bsz, WINDOW, KV_WIDTH) for z in kv_past)
        o_swa, new_k, new_v = _attn_decode(proj.reshape(bsz, t_len, PROJ_COLS), k_past, v_past, lw["sinks"],
                                           bt=min(tiles["bt_attn"], bsz))
        o_swa, new_k, new_v = o_swa.reshape(m, SWA_WIDTH), new_k.reshape(kv_shape), new_v.reshape(kv_shape)

    x1, h2 = _outproj(x2, o.reshape(m, RWKV_WIDTH), r, k, v, g, o_swa, lw["post"], lw["w_out_r"], lw["w_out_s"],
                      lw["norm_ffn"], tm=tiles["tm_out"])
    y = _ffn(h2, x1, lw["w_gate"], lw["w_up"], lw["w_down"], lw["norm_final"], tm=tiles["tm_ffn"], tf=tiles["tf"])

    new_shift = _shift_from_proj(proj.reshape(bsz, t_len, PROJ_COLS)[:, -1])
    return y.reshape(bsz, t_len, d), _state_from_kernel(s_fin), new_shift, new_k, new_v


def _tiles(m, t_len):
    return dict(tm_in=min(1024, m), tn_in=1664, tm_prep=min(512, m), tb=min(256, t_len),
                tm_out=min(256, m), tm_ffn=min(512, m), tf=1408, bt_attn=16, attn_sub=4, bb_scan=4)


def kernel(x_prompt, x_sample, state_rwkv, state_shift, cache_swa_k, cache_swa_v, norm_attn, w_in, mu_shift, w0, w2, a0, a2, g2, k_k, k_a, r_k, ln_x_w, ln_x_b, sinks, w_out, norm_ffn, w_gate, w_up, w_down, norm_final):
    assert norm_attn.shape[0] == 1, "single trunk layer"
    lw = _layer_weights(norm_attn[0], w_in[0], mu_shift[0], w0[0], w2[0], a0[0], a2[0], g2[0], k_k[0], k_a[0],
                        r_k[0], ln_x_w[0], ln_x_b[0], sinks[0], w_out[0], norm_ffn[0], w_gate[0], w_up[0],
                        w_down[0], norm_final)
    bp, tp, _ = x_prompt.shape
    bs, ts, _ = x_sample.shape
    assert bp == 1 and tp % BLOCK == 0 and ts < SUBLANES
    yp, p_state, p_shift, p_k, p_v = _layer(
        x_prompt, jnp.zeros((bp, RWKV_COLS), F32), jnp.zeros((bp, RWKV_HEADS, HEAD_DIM, HEAD_DIM), F32),
        None, lw, tiles=_tiles(bp * tp, tp))
    ys, s_state, s_shift, s_k, s_v = _layer(
        x_sample, state_shift[0], state_rwkv[0], (cache_swa_k[0], cache_swa_v[0]), lw, tiles=_tiles(bs * ts, ts))
    return (yp, ys, p_state[None], p_shift[None], p_k[None], p_v[None],
            s_state[None], s_shift[None], s_k[None], s_v[None])
```

```python
import functools

import numpy as np
import jax
import jax.numpy as jnp
from jax import lax
from jax.experimental import pallas as pl
from jax.experimental.pallas import tpu as pltpu

F32 = jnp.float32
BF16 = jnp.bfloat16

D_MODEL = 2048
HEAD_DIM = 64
RWKV_WIDTH = 1024
RWKV_HEADS = 16
SWA_WIDTH = 1024
SWA_HEADS = 16
SWA_KV_HEADS = 4
SWA_GROUP = 4
KV_WIDTH = 256
WINDOW = 128
BLOCK = 128
DECAY_LORA = 64
AAA_LORA = 64
GATE_LORA = 160
LORA_COLS = DECAY_LORA + AAA_LORA + GATE_LORA
RWKV_COLS = 3 * RWKV_WIDTH + LORA_COLS
D_FF = 5632
RMS_EPS = 1e-5
GN_EPS = 64e-5

SUBLANES = 8
LANES = 128
I_LO = LANES // RWKV_HEADS


def _reorder_last(x, split, a, b):
    return x.reshape(x.shape[:-1] + split).swapaxes(a, b).reshape(x.shape)


def _key_order(x):
    return _reorder_last(x, (RWKV_HEADS, HEAD_DIM), -1, -2)


def _key_natural(x):
    return _reorder_last(x, (HEAD_DIM, RWKV_HEADS), -1, -2)


def _val_order(x):
    return _reorder_last(x, (RWKV_HEADS, HEAD_DIM // I_LO, I_LO), -3, -2)


def _val_natural(x):
    return _reorder_last(x, (HEAD_DIM // I_LO, RWKV_HEADS, I_LO), -3, -2)


J_GROUPS = HEAD_DIM // SUBLANES
I_GROUPS = HEAD_DIM // I_LO
SCAN_SUB = 64


def _sublane_allsum(x):
    for shift in (4, 2, 1):
        x = x + pltpu.roll(x, shift, 0)
    return x


def _sublane_sums(xs):
    sub = lax.broadcasted_iota(jnp.int32, (SUBLANES, LANES), 0)

    def combine(x, y, d):
        clear = (sub & d) == 0
        return jnp.where(clear, x, pltpu.roll(y, d, 0)) + jnp.where(clear, pltpu.roll(x, SUBLANES - d, 0), y)

    z = [combine(xs[0], xs[4], 4), combine(xs[2], xs[6], 4), combine(xs[1], xs[5], 4), combine(xs[3], xs[7], 4)]
    return combine(combine(z[0], z[1], 2), combine(z[2], z[3], 2), 1)


def _cumprod_leading(x):
    n, shift = x.shape[0], 1
    while shift < n:
        x = jnp.concatenate([x[:shift], x[shift:] * x[:-shift]], axis=0)
        shift *= 2
    return x


def _scan_kernel(a_ref, w_ref, b_ref, k_ref, r_ref, v_ref, s0_ref, o_ref, sT_ref, s_scr, sa_scr, src_scr, pend_scr,
                 *tiles, tb, sub, bb):
    ti = pl.program_id(1)
    n_sub = tb // sub

    @pl.when(ti == 0)
    def _():
        s_scr[...] = s0_ref[...]

    gather_idx = (lax.broadcasted_iota(jnp.int32, (SUBLANES, LANES), 0) * RWKV_HEADS
                  + lax.broadcasted_iota(jnp.int32, (SUBLANES, LANES), 1) // I_LO)

    def gather(src_row):
        return jnp.take_along_axis(jnp.broadcast_to(src_row, (SUBLANES, LANES)), gather_idx, axis=1)

    A, B, K, R = range(4)
    tile = lambda bi, qi, parity: tiles[2 * qi + parity].at[bi]

    def prepare(bi):
        p_end = None
        for n in range(n_sub):
            rows = slice(n * sub, (n + 1) * sub)
            p = _cumprod_leading(w_ref[bi, rows])
            p_inv = 1.0 / p
            first = jnp.ones_like(p[:1]) if p_end is None else p_end[None]
            p_prev = jnp.concatenate([first, p[:-1]], axis=0)
            src_scr[bi, A, rows] = a_ref[bi, rows] * p_prev
            src_scr[bi, B, rows] = b_ref[bi, rows] * p_inv
            src_scr[bi, K, rows] = k_ref[bi, rows] * p_inv
            src_scr[bi, R, rows] = r_ref[bi, rows] * p
            p_end = p[sub - 1]
            for jg in range(J_GROUPS):
                pend_scr[bi, n, jg] = gather(p_end[jg:jg + 1, :])

    def build(bi, qi, t, parity):
        src_t = jnp.minimum(t, tb - 1)
        for jg in range(J_GROUPS):
            tile(bi, qi, parity)[jg] = gather(src_scr[bi, qi, src_t, pl.ds(jg, 1), :])

    def first_sa(bi):
        for qi in range(4):
            build(bi, qi, 0, 0)
        build(bi, A, 1, 1)
        sa0 = [None] * I_GROUPS
        for jg in range(J_GROUPS):
            a0 = tile(bi, A, 0)[jg]
            for ig in range(I_GROUPS):
                term = s_scr[bi, jg * I_GROUPS + ig] * a0
                sa0[ig] = term if sa0[ig] is None else sa0[ig] + term
        return [_sublane_allsum(x) for x in sa0]

    def step(bi, t, parity, sa, rescale=None):
        for qi in (B, K, R):
            build(bi, qi, t + 1, 1 - parity)
        build(bi, A, t + 2, parity)
        vb = [jnp.broadcast_to(v_ref[bi, t, pl.ds(ig, 1), :], (SUBLANES, LANES)) for ig in range(I_GROUPS)]
        out = [None] * I_GROUPS
        nxt = [None] * I_GROUPS
        for jg in range(J_GROUPS):
            b, k, r = (tile(bi, qi, parity)[jg] for qi in (B, K, R))
            a_next = tile(bi, A, 1 - parity)[jg]
            for ig in range(I_GROUPS):
                idx = jg * I_GROUPS + ig
                s_old = s_scr[bi, idx]
                if rescale is not None:
                    s_old = s_old * pend_scr[bi, rescale, jg]
                sj = s_old + sa[ig] * b + vb[ig] * k
                s_scr[bi, idx] = sj
                to, tn = sj * r, sj * a_next
                out[ig] = to if out[ig] is None else out[ig] + to
                nxt[ig] = tn if nxt[ig] is None else nxt[ig] + tn
        o_ref[bi, t] = _sublane_sums(out)
        sa_scr[bi, parity] = _sublane_sums(nxt)
        return [jnp.broadcast_to(sa_scr[bi, parity, pl.ds(ig, 1), :], (SUBLANES, LANES)) for ig in range(I_GROUPS)]

    def finish(bi):
        for jg in range(J_GROUPS):
            for ig in range(I_GROUPS):
                idx = jg * I_GROUPS + ig
                s_scr[bi, idx] = s_scr[bi, idx] * pend_scr[bi, n_sub - 1, jg]

    for bi in range(bb):
        prepare(bi)
    if tb <= SUBLANES:
        sas = [first_sa(bi) for bi in range(bb)]
        for t in range(tb):
            sas = [step(bi, t, t % 2, sas[bi]) for bi in range(bb)]
    else:
        for bi in range(bb):
            sa = first_sa(bi)
            for n in range(n_sub):
                t0 = n * sub
                sa = step(bi, t0, 0, sa, rescale=n - 1 if n else None)
                sa = step(bi, t0 + 1, 1, sa)

                def two_steps(u, sa, bi=bi):
                    return step(bi, 2 * u + 1, 1, step(bi, 2 * u, 0, sa))

                sa = lax.fori_loop(t0 // 2 + 1, (t0 + sub) // 2, two_steps, sa)
    for bi in range(bb):
        finish(bi)

    @pl.when(ti == pl.num_programs(1) - 1)
    def _():
        sT_ref[...] = s_scr[...]


def _rwkv_scan(a, w, b, k, r, v, s0, *, tb, sub, bb):
    bsz, t_len = a.shape[0], a.shape[1]
    assert t_len % tb == 0 and tb % sub == 0 and sub % 2 == 0 and bsz % bb == 0
    tok_spec = pl.BlockSpec((bb, tb, SUBLANES, LANES), lambda bi, ti: (bi, ti, 0, 0))
    st_spec = pl.BlockSpec((bb, HEAD_DIM, SUBLANES, LANES), lambda bi, ti: (bi, 0, 0, 0))
    return pl.pallas_call(
        functools.partial(_scan_kernel, tb=tb, sub=sub, bb=bb),
        grid=(bsz // bb, t_len // tb),
        in_specs=[tok_spec] * 6 + [st_spec],
        out_specs=[tok_spec, st_spec],
        out_shape=[jax.ShapeDtypeStruct(a.shape, F32), jax.ShapeDtypeStruct(s0.shape, F32)],
        scratch_shapes=([pltpu.VMEM((bb, HEAD_DIM, SUBLANES, LANES), F32), pltpu.VMEM((bb, 2, SUBLANES, LANES), F32),
                         pltpu.VMEM((bb, 4, tb, SUBLANES, LANES), F32),
                         pltpu.VMEM((bb, tb // sub, J_GROUPS, SUBLANES, LANES), F32)]
                        + [pltpu.VMEM((bb, J_GROUPS, SUBLANES, LANES), F32)] * 8),
        compiler_params=pltpu.CompilerParams(dimension_semantics=("parallel", "arbitrary"),
                                             vmem_limit_bytes=VMEM_LIMIT),
        name="rwkv_scan",
    )(a, w, b, k, r, v, s0)


PROJ_COLS = 5120
LORA_PAD = 384
COL_Q, COL_R, COL_K, COL_V, COL_SK, COL_SV, COL_L = 0, 1024, 2048, 3072, 4096, 4352, 4608
VMEM_LIMIT = 56 * 1024 * 1024


def _rms(x, g):
    return x * lax.rsqrt(jnp.mean(x * x, axis=-1, keepdims=True) + RMS_EPS) * g


def _inproj_kernel(x_ref, g_ref, w_ref, o_ref, xn_ref):
    @pl.when(pl.program_id(1) == 0)
    def _():
        xn_ref[...] = _rms(x_ref[...], g_ref[...]).astype(BF16)

    o_ref[...] = jnp.dot(xn_ref[...], w_ref[...], preferred_element_type=F32)


def _inproj(x, g, w, *, tm, tn):
    m, d = x.shape
    n = w.shape[1]
    return pl.pallas_call(
        _inproj_kernel,
        grid=(m // tm, n // tn),
        in_specs=[pl.BlockSpec((tm, d), lambda i, j: (i, 0)),
                  pl.BlockSpec((1, d), lambda i, j: (0, 0)),
                  pl.BlockSpec((d, tn), lambda i, j: (0, j))],
        out_specs=pl.BlockSpec((tm, tn), lambda i, j: (i, j)),
        out_shape=jax.ShapeDtypeStruct((m, n), F32),
        scratch_shapes=[pltpu.VMEM((tm, d), BF16)],
        compiler_params=pltpu.CompilerParams(dimension_semantics=("parallel", "arbitrary"),
                                             vmem_limit_bytes=VMEM_LIMIT),
        name="inproj",
    )(x, g, w)


def _tile_sum(x):
    parts = [x[:, c * LANES:(c + 1) * LANES] for c in range(RWKV_WIDTH // LANES)]
    while len(parts) > 1:
        parts = [parts[i] + parts[i + 1] for i in range(0, len(parts), 2)]
    return parts[0]


def _tile8(x):
    return jnp.concatenate([x] * (RWKV_WIDTH // LANES), axis=1)


def _group_sum(x, m):
    hi = x.astype(BF16)
    rest = x - hi.astype(F32)
    mid = rest.astype(BF16)
    lo = (rest - mid.astype(F32)).astype(BF16)
    dot = lambda a: jnp.dot(a, m, preferred_element_type=F32)
    return dot(hi) + dot(mid) + dot(lo)


def _lora_dot(x, w):
    return jnp.dot(x.astype(BF16), w, preferred_element_type=F32)


KV_MU_R, KV_MU_K, KV_W0, KV_A0, KV_KK, KV_KA, KV_RK = range(7)
VV_MU_V, VV_LN_W, VV_LN_B = range(3)


def _prep_kernel(pr_ref, pk_ref, pv_ref, pl_ref, fr_ref, fk_ref, fv_ref, fl_ref,
                 kvec_ref, vvec_ref, mul_ref, wa_ref, g2_ref, sumh_ref,
                 a_out, w_out, b_out, k_out, r_out, v_out, g_out,
                 cr_ref, ck_ref, cv_ref, cl_ref, *, period):
    krow = lambda i: kvec_ref[i:i + 1, :]

    def shifted(p_ref, fix_ref, carry_ref, mu):
        p = p_ref[...]
        rolled = pltpu.roll(p, 1, 0)
        rows = lax.broadcasted_iota(jnp.int32, p.shape, 0)
        if period is None:
            @pl.when(pl.program_id(0) == 0)
            def _():
                carry_ref[...] = fix_ref[...]

            prev = jnp.where(rows == 0, carry_ref[...], rolled)
            carry_ref[...] = p[p.shape[0] - 1:, :]
        else:
            prev = jnp.where(rows % period == 0, fix_ref[...], rolled)
        return p + mu * (prev - p)

    r = shifted(pr_ref, fr_ref, cr_ref, krow(KV_MU_R))
    k = shifted(pk_ref, fk_ref, ck_ref, krow(KV_MU_K))
    v = shifted(pv_ref, fv_ref, cv_ref, vvec_ref[VV_MU_V:VV_MU_V + 1, :])
    xl = shifted(pl_ref, fl_ref, cl_ref, mul_ref[...])

    x_wa = xl[:, :LANES]
    is_w = lax.broadcasted_iota(jnp.int32, x_wa.shape, 1) < DECAY_LORA
    zw = krow(KV_W0) + _lora_dot(jnp.where(is_w, jnp.tanh(x_wa), 0.0), wa_ref[...])
    za = krow(KV_A0) + _lora_dot(jnp.where(is_w, 0.0, x_wa), wa_ref[...])
    g = _lora_dot(jax.nn.sigmoid(xl[:, LANES:]), g2_ref[...])
    y = -zw
    softplus = jnp.maximum(y, 0.0) + jnp.log(1.0 + jnp.exp(-jnp.abs(y)))
    decay = jnp.exp(-jnp.exp(-softplus - 0.5))
    a_sig = jax.nn.sigmoid(za)

    kk = k * krow(KV_KK)
    ssq = _tile8(_group_sum(_tile_sum(kk * kk), sumh_ref[...]))
    kk = kk / jnp.maximum(jnp.sqrt(ssq), 1e-12)

    a_out[...] = -kk
    w_out[...] = decay
    b_out[...] = kk * a_sig
    k_out[...] = k * (1.0 + (a_sig - 1.0) * krow(KV_KA))
    r_out[...] = r
    v_out[...] = v
    g_out[...] = g


def _prep(proj, fixes, params, *, tm, period):
    m = proj.shape[0]
    c = RWKV_WIDTH
    grid = (m // tm,)

    def col(width, idx):
        return pl.BlockSpec((tm, width), lambda i: (i, idx))

    def full(arr):
        return pl.BlockSpec(arr.shape, lambda i: (0,) * arr.ndim)

    if period is None:
        fix_specs = [full(f) for f in fixes]
    else:
        fix_specs = [pl.BlockSpec((tm, f.shape[1]), lambda i: (i, 0)) for f in fixes]
    out_spec = pl.BlockSpec((tm, c), lambda i: (i, 0))
    return pl.pallas_call(
        functools.partial(_prep_kernel, period=period),
        grid=grid,
        in_specs=[col(c, COL_R // c), col(c, COL_K // c), col(c, COL_V // c), col(LORA_PAD, COL_L // LORA_PAD)]
        + fix_specs + [full(p) for p in params],
        out_specs=[out_spec] * 7,
        out_shape=[jax.ShapeDtypeStruct((m, c), F32)] * 7,
        scratch_shapes=[pltpu.VMEM((1, c), F32)] * 3 + [pltpu.VMEM((1, LORA_PAD), F32)],
        compiler_params=pltpu.CompilerParams(dimension_semantics=("arbitrary",), vmem_limit_bytes=VMEM_LIMIT),
        name="rwkv_prep",
    )(proj, proj, proj, proj, *fixes, *params)


def _rwkv_out(o, r, k, v, g, kvec_ref, vvec_ref, gsum_ref, kv_ref):
    inv_n = 1.0 / HEAD_DIM
    mean = _tile8(_group_sum(_tile_sum(o), gsum_ref[...])) * inv_n
    d = o - mean
    var = _tile8(_group_sum(_tile_sum(d * d), gsum_ref[...])) * inv_n
    normed = (d * lax.rsqrt(var + GN_EPS) * vvec_ref[VV_LN_W:VV_LN_W + 1, :]
              + vvec_ref[VV_LN_B:VV_LN_B + 1, :])
    rk = _tile8(_group_sum(_tile_sum(r * k * kvec_ref[KV_RK:KV_RK + 1, :]), kv_ref[...]))
    return ((normed + rk * v) * g).astype(BF16)


ALIBI_SLOPES = [2.0 ** (-8.0 * (i + 1) / SWA_HEADS) for i in range(SWA_HEADS)]
SCORE_SCALE = HEAD_DIM ** -0.5


LOG2E = 1.4426950408889634


def _attn_block_kernel(sink_ref, q_ref, kc_ref, vc_ref, kp_ref, vp_ref, o_ref, ot_scr, bias_scr, *, sub_blocks):
    n = pl.program_id(0)
    span = WINDOW + BLOCK

    @pl.when(n == 0)
    def _():
        si = lax.broadcasted_iota(jnp.int32, (span, BLOCK), 0)
        qi = lax.broadcasted_iota(jnp.int32, (span, BLOCK), 1)
        dist = qi - si + WINDOW
        valid = (dist >= 0) & (dist < WINDOW)
        distf = dist.astype(F32)
        for h in range(SWA_HEADS):
            bias = jnp.where(valid, (-ALIBI_SLOPES[h] * LOG2E) * distf, -jnp.inf)
            bias_scr[1, h] = bias
            bias_scr[0, h] = jnp.where(si >= WINDOW, bias, -jnp.inf)

    for sb in range(sub_blocks):
        rows = slice(sb * BLOCK, (sb + 1) * BLOCK)
        if sb == 0:
            k_prev, v_prev = kp_ref[...], vp_ref[...]
            table = jnp.where(n == 0, 0, 1)
        else:
            prev_rows = slice((sb - 1) * BLOCK, sb * BLOCK)
            k_prev, v_prev = kc_ref[prev_rows, :], vc_ref[prev_rows, :]
            table = 1
        k_span = jnp.concatenate([k_prev, kc_ref[rows, :]], axis=0)
        v_t = jnp.concatenate([v_prev, vc_ref[rows, :]], axis=0).T.astype(BF16)
        for kh in range(SWA_KV_HEADS):
            ks = k_span[:, kh * HEAD_DIM:(kh + 1) * HEAD_DIM].astype(BF16)
            heads = range(kh * SWA_GROUP, (kh + 1) * SWA_GROUP)
            qg = jnp.concatenate([q_ref[rows, h * HEAD_DIM:(h + 1) * HEAD_DIM] for h in heads], axis=0).astype(BF16)
            s_all = lax.dot_general(ks, qg, (((1,), (1,)), ((), ())), preferred_element_type=F32)
            probs, dens = [], []
            for gi, h in enumerate(heads):
                s = s_all[:, gi * BLOCK:(gi + 1) * BLOCK] * (SCORE_SCALE * LOG2E) + bias_scr[table, h]
                sink = sink_ref[h] * LOG2E
                m = jnp.maximum(jnp.max(s, axis=0, keepdims=True), sink)
                e = jnp.exp2(s - m)
                dens.append(jnp.sum(e, axis=0, keepdims=True) + jnp.exp2(sink - m))
                probs.append(e.astype(BF16))
            o_t = jnp.dot(v_t[kh * HEAD_DIM:(kh + 1) * HEAD_DIM, :], jnp.concatenate(probs, axis=1),
                          preferred_element_type=F32) / jnp.concatenate(dens, axis=1)
            for gi, h in enumerate(heads):
                ot_scr[sb, h * HEAD_DIM:(h + 1) * HEAD_DIM, :] = o_t[:, gi * BLOCK:(gi + 1) * BLOCK]
        o_ref[rows, :] = ot_scr[sb].T.astype(BF16)


def _attn_prompt(proj, sinks, *, sub_blocks):
    t = proj.shape[0]
    tq = sub_blocks * BLOCK
    cur = lambda width, idx: pl.BlockSpec((tq, width), lambda n: (n, idx))
    prev = lambda width, idx: pl.BlockSpec((BLOCK, width), lambda n: (jnp.maximum(n * sub_blocks - 1, 0), idx))
    return pl.pallas_call(
        functools.partial(_attn_block_kernel, sub_blocks=sub_blocks),
        grid=(t // tq,),
        in_specs=[pl.BlockSpec(memory_space=pltpu.SMEM),
                  cur(SWA_WIDTH, COL_Q // SWA_WIDTH), cur(KV_WIDTH, COL_SK // KV_WIDTH), cur(KV_WIDTH, COL_SV // KV_WIDTH),
                  prev(KV_WIDTH, COL_SK // KV_WIDTH), prev(KV_WIDTH, COL_SV // KV_WIDTH)],
        out_specs=pl.BlockSpec((tq, SWA_WIDTH), lambda n: (n, 0)),
        out_shape=jax.ShapeDtypeStruct((t, SWA_WIDTH), BF16),
        scratch_shapes=[pltpu.VMEM((sub_blocks, SWA_WIDTH, BLOCK), F32),
                        pltpu.VMEM((2, SWA_HEADS, WINDOW + BLOCK, BLOCK), F32)],
        compiler_params=pltpu.CompilerParams(dimension_semantics=("arbitrary",), vmem_limit_bytes=VMEM_LIMIT),
        name="swa_prompt",
    )(sinks, proj, proj, proj, proj, proj)


def _attn_decode_kernel(sink_ref, q_ref, kn_ref, vn_ref, kp_ref, vp_ref, o_ref, ko_ref, vo_ref, *, t_new, bt):
    span = WINDOW + SUBLANES
    rows = SWA_GROUP * SUBLANES
    row = lax.broadcasted_iota(jnp.int32, (bt, rows, span), 1)
    si = lax.broadcasted_iota(jnp.int32, (bt, rows, span), 2)
    dist = row % SUBLANES - si + WINDOW
    valid = (dist >= 0) & (dist < WINDOW)
    distf = dist.astype(F32)
    grp = row // SUBLANES
    grp_col = grp[:, :, :1]

    def per_group(tile_grp, vals):
        out = vals[-1]
        for gi in range(SWA_GROUP - 2, -1, -1):
            out = jnp.where(tile_grp == gi, vals[gi], out)
        return out

    pad = jnp.zeros((bt, SUBLANES - t_new, HEAD_DIM), F32)
    for kh in range(SWA_KV_HEADS):
        lanes = slice(kh * HEAD_DIM, (kh + 1) * HEAD_DIM)
        heads = range(kh * SWA_GROUP, (kh + 1) * SWA_GROUP)
        ks = jnp.concatenate([kp_ref[:, :, lanes], kn_ref[:, :, lanes], pad], axis=1).astype(BF16)
        vs = jnp.concatenate([vp_ref[:, :, lanes], vn_ref[:, :, lanes], pad], axis=1).astype(BF16)
        qg = jnp.concatenate([piece for h in heads for piece in (q_ref[:, :, h * HEAD_DIM:(h + 1) * HEAD_DIM], pad)],
                             axis=1).astype(BF16)
        s = jnp.einsum("bqd,bsd->bqs", qg, ks, preferred_element_type=F32) * SCORE_SCALE
        slope = per_group(grp, [ALIBI_SLOPES[h] for h in heads])
        s = jnp.where(valid, s - slope * distf, -jnp.inf)
        sink = per_group(grp_col, [sink_ref[h] for h in heads])
        m = jnp.maximum(jnp.max(s, axis=-1, keepdims=True), sink)
        e = jnp.exp(s - m)
        den = jnp.sum(e, axis=-1, keepdims=True) + jnp.exp(sink - m)
        og = jnp.einsum("bqs,bsd->bqd", (e / den).astype(BF16), vs, preferred_element_type=F32)
        for gi, h in enumerate(heads):
            o_ref[:, :, h * HEAD_DIM:(h + 1) * HEAD_DIM] = og[:, gi * SUBLANES:gi * SUBLANES + t_new, :].astype(BF16)
    ko_ref[:, :WINDOW - t_new, :] = kp_ref[:, t_new:, :]
    ko_ref[:, WINDOW - t_new:, :] = kn_ref[...]
    vo_ref[:, :WINDOW - t_new, :] = vp_ref[:, t_new:, :]
    vo_ref[:, WINDOW - t_new:, :] = vn_ref[...]


def _attn_decode(proj3, k_past, v_past, sinks, *, bt):
    bsz, t_new = proj3.shape[0], proj3.shape[1]
    new = lambda width, idx: pl.BlockSpec((bt, t_new, width), lambda b: (b, 0, idx))
    past = pl.BlockSpec((bt, WINDOW, KV_WIDTH), lambda b: (b, 0, 0))
    return pl.pallas_call(
        functools.partial(_attn_decode_kernel, t_new=t_new, bt=bt),
        grid=(bsz // bt,),
        in_specs=[pl.BlockSpec(memory_space=pltpu.SMEM),
                  new(SWA_WIDTH, COL_Q // SWA_WIDTH), new(KV_WIDTH, COL_SK // KV_WIDTH), new(KV_WIDTH, COL_SV // KV_WIDTH),
                  past, past],
        out_specs=[pl.BlockSpec((bt, t_new, SWA_WIDTH), lambda b: (b, 0, 0)), past, past],
        out_shape=[jax.ShapeDtypeStruct((bsz, t_new, SWA_WIDTH), BF16),
                   jax.ShapeDtypeStruct(k_past.shape, F32), jax.ShapeDtypeStruct(v_past.shape, F32)],
        compiler_params=pltpu.CompilerParams(dimension_semantics=("parallel",), vmem_limit_bytes=VMEM_LIMIT),
        name="swa_decode",
    )(sinks, proj3, proj3, proj3, k_past, v_past)


def _outproj_kernel(x_ref, o_ref, r_ref, k_ref, v_ref, g_ref, osw_ref, kvec_ref, vvec_ref, gsum_ref, kv_ref,
                    wr_ref, ws_ref, gn_ref, x1_ref, h2_ref):
    o_rwkv = _rwkv_out(o_ref[...], r_ref[...], k_ref[...], v_ref[...], g_ref[...], kvec_ref, vvec_ref, gsum_ref, kv_ref)
    x1 = (x_ref[...] + jnp.dot(o_rwkv, wr_ref[...], preferred_element_type=F32)
          + jnp.dot(osw_ref[...], ws_ref[...], preferred_element_type=F32))
    x1_ref[...] = x1
    h2_ref[...] = _rms(x1, gn_ref[...]).astype(BF16)


def _outproj(x, o, r, k, v, g, o_swa, post_params, w_r, w_s, g_norm, *, tm):
    m, d = x.shape
    row = lambda width: pl.BlockSpec((tm, width), lambda i: (i, 0))
    full = lambda arr: pl.BlockSpec(arr.shape, lambda i: (0, 0))
    consts = (*post_params, w_r, w_s, g_norm)
    return pl.pallas_call(
        _outproj_kernel,
        grid=(m // tm,),
        in_specs=[row(d)] + [row(RWKV_WIDTH)] * 5 + [row(SWA_WIDTH)] + [full(c) for c in consts],
        out_specs=[row(d), row(d)],
        out_shape=[jax.ShapeDtypeStruct((m, d), F32), jax.ShapeDtypeStruct((m, d), BF16)],
        compiler_params=pltpu.CompilerParams(dimension_semantics=("parallel",), vmem_limit_bytes=VMEM_LIMIT),
        name="outproj",
    )(x, o, r, k, v, g, o_swa, *consts)


def _ffn_kernel(h_ref, x1_ref, wg_ref, wu_ref, wd_ref, gf_ref, y_ref, acc_ref):
    j = pl.program_id(1)

    @pl.when(j == 0)
    def _():
        acc_ref[...] = jnp.zeros_like(acc_ref)

    h = h_ref[...]
    gate = jnp.dot(h, wg_ref[...], preferred_element_type=F32)
    up = jnp.dot(h, wu_ref[...], preferred_element_type=F32)
    act = (gate * jax.nn.sigmoid(gate) * up).astype(BF16)
    acc_ref[...] += jnp.dot(act, wd_ref[...], preferred_element_type=F32)

    @pl.when(j == pl.num_programs(1) - 1)
    def _():
        y_ref[...] = _rms(x1_ref[...] + acc_ref[...], gf_ref[...])


def _ffn(h2, x1, w_gate, w_up, w_down, g_final, *, tm, tf):
    m, d = x1.shape
    f = w_gate.shape[1]
    return pl.pallas_call(
        _ffn_kernel,
        grid=(m // tm, f // tf),
        in_specs=[pl.BlockSpec((tm, d), lambda i, j: (i, 0)),
                  pl.BlockSpec((tm, d), lambda i, j: (i, 0)),
                  pl.BlockSpec((d, tf), lambda i, j: (0, j)),
                  pl.BlockSpec((d, tf), lambda i, j: (0, j)),
                  pl.BlockSpec((tf, d), lambda i, j: (j, 0)),
                  pl.BlockSpec((1, d), lambda i, j: (0, 0))],
        out_specs=pl.BlockSpec((tm, d), lambda i, j: (i, 0)),
        out_shape=jax.ShapeDtypeStruct((m, d), F32),
        scratch_shapes=[pltpu.VMEM((tm, d), F32)],
        compiler_params=pltpu.CompilerParams(dimension_semantics=("parallel", "arbitrary"),
                                             vmem_limit_bytes=VMEM_LIMIT),
        name="ffn",
    )(h2, x1, w_gate, w_up, w_down, g_final)


def _group_matrices():
    lane = np.arange(LANES)
    key_head = lane % RWKV_HEADS
    val_head = lane // I_LO
    sum_key = (key_head[:, None] == key_head[None, :]).astype(np.float32)
    sum_val = (val_head[:, None] == val_head[None, :]).astype(np.float32)
    key_to_val = (key_head[:, None] == val_head[None, :]).astype(np.float32)
    return sum_key, sum_val, key_to_val


def _state_to_kernel(s):
    b = s.shape[0]
    s = s.reshape(b, RWKV_HEADS, I_GROUPS, I_LO, J_GROUPS, SUBLANES)
    return s.transpose(0, 4, 2, 5, 1, 3).reshape(b, HEAD_DIM, SUBLANES, LANES)


def _state_from_kernel(s):
    b = s.shape[0]
    s = s.reshape(b, J_GROUPS, I_GROUPS, SUBLANES, RWKV_HEADS, I_LO)
    return s.transpose(0, 4, 2, 5, 1, 3).reshape(b, RWKV_HEADS, HEAD_DIM, HEAD_DIM)


def _shift_to_cols(shift):
    c = RWKV_WIDTH
    return (_key_order(shift[:, 0:c]), _key_order(shift[:, c:2 * c]), _val_order(shift[:, 2 * c:3 * c]),
            jnp.pad(shift[:, 3 * c:], ((0, 0), (0, LORA_PAD - LORA_COLS))))


def _shift_from_proj(rows):
    return jnp.concatenate([_key_natural(rows[:, COL_R:COL_K]), _key_natural(rows[:, COL_K:COL_V]),
                            _val_natural(rows[:, COL_V:COL_SK]), rows[:, COL_L:COL_L + LORA_COLS]], axis=1)


def _layer_weights(norm_attn, w_in, mu_shift, w0, w2, a0, a2, g2, k_k, k_a, r_k, ln_x_w, ln_x_b, sinks, w_out,
                   norm_ffn, w_gate, w_up, w_down, norm_final):
    c = RWKV_WIDTH
    swa0 = RWKV_COLS
    w_proj = jnp.concatenate([
        w_in[:, swa0:swa0 + SWA_WIDTH],
        _key_order(w_in[:, 0:c]), _key_order(w_in[:, c:2 * c]), _val_order(w_in[:, 2 * c:3 * c]),
        w_in[:, swa0 + SWA_WIDTH:swa0 + SWA_WIDTH + 2 * KV_WIDTH],
        jnp.pad(w_in[:, 3 * c:RWKV_COLS], ((0, 0), (0, PROJ_COLS - COL_L - LORA_COLS))),
    ], axis=1).astype(BF16)
    row = lambda v: v.reshape(1, -1)
    zero = jnp.zeros((c,), F32)
    kvec = _key_order(jnp.stack([mu_shift[0:c], mu_shift[c:2 * c], w0, a0, k_k, k_a, r_k.reshape(-1), zero]))
    vvec = _val_order(jnp.stack([mu_shift[2 * c:3 * c], ln_x_w, ln_x_b] + [zero] * 5))
    mu_l = jnp.pad(mu_shift[3 * c:], (0, LORA_PAD - LORA_COLS)).reshape(1, -1)
    lora_wa = _key_order(jnp.concatenate([w2, a2], axis=0)).astype(BF16)
    g2p = jnp.pad(_val_order(g2), ((0, LORA_PAD - LANES - GATE_LORA), (0, 0))).astype(BF16)
    sum_key, sum_val, key_to_val = _group_matrices()
    prep_params = (kvec, vvec, mu_l, lora_wa, g2p, jnp.asarray(sum_key, BF16))
    post_params = (kvec, vvec, jnp.asarray(sum_val, BF16), jnp.asarray(key_to_val, BF16))
    w_out_r = w_out[:c].reshape(RWKV_HEADS, I_GROUPS, I_LO, -1).swapaxes(0, 1).reshape(c, -1)
    return dict(
        norm_attn=row(norm_attn), w_proj=w_proj, prep=prep_params, post=post_params, sinks=sinks,
        w_out_r=w_out_r.astype(BF16), w_out_s=w_out[c:].astype(BF16), norm_ffn=row(norm_ffn),
        w_gate=w_gate.astype(BF16), w_up=w_up.astype(BF16), w_down=w_down.astype(BF16), norm_final=row(norm_final))


def _layer(x, shift_prev, state0, kv_past, lw, *, tiles):
    bsz, t_len, d = x.shape
    m = bsz * t_len
    x2 = x.reshape(m, d)
    proj = _inproj(x2, lw["norm_attn"], lw["w_proj"], tm=tiles["tm_in"], tn=tiles["tn_in"])

    fix_cols = _shift_to_cols(shift_prev)
    if bsz == 1:
        fixes, period = fix_cols, None
    else:
        fixes = tuple(jnp.pad(f[:, None, :], ((0, 0), (0, t_len - 1), (0, 0))).reshape(m, -1) for f in fix_cols)
        period = t_len
    a, w, b, k, r, v, g = _prep(proj, fixes, lw["prep"], tm=tiles["tm_prep"], period=period)

    tok = lambda z: z.reshape(bsz, t_len, SUBLANES, LANES)
    o, s_fin = _rwkv_scan(tok(a), tok(w), tok(b), tok(k), tok(r), tok(v), _state_to_kernel(state0), tb=tiles["tb"],
                          sub=min(SCAN_SUB, t_len), bb=min(tiles["bb_scan"], bsz))

    kv_shape = (bsz, WINDOW, SWA_KV_HEADS, HEAD_DIM)
    if kv_past is None:
        o_swa = _attn_prompt(proj, lw["sinks"], sub_blocks=tiles["attn_sub"])
        new_k = proj[m - WINDOW:, COL_SK:COL_SV].reshape(kv_shape)
        new_v = proj[m - WINDOW:, COL_SV:COL_L].reshape(kv_shape)
    else:
        k_past, v_past = (z.reshape(bsz, WINDOW, KV_WIDTH) for z in kv_past)
        o_swa, new_k, new_v = _attn_decode(proj.reshape(bsz, t_len, PROJ_COLS), k_past, v_past, lw["sinks"],
                                           bt=min(tiles["bt_attn"], bsz))
        o_swa, new_k, new_v = o_swa.reshape(m, SWA_WIDTH), new_k.reshape(kv_shape), new_v.reshape(kv_shape)

    x1, h2 = _outproj(x2, o.reshape(m, RWKV_WIDTH), r, k, v, g, o_swa, lw["post"], lw["w_out_r"], lw["w_out_s"],
                      lw["norm_ffn"], tm=tiles["tm_out"])
    y = _ffn(h2, x1, lw["w_gate"], lw["w_up"], lw["w_down"], lw["norm_final"], tm=tiles["tm_ffn"], tf=tiles["tf"])

    new_shift = _shift_from_proj(proj.reshape(bsz, t_len, PROJ_COLS)[:, -1])
    return y.reshape(bsz, t_len, d), _state_from_kernel(s_fin), new_shift, new_k, new_v


def _tiles(m, t_len):
    return dict(tm_in=min(1024, m), tn_in=1280, tm_prep=min(512, m), tb=min(256, t_len),
                tm_out=min(256, m), tm_ffn=min(512, m), tf=512, bt_attn=16, attn_sub=8, bb_scan=8)


def kernel(x_prompt, x_sample, state_rwkv, state_shift, cache_swa_k, cache_swa_v, norm_attn, w_in, mu_shift, w0, w2, a0, a2, g2, k_k, k_a, r_k, ln_x_w, ln_x_b, sinks, w_out, norm_ffn, w_gate, w_up, w_down, norm_final):
    assert norm_attn.shape[0] == 1, "single trunk layer"
    lw = _layer_weights(norm_attn[0], w_in[0], mu_shift[0], w0[0], w2[0], a0[0], a2[0], g2[0], k_k[0], k_a[0],
                        r_k[0], ln_x_w[0], ln_x_b[0], sinks[0], w_out[0], norm_ffn[0], w_gate[0], w_up[0],
                        w_down[0], norm_final)
    bp, tp, _ = x_prompt.shape
    bs, ts, _ = x_sample.shape
    assert bp == 1 and tp % BLOCK == 0 and ts < SUBLANES
    yp, p_state, p_shift, p_k, p_v = _layer(
        x_prompt, jnp.zeros((bp, RWKV_COLS), F32), jnp.zeros((bp, RWKV_HEADS, HEAD_DIM, HEAD_DIM), F32),
        None, lw, tiles=_tiles(bp * tp, tp))
    ys, s_state, s_shift, s_k, s_v = _layer(
        x_sample, state_shift[0], state_rwkv[0], (cache_swa_k[0], cache_swa_v[0]), lw, tiles=_tiles(bs * ts, ts))
    return (yp, ys, p_state[None], p_shift[None], p_k[None], p_v[None],
            s_state[None], s_shift[None], s_k[None], s_v[None])
```

```python
import functools

import numpy as np
import jax
import jax.numpy as jnp
from jax import lax
from jax.experimental import pallas as pl
from jax.experimental.pallas import tpu as pltpu

F32 = jnp.float32
BF16 = jnp.bfloat16

D_MODEL = 2048
HEAD_DIM = 64
RWKV_WIDTH = 1024
RWKV_HEADS = 16
SWA_WIDTH = 1024
SWA_HEADS = 16
SWA_KV_HEADS = 4
SWA_GROUP = 4
KV_WIDTH = 256
WINDOW = 128
BLOCK = 128
DECAY_LORA = 64
AAA_LORA = 64
GATE_LORA = 160
LORA_COLS = DECAY_LORA + AAA_LORA + GATE_LORA
RWKV_COLS = 3 * RWKV_WIDTH + LORA_COLS
D_FF = 5632
RMS_EPS = 1e-5
GN_EPS = 64e-5

SUBLANES = 8
LANES = 128
I_LO = LANES // RWKV_HEADS


def _reorder_last(x, split, a, b):
    return x.reshape(x.shape[:-1] + split).swapaxes(a, b).reshape(x.shape)


def _key_order(x):
    return _reorder_last(x, (RWKV_HEADS, HEAD_DIM), -1, -2)


def _key_natural(x):
    return _reorder_last(x, (HEAD_DIM, RWKV_HEADS), -1, -2)


def _val_order(x):
    return _reorder_last(x, (RWKV_HEADS, HEAD_DIM // I_LO, I_LO), -3, -2)


def _val_natural(x):
    return _reorder_last(x, (HEAD_DIM // I_LO, RWKV_HEADS, I_LO), -3, -2)


J_GROUPS = HEAD_DIM // SUBLANES
I_GROUPS = HEAD_DIM // I_LO
SCAN_SUB = 64


def _sublane_allsum(x):
    for shift in (4, 2, 1):
        x = x + pltpu.roll(x, shift, 0)
    return x


def _sublane_sums(xs):
    sub = lax.broadcasted_iota(jnp.int32, (SUBLANES, LANES), 0)

    def combine(x, y, d):
        clear = (sub & d) == 0
        return jnp.where(clear, x, pltpu.roll(y, d, 0)) + jnp.where(clear, pltpu.roll(x, SUBLANES - d, 0), y)

    z = [combine(xs[0], xs[4], 4), combine(xs[2], xs[6], 4), combine(xs[1], xs[5], 4), combine(xs[3], xs[7], 4)]
    return combine(combine(z[0], z[1], 2), combine(z[2], z[3], 2), 1)


def _cumprod_leading(x):
    n, shift = x.shape[0], 1
    while shift < n:
        x = jnp.concatenate([x[:shift], x[shift:] * x[:-shift]], axis=0)
        shift *= 2
    return x


def _scan_kernel(a_ref, w_ref, b_ref, k_ref, r_ref, v_ref, s0_ref, o_ref, sT_ref, s_scr, sa_scr, src_scr, pend_scr,
                 *tiles, tb, sub, bb):
    ti = pl.program_id(1)
    n_sub = tb // sub

    @pl.when(ti == 0)
    def _():
        s_scr[...] = s0_ref[...]

    gather_idx = (lax.broadcasted_iota(jnp.int32, (SUBLANES, LANES), 0) * RWKV_HEADS
                  + lax.broadcasted_iota(jnp.int32, (SUBLANES, LANES), 1) // I_LO)

    def gather(src_row):
        return jnp.take_along_axis(jnp.broadcast_to(src_row, (SUBLANES, LANES)), gather_idx, axis=1)

    A, B, K, R = range(4)
    tile = lambda bi, qi, parity: tiles[2 * qi + parity].at[bi]

    def prepare(bi):
        p_end = None
        for n in range(n_sub):
            rows = slice(n * sub, (n + 1) * sub)
            p = _cumprod_leading(w_ref[bi, rows])
            p_inv = 1.0 / p
            first = jnp.ones_like(p[:1]) if p_end is None else p_end[None]
            p_prev = jnp.concatenate([first, p[:-1]], axis=0)
            src_scr[bi, A, rows] = a_ref[bi, rows] * p_prev
            src_scr[bi, B, rows] = b_ref[bi, rows] * p_inv
            src_scr[bi, K, rows] = k_ref[bi, rows] * p_inv
            src_scr[bi, R, rows] = r_ref[bi, rows] * p
            p_end = p[sub - 1]
            for jg in range(J_GROUPS):
                pend_scr[bi, n, jg] = gather(p_end[jg:jg + 1, :])

    def build(bi, qi, t, parity):
        src_t = jnp.minimum(t, tb - 1)
        for jg in range(J_GROUPS):
            tile(bi, qi, parity)[jg] = gather(src_scr[bi, qi, src_t, pl.ds(jg, 1), :])

    def first_sa(bi):
        for qi in range(4):
            build(bi, qi, 0, 0)
        build(bi, A, 1, 1)
        sa0 = [None] * I_GROUPS
        for jg in range(J_GROUPS):
            a0 = tile(bi, A, 0)[jg]
            for ig in range(I_GROUPS):
                term = s_scr[bi, jg * I_GROUPS + ig] * a0
                sa0[ig] = term if sa0[ig] is None else sa0[ig] + term
        return [_sublane_allsum(x) for x in sa0]

    def step(bi, t, parity, sa, rescale=None):
        for qi in (B, K, R):
            build(bi, qi, t + 1, 1 - parity)
        build(bi, A, t + 2, parity)
        vb = [jnp.broadcast_to(v_ref[bi, t, pl.ds(ig, 1), :], (SUBLANES, LANES)) for ig in range(I_GROUPS)]
        out = [None] * I_GROUPS
        nxt = [None] * I_GROUPS
        for jg in range(J_GROUPS):
            b, k, r = (tile(bi, qi, parity)[jg] for qi in (B, K, R))
            a_next = tile(bi, A, 1 - parity)[jg]
            for ig in range(I_GROUPS):
                idx = jg * I_GROUPS + ig
                s_old = s_scr[bi, idx]
                if rescale is not None:
                    s_old = s_old * pend_scr[bi, rescale, jg]
                sj = s_old + sa[ig] * b + vb[ig] * k
                s_scr[bi, idx] = sj
                to, tn = sj * r, sj * a_next
                out[ig] = to if out[ig] is None else out[ig] + to
                nxt[ig] = tn if nxt[ig] is None else nxt[ig] + tn
        o_ref[bi, t] = _sublane_sums(out)
        sa_scr[bi, parity] = _sublane_sums(nxt)
        return [jnp.broadcast_to(sa_scr[bi, parity, pl.ds(ig, 1), :], (SUBLANES, LANES)) for ig in range(I_GROUPS)]

    def finish(bi):
        for jg in range(J_GROUPS):
            for ig in range(I_GROUPS):
                idx = jg * I_GROUPS + ig
                s_scr[bi, idx] = s_scr[bi, idx] * pend_scr[bi, n_sub - 1, jg]

    for bi in range(bb):
        prepare(bi)
    if tb <= SUBLANES:
        sas = [first_sa(bi) for bi in range(bb)]
        for t in range(tb):
            sas = [step(bi, t, t % 2, sas[bi]) for bi in range(bb)]
    else:
        for bi in range(bb):
            sa = first_sa(bi)
            for n in range(n_sub):
                t0 = n * sub
                sa = step(bi, t0, 0, sa, rescale=n - 1 if n else None)
                sa = step(bi, t0 + 1, 1, sa)

                def two_steps(u, sa, bi=bi):
                    return step(bi, 2 * u + 1, 1, step(bi, 2 * u, 0, sa))

                sa = lax.fori_loop(t0 // 2 + 1, (t0 + sub) // 2, two_steps, sa)
    for bi in range(bb):
        finish(bi)

    @pl.when(ti == pl.num_programs(1) - 1)
    def _():
        sT_ref[...] = s_scr[...]


def _rwkv_scan(a, w, b, k, r, v, s0, *, tb, sub, bb):
    bsz, t_len = a.shape[0], a.shape[1]
    assert t_len % tb == 0 and tb % sub == 0 and sub % 2 == 0 and bsz % bb == 0
    tok_spec = pl.BlockSpec((bb, tb, SUBLANES, LANES), lambda bi, ti: (bi, ti, 0, 0))
    st_spec = pl.BlockSpec((bb, HEAD_DIM, SUBLANES, LANES), lambda bi, ti: (bi, 0, 0, 0))
    return pl.pallas_call(
        functools.partial(_scan_kernel, tb=tb, sub=sub, bb=bb),
        grid=(bsz // bb, t_len // tb),
        in_specs=[tok_spec] * 6 + [st_spec],
        out_specs=[tok_spec, st_spec],
        out_shape=[jax.ShapeDtypeStruct(a.shape, F32), jax.ShapeDtypeStruct(s0.shape, F32)],
        scratch_shapes=([pltpu.VMEM((bb, HEAD_DIM, SUBLANES, LANES), F32), pltpu.VMEM((bb, 2, SUBLANES, LANES), F32),
                         pltpu.VMEM((bb, 4, tb, SUBLANES, LANES), F32),
                         pltpu.VMEM((bb, tb // sub, J_GROUPS, SUBLANES, LANES), F32)]
                        + [pltpu.VMEM((bb, J_GROUPS, SUBLANES, LANES), F32)] * 8),
        compiler_params=pltpu.CompilerParams(dimension_semantics=("parallel", "arbitrary"),
                                             vmem_limit_bytes=VMEM_LIMIT),
        name="rwkv_scan",
    )(a, w, b, k, r, v, s0)


PROJ_COLS = 5120
LORA_PAD = 384
COL_Q, COL_R, COL_K, COL_V, COL_SK, COL_SV, COL_L = 0, 1024, 2048, 3072, 4096, 4352, 4608
VMEM_LIMIT = 56 * 1024 * 1024


def _rms(x, g):
    return x * lax.rsqrt(jnp.mean(x * x, axis=-1, keepdims=True) + RMS_EPS) * g


def _inproj_kernel(x_ref, g_ref, w_ref, o_ref, xn_ref):
    @pl.when(pl.program_id(1) == 0)
    def _():
        xn_ref[...] = _rms(x_ref[...], g_ref[...]).astype(BF16)

    o_ref[...] = jnp.dot(xn_ref[...], w_ref[...], preferred_element_type=F32)


def _inproj(x, g, w, *, tm, tn):
    m, d = x.shape
    n = w.shape[1]
    return pl.pallas_call(
        _inproj_kernel,
        grid=(m // tm, n // tn),
        in_specs=[pl.BlockSpec((tm, d), lambda i, j: (i, 0)),
                  pl.BlockSpec((1, d), lambda i, j: (0, 0)),
                  pl.BlockSpec((d, tn), lambda i, j: (0, j))],
        out_specs=pl.BlockSpec((tm, tn), lambda i, j: (i, j)),
        out_shape=jax.ShapeDtypeStruct((m, n), F32),
        scratch_shapes=[pltpu.VMEM((tm, d), BF16)],
        compiler_params=pltpu.CompilerParams(dimension_semantics=("parallel", "arbitrary"),
                                             vmem_limit_bytes=VMEM_LIMIT),
        name="inproj",
    )(x, g, w)


def _tile_sum(x):
    parts = [x[:, c * LANES:(c + 1) * LANES] for c in range(RWKV_WIDTH // LANES)]
    while len(parts) > 1:
        parts = [parts[i] + parts[i + 1] for i in range(0, len(parts), 2)]
    return parts[0]


def _tile8(x):
    return jnp.concatenate([x] * (RWKV_WIDTH // LANES), axis=1)


def _group_sum(x, m):
    hi = x.astype(BF16)
    rest = x - hi.astype(F32)
    mid = rest.astype(BF16)
    lo = (rest - mid.astype(F32)).astype(BF16)
    dot = lambda a: jnp.dot(a, m, preferred_element_type=F32)
    return dot(hi) + dot(mid) + dot(lo)


def _lora_dot(x, w):
    return jnp.dot(x.astype(BF16), w, preferred_element_type=F32)


KV_MU_R, KV_MU_K, KV_W0, KV_A0, KV_KK, KV_KA, KV_RK = range(7)
VV_MU_V, VV_LN_W, VV_LN_B = range(3)


def _prep_kernel(pr_ref, pk_ref, pv_ref, pl_ref, fr_ref, fk_ref, fv_ref, fl_ref,
                 kvec_ref, vvec_ref, mul_ref, wa_ref, g2_ref, sumh_ref,
                 a_out, w_out, b_out, k_out, r_out, v_out, g_out,
                 cr_ref, ck_ref, cv_ref, cl_ref, *, period):
    krow = lambda i: kvec_ref[i:i + 1, :]

    def shifted(p_ref, fix_ref, carry_ref, mu):
        p = p_ref[...]
        rolled = pltpu.roll(p, 1, 0)
        rows = lax.broadcasted_iota(jnp.int32, p.shape, 0)
        if period is None:
            @pl.when(pl.program_id(0) == 0)
            def _():
                carry_ref[...] = fix_ref[...]

            prev = jnp.where(rows == 0, carry_ref[...], rolled)
            carry_ref[...] = p[p.shape[0] - 1:, :]
        else:
            prev = jnp.where(rows % period == 0, fix_ref[...], rolled)
        return p + mu * (prev - p)

    r_out[...] = shifted(pr_ref, fr_ref, cr_ref, krow(KV_MU_R))
    v_out[...] = shifted(pv_ref, fv_ref, cv_ref, vvec_ref[VV_MU_V:VV_MU_V + 1, :])
    k_out[...] = shifted(pk_ref, fk_ref, ck_ref, krow(KV_MU_K))
    xl = shifted(pl_ref, fl_ref, cl_ref, mul_ref[...])

    kk_all = k_out[...] * krow(KV_KK)
    inv_norm = 1.0 / jnp.maximum(jnp.sqrt(_group_sum(_tile_sum(kk_all * kk_all), sumh_ref[...])), 1e-12)

    x_wa = xl[:, :LANES]
    is_w = lax.broadcasted_iota(jnp.int32, x_wa.shape, 1) < DECAY_LORA
    lhs_w = jnp.where(is_w, jnp.tanh(x_wa), 0.0).astype(BF16)
    lhs_a = jnp.where(is_w, 0.0, x_wa).astype(BF16)
    lhs_g = jax.nn.sigmoid(xl[:, LANES:]).astype(BF16)
    half = RWKV_WIDTH // 2
    for c0 in (0, half):
        cols = slice(c0, c0 + half)
        dot = lambda lhs, w_ref: jnp.dot(lhs, w_ref[:, cols], preferred_element_type=F32)
        zw = kvec_ref[KV_W0:KV_W0 + 1, cols] + dot(lhs_w, wa_ref)
        za = kvec_ref[KV_A0:KV_A0 + 1, cols] + dot(lhs_a, wa_ref)
        g_out[:, cols] = dot(lhs_g, g2_ref)
        y = -zw
        softplus = jnp.maximum(y, 0.0) + jnp.log(1.0 + jnp.exp(-jnp.abs(y)))
        w_out[:, cols] = jnp.exp(-jnp.exp(-softplus - 0.5))
        a_sig = jax.nn.sigmoid(za)
        k = k_out[:, cols]
        kk = k * kvec_ref[KV_KK:KV_KK + 1, cols] * jnp.concatenate([inv_norm] * (half // LANES), axis=1)
        a_out[:, cols] = -kk
        b_out[:, cols] = kk * a_sig
        k_out[:, cols] = k * (1.0 + (a_sig - 1.0) * kvec_ref[KV_KA:KV_KA + 1, cols])


def _prep(proj, fixes, params, *, tm, period):
    m = proj.shape[0]
    c = RWKV_WIDTH
    grid = (m // tm,)

    def col(width, idx):
        return pl.BlockSpec((tm, width), lambda i: (i, idx))

    def full(arr):
        return pl.BlockSpec(arr.shape, lambda i: (0,) * arr.ndim)

    if period is None:
        fix_specs = [full(f) for f in fixes]
    else:
        fix_specs = [pl.BlockSpec((tm, f.shape[1]), lambda i: (i, 0)) for f in fixes]
    out_spec = pl.BlockSpec((tm, c), lambda i: (i, 0))
    return pl.pallas_call(
        functools.partial(_prep_kernel, period=period),
        grid=grid,
        in_specs=[col(c, COL_R // c), col(c, COL_K // c), col(c, COL_V // c), col(LORA_PAD, COL_L // LORA_PAD)]
        + fix_specs + [full(p) for p in params],
        out_specs=[out_spec] * 7,
        out_shape=[jax.ShapeDtypeStruct((m, c), F32)] * 7,
        scratch_shapes=[pltpu.VMEM((1, c), F32)] * 3 + [pltpu.VMEM((1, LORA_PAD), F32)],
        compiler_params=pltpu.CompilerParams(dimension_semantics=("arbitrary",), vmem_limit_bytes=VMEM_LIMIT),
        name="rwkv_prep",
    )(proj, proj, proj, proj, *fixes, *params)


def _rwkv_out(o, r, k, v, g, kvec_ref, vvec_ref, gsum_ref, kv_ref):
    inv_n = 1.0 / HEAD_DIM
    mean = _tile8(_group_sum(_tile_sum(o), gsum_ref[...])) * inv_n
    d = o - mean
    var = _tile8(_group_sum(_tile_sum(d * d), gsum_ref[...])) * inv_n
    normed = (d * lax.rsqrt(var + GN_EPS) * vvec_ref[VV_LN_W:VV_LN_W + 1, :]
              + vvec_ref[VV_LN_B:VV_LN_B + 1, :])
    rk = _tile8(_group_sum(_tile_sum(r * k * kvec_ref[KV_RK:KV_RK + 1, :]), kv_ref[...]))
    return ((normed + rk * v) * g).astype(BF16)


ALIBI_SLOPES = [2.0 ** (-8.0 * (i + 1) / SWA_HEADS) for i in range(SWA_HEADS)]
SCORE_SCALE = HEAD_DIM ** -0.5


LOG2E = 1.4426950408889634


def _attn_block_kernel(sink_ref, q_ref, kc_ref, vc_ref, kp_ref, vp_ref, o_ref, ot_scr, bias_scr, *, sub_blocks):
    n = pl.program_id(0)
    span = WINDOW + BLOCK

    @pl.when(n == 0)
    def _():
        si = lax.broadcasted_iota(jnp.int32, (span, BLOCK), 0)
        qi = lax.broadcasted_iota(jnp.int32, (span, BLOCK), 1)
        dist = qi - si + WINDOW
        valid = (dist >= 0) & (dist < WINDOW)
        distf = dist.astype(F32)
        for h in range(SWA_HEADS):
            bias = jnp.where(valid, (-ALIBI_SLOPES[h] * LOG2E) * distf, -jnp.inf)
            bias_scr[1, h] = bias
            bias_scr[0, h] = jnp.where(si >= WINDOW, bias, -jnp.inf)

    for sb in range(sub_blocks):
        rows = slice(sb * BLOCK, (sb + 1) * BLOCK)
        if sb == 0:
            k_prev, v_prev = kp_ref[...], vp_ref[...]
            table = jnp.where(n == 0, 0, 1)
        else:
            prev_rows = slice((sb - 1) * BLOCK, sb * BLOCK)
            k_prev, v_prev = kc_ref[prev_rows, :], vc_ref[prev_rows, :]
            table = 1
        k_span = jnp.concatenate([k_prev, kc_ref[rows, :]], axis=0)
        v_t = jnp.concatenate([v_prev, vc_ref[rows, :]], axis=0).T.astype(BF16)
        for kh in range(SWA_KV_HEADS):
            ks = k_span[:, kh * HEAD_DIM:(kh + 1) * HEAD_DIM].astype(BF16)
            heads = range(kh * SWA_GROUP, (kh + 1) * SWA_GROUP)
            qg = jnp.concatenate([q_ref[rows, h * HEAD_DIM:(h + 1) * HEAD_DIM] for h in heads], axis=0).astype(BF16)
            s_all = lax.dot_general(ks, qg, (((1,), (1,)), ((), ())), preferred_element_type=F32)
            probs, dens = [], []
            for gi, h in enumerate(heads):
                s = s_all[:, gi * BLOCK:(gi + 1) * BLOCK] * (SCORE_SCALE * LOG2E) + bias_scr[table, h]
                sink = sink_ref[h] * LOG2E
                m = jnp.maximum(jnp.max(s, axis=0, keepdims=True), sink)
                e = jnp.exp2(s - m)
                dens.append(jnp.sum(e, axis=0, keepdims=True) + jnp.exp2(sink - m))
                probs.append(e.astype(BF16))
            o_t = jnp.dot(v_t[kh * HEAD_DIM:(kh + 1) * HEAD_DIM, :], jnp.concatenate(probs, axis=1),
                          preferred_element_type=F32) / jnp.concatenate(dens, axis=1)
            for gi, h in enumerate(heads):
                ot_scr[sb, h * HEAD_DIM:(h + 1) * HEAD_DIM, :] = o_t[:, gi * BLOCK:(gi + 1) * BLOCK]
        o_ref[rows, :] = ot_scr[sb].T.astype(BF16)


def _attn_prompt(proj, sinks, *, sub_blocks):
    t = proj.shape[0]
    tq = sub_blocks * BLOCK
    cur = lambda width, idx: pl.BlockSpec((tq, width), lambda n: (n, idx))
    prev = lambda width, idx: pl.BlockSpec((BLOCK, width), lambda n: (jnp.maximum(n * sub_blocks - 1, 0), idx))
    return pl.pallas_call(
        functools.partial(_attn_block_kernel, sub_blocks=sub_blocks),
        grid=(t // tq,),
        in_specs=[pl.BlockSpec(memory_space=pltpu.SMEM),
                  cur(SWA_WIDTH, COL_Q // SWA_WIDTH), cur(KV_WIDTH, COL_SK // KV_WIDTH), cur(KV_WIDTH, COL_SV // KV_WIDTH),
                  prev(KV_WIDTH, COL_SK // KV_WIDTH), prev(KV_WIDTH, COL_SV // KV_WIDTH)],
        out_specs=pl.BlockSpec((tq, SWA_WIDTH), lambda n: (n, 0)),
        out_shape=jax.ShapeDtypeStruct((t, SWA_WIDTH), BF16),
        scratch_shapes=[pltpu.VMEM((sub_blocks, SWA_WIDTH, BLOCK), F32),
                        pltpu.VMEM((2, SWA_HEADS, WINDOW + BLOCK, BLOCK), F32)],
        compiler_params=pltpu.CompilerParams(dimension_semantics=("arbitrary",), vmem_limit_bytes=VMEM_LIMIT),
        name="swa_prompt",
    )(sinks, proj, proj, proj, proj, proj)


def _attn_decode_kernel(sink_ref, q_ref, kn_ref, vn_ref, kp_ref, vp_ref, o_ref, ko_ref, vo_ref, *, t_new, bt):
    span = WINDOW + SUBLANES
    rows = SWA_GROUP * SUBLANES
    row = lax.broadcasted_iota(jnp.int32, (bt, rows, span), 1)
    si = lax.broadcasted_iota(jnp.int32, (bt, rows, span), 2)
    dist = row % SUBLANES - si + WINDOW
    valid = (dist >= 0) & (dist < WINDOW)
    distf = dist.astype(F32)
    grp = row // SUBLANES
    grp_col = grp[:, :, :1]

    def per_group(tile_grp, vals):
        out = vals[-1]
        for gi in range(SWA_GROUP - 2, -1, -1):
            out = jnp.where(tile_grp == gi, vals[gi], out)
        return out

    pad = jnp.zeros((bt, SUBLANES - t_new, HEAD_DIM), F32)
    for kh in range(SWA_KV_HEADS):
        lanes = slice(kh * HEAD_DIM, (kh + 1) * HEAD_DIM)
        heads = range(kh * SWA_GROUP, (kh + 1) * SWA_GROUP)
        ks = jnp.concatenate([kp_ref[:, :, lanes], kn_ref[:, :, lanes], pad], axis=1).astype(BF16)
        vs = jnp.concatenate([vp_ref[:, :, lanes], vn_ref[:, :, lanes], pad], axis=1).astype(BF16)
        qg = jnp.concatenate([piece for h in heads for piece in (q_ref[:, :, h * HEAD_DIM:(h + 1) * HEAD_DIM], pad)],
                             axis=1).astype(BF16)
        s = jnp.einsum("bqd,bsd->bqs", qg, ks, preferred_element_type=F32) * SCORE_SCALE
        slope = per_group(grp, [ALIBI_SLOPES[h] for h in heads])
        s = jnp.where(valid, s - slope * distf, -jnp.inf)
        sink = per_group(grp_col, [sink_ref[h] for h in heads])
        m = jnp.maximum(jnp.max(s, axis=-1, keepdims=True), sink)
        e = jnp.exp(s - m)
        den = jnp.sum(e, axis=-1, keepdims=True) + jnp.exp(sink - m)
        og = jnp.einsum("bqs,bsd->bqd", (e / den).astype(BF16), vs, preferred_element_type=F32)
        for gi, h in enumerate(heads):
            o_ref[:, :, h * HEAD_DIM:(h + 1) * HEAD_DIM] = og[:, gi * SUBLANES:gi * SUBLANES + t_new, :].astype(BF16)
    ko_ref[:, :WINDOW - t_new, :] = kp_ref[:, t_new:, :]
    ko_ref[:, WINDOW - t_new:, :] = kn_ref[...]
    vo_ref[:, :WINDOW - t_new, :] = vp_ref[:, t_new:, :]
    vo_ref[:, WINDOW - t_new:, :] = vn_ref[...]


def _attn_decode(proj3, k_past, v_past, sinks, *, bt):
    bsz, t_new = proj3.shape[0], proj3.shape[1]
    new = lambda width, idx: pl.BlockSpec((bt, t_new, width), lambda b: (b, 0, idx))
    past = pl.BlockSpec((bt, WINDOW, KV_WIDTH), lambda b: (b, 0, 0))
    return pl.pallas_call(
        functools.partial(_attn_decode_kernel, t_new=t_new, bt=bt),
        grid=(bsz // bt,),
        in_specs=[pl.BlockSpec(memory_space=pltpu.SMEM),
                  new(SWA_WIDTH, COL_Q // SWA_WIDTH), new(KV_WIDTH, COL_SK // KV_WIDTH), new(KV_WIDTH, COL_SV // KV_WIDTH),
                  past, past],
        out_specs=[pl.BlockSpec((bt, t_new, SWA_WIDTH), lambda b: (b, 0, 0)), past, past],
        out_shape=[jax.ShapeDtypeStruct((bsz, t_new, SWA_WIDTH), BF16),
                   jax.ShapeDtypeStruct(k_past.shape, F32), jax.ShapeDtypeStruct(v_past.shape, F32)],
        compiler_params=pltpu.CompilerParams(dimension_semantics=("parallel",), vmem_limit_bytes=VMEM_LIMIT),
        name="swa_decode",
    )(sinks, proj3, proj3, proj3, k_past, v_past)


def _outproj_kernel(x_ref, o_ref, r_ref, k_ref, v_ref, g_ref, osw_ref, kvec_ref, vvec_ref, gsum_ref, kv_ref,
                    wr_ref, ws_ref, gn_ref, x1_ref, h2_ref):
    o_rwkv = _rwkv_out(o_ref[...], r_ref[...], k_ref[...], v_ref[...], g_ref[...], kvec_ref, vvec_ref, gsum_ref, kv_ref)
    x1 = (x_ref[...] + jnp.dot(o_rwkv, wr_ref[...], preferred_element_type=F32)
          + jnp.dot(osw_ref[...], ws_ref[...], preferred_element_type=F32))
    x1_ref[...] = x1
    h2_ref[...] = _rms(x1, gn_ref[...]).astype(BF16)


def _outproj(x, o, r, k, v, g, o_swa, post_params, w_r, w_s, g_norm, *, tm):
    m, d = x.shape
    row = lambda width: pl.BlockSpec((tm, width), lambda i: (i, 0))
    full = lambda arr: pl.BlockSpec(arr.shape, lambda i: (0, 0))
    consts = (*post_params, w_r, w_s, g_norm)
    return pl.pallas_call(
        _outproj_kernel,
        grid=(m // tm,),
        in_specs=[row(d)] + [row(RWKV_WIDTH)] * 5 + [row(SWA_WIDTH)] + [full(c) for c in consts],
        out_specs=[row(d), row(d)],
        out_shape=[jax.ShapeDtypeStruct((m, d), F32), jax.ShapeDtypeStruct((m, d), BF16)],
        compiler_params=pltpu.CompilerParams(dimension_semantics=("parallel",), vmem_limit_bytes=VMEM_LIMIT),
        name="outproj",
    )(x, o, r, k, v, g, o_swa, *consts)


def _ffn_kernel(h_ref, x1_ref, wg_ref, wu_ref, wd_ref, gf_ref, y_ref, acc_ref):
    j = pl.program_id(1)

    @pl.when(j == 0)
    def _():
        acc_ref[...] = jnp.zeros_like(acc_ref)

    h = h_ref[...]
    gate = jnp.dot(h, wg_ref[...], preferred_element_type=F32)
    up = jnp.dot(h, wu_ref[...], preferred_element_type=F32)
    act = (gate * jax.nn.sigmoid(gate) * up).astype(BF16)
    acc_ref[...] += jnp.dot(act, wd_ref[...], preferred_element_type=F32)

    @pl.when(j == pl.num_programs(1) - 1)
    def _():
        y_ref[...] = _rms(x1_ref[...] + acc_ref[...], gf_ref[...])


def _ffn(h2, x1, w_gate, w_up, w_down, g_final, *, tm, tf):
    m, d = x1.shape
    f = w_gate.shape[1]
    return pl.pallas_call(
        _ffn_kernel,
        grid=(m // tm, f // tf),
        in_specs=[pl.BlockSpec((tm, d), lambda i, j: (i, 0)),
                  pl.BlockSpec((tm, d), lambda i, j: (i, 0)),
                  pl.BlockSpec((d, tf), lambda i, j: (0, j)),
                  pl.BlockSpec((d, tf), lambda i, j: (0, j)),
                  pl.BlockSpec((tf, d), lambda i, j: (j, 0)),
                  pl.BlockSpec((1, d), lambda i, j: (0, 0))],
        out_specs=pl.BlockSpec((tm, d), lambda i, j: (i, 0)),
        out_shape=jax.ShapeDtypeStruct((m, d), F32),
        scratch_shapes=[pltpu.VMEM((tm, d), F32)],
        compiler_params=pltpu.CompilerParams(dimension_semantics=("parallel", "arbitrary"),
                                             vmem_limit_bytes=VMEM_LIMIT),
        name="ffn",
    )(h2, x1, w_gate, w_up, w_down, g_final)


def _group_matrices():
    lane = np.arange(LANES)
    key_head = lane % RWKV_HEADS
    val_head = lane // I_LO
    sum_key = (key_head[:, None] == key_head[None, :]).astype(np.float32)
    sum_val = (val_head[:, None] == val_head[None, :]).astype(np.float32)
    key_to_val = (key_head[:, None] == val_head[None, :]).astype(np.float32)
    return sum_key, sum_val, key_to_val


def _state_to_kernel(s):
    b = s.shape[0]
    s = s.reshape(b, RWKV_HEADS, I_GROUPS, I_LO, J_GROUPS, SUBLANES)
    return s.transpose(0, 4, 2, 5, 1, 3).reshape(b, HEAD_DIM, SUBLANES, LANES)


def _state_from_kernel(s):
    b = s.shape[0]
    s = s.reshape(b, J_GROUPS, I_GROUPS, SUBLANES, RWKV_HEADS, I_LO)
    return s.transpose(0, 4, 2, 5, 1, 3).reshape(b, RWKV_HEADS, HEAD_DIM, HEAD_DIM)


def _shift_to_cols(shift):
    c = RWKV_WIDTH
    return (_key_order(shift[:, 0:c]), _key_order(shift[:, c:2 * c]), _val_order(shift[:, 2 * c:3 * c]),
            jnp.pad(shift[:, 3 * c:], ((0, 0), (0, LORA_PAD - LORA_COLS))))


def _shift_from_proj(rows):
    return jnp.concatenate([_key_natural(rows[:, COL_R:COL_K]), _key_natural(rows[:, COL_K:COL_V]),
                            _val_natural(rows[:, COL_V:COL_SK]), rows[:, COL_L:COL_L + LORA_COLS]], axis=1)


def _layer_weights(norm_attn, w_in, mu_shift, w0, w2, a0, a2, g2, k_k, k_a, r_k, ln_x_w, ln_x_b, sinks, w_out,
                   norm_ffn, w_gate, w_up, w_down, norm_final):
    c = RWKV_WIDTH
    swa0 = RWKV_COLS
    w_proj = jnp.concatenate([
        w_in[:, swa0:swa0 + SWA_WIDTH],
        _key_order(w_in[:, 0:c]), _key_order(w_in[:, c:2 * c]), _val_order(w_in[:, 2 * c:3 * c]),
        w_in[:, swa0 + SWA_WIDTH:swa0 + SWA_WIDTH + 2 * KV_WIDTH],
        jnp.pad(w_in[:, 3 * c:RWKV_COLS], ((0, 0), (0, PROJ_COLS - COL_L - LORA_COLS))),
    ], axis=1).astype(BF16)
    row = lambda v: v.reshape(1, -1)
    zero = jnp.zeros((c,), F32)
    kvec = _key_order(jnp.stack([mu_shift[0:c], mu_shift[c:2 * c], w0, a0, k_k, k_a, r_k.reshape(-1), zero]))
    vvec = _val_order(jnp.stack([mu_shift[2 * c:3 * c], ln_x_w, ln_x_b] + [zero] * 5))
    mu_l = jnp.pad(mu_shift[3 * c:], (0, LORA_PAD - LORA_COLS)).reshape(1, -1)
    lora_wa = _key_order(jnp.concatenate([w2, a2], axis=0)).astype(BF16)
    g2p = jnp.pad(_val_order(g2), ((0, LORA_PAD - LANES - GATE_LORA), (0, 0))).astype(BF16)
    sum_key, sum_val, key_to_val = _group_matrices()
    prep_params = (kvec, vvec, mu_l, lora_wa, g2p, jnp.asarray(sum_key, BF16))
    post_params = (kvec, vvec, jnp.asarray(sum_val, BF16), jnp.asarray(key_to_val, BF16))
    w_out_r = w_out[:c].reshape(RWKV_HEADS, I_GROUPS, I_LO, -1).swapaxes(0, 1).reshape(c, -1)
    return dict(
        norm_attn=row(norm_attn), w_proj=w_proj, prep=prep_params, post=post_params, sinks=sinks,
        w_out_r=w_out_r.astype(BF16), w_out_s=w_out[c:].astype(BF16), norm_ffn=row(norm_ffn),
        w_gate=w_gate.astype(BF16), w_up=w_up.astype(BF16), w_down=w_down.astype(BF16), norm_final=row(norm_final))


def _layer(x, shift_prev, state0, kv_past, lw, *, tiles):
    bsz, t_len, d = x.shape
    m = bsz * t_len
    x2 = x.reshape(m, d)
    proj = _inproj(x2, lw["norm_attn"], lw["w_proj"], tm=tiles["tm_in"], tn=tiles["tn_in"])

    fix_cols = _shift_to_cols(shift_prev)
    if bsz == 1:
        fixes, period = fix_cols, None
    else:
        fixes = tuple(jnp.pad(f[:, None, :], ((0, 0), (0, t_len - 1), (0, 0))).reshape(m, -1) for f in fix_cols)
        period = t_len
    a, w, b, k, r, v, g = _prep(proj, fixes, lw["prep"], tm=tiles["tm_prep"], period=period)

    tok = lambda z: z.reshape(bsz, t_len, SUBLANES, LANES)
    o, s_fin = _rwkv_scan(tok(a), tok(w), tok(b), tok(k), tok(r), tok(v), _state_to_kernel(state0), tb=tiles["tb"],
                          sub=min(SCAN_SUB, t_len), bb=min(tiles["bb_scan"], bsz))

    kv_shape = (bsz, WINDOW, SWA_KV_HEADS, HEAD_DIM)
    if kv_past is None:
        o_swa = _attn_prompt(proj, lw["sinks"], sub_blocks=tiles["attn_sub"])
        new_k = proj[m - WINDOW:, COL_SK:COL_SV].reshape(kv_shape)
        new_v = proj[m - WINDOW:, COL_SV:COL_L].reshape(kv_shape)
    else:
        k_past, v_past = (z.reshape(bsz, WINDOW, KV_WIDTH) for z in kv_past)
        o_swa, new_k, new_v = _attn_decode(proj.reshape(bsz, t_len, PROJ_COLS), k_past, v_past, lw["sinks"],
                                           bt=min(tiles["bt_attn"], bsz))
        o_swa, new_k, new_v = o_swa.reshape(m, SWA_WIDTH), new_k.reshape(kv_shape), new_v.reshape(kv_shape)

    x1, h2 = _outproj(x2, o.reshape(m, RWKV_WIDTH), r, k, v, g, o_swa, lw["post"], lw["w_out_r"], lw["w_out_s"],
                      lw["norm_ffn"], tm=tiles["tm_out"])
    y = _ffn(h2, x1, lw["w_gate"], lw["w_up"], lw["w_down"], lw["norm_final"], tm=tiles["tm_ffn"], tf=tiles["tf"])

    new_shift = _shift_from_proj(proj.reshape(bsz, t_len, PROJ_COLS)[:, -1])
    return y.reshape(bsz, t_len, d), _state_from_kernel(s_fin), new_shift, new_k, new_v


def _tiles(m, t_len):
    return dict(tm_in=min(1024, m), tn_in=1280, tm_prep=min(512, m), tb=min(256, t_len),
                tm_out=min(256, m), tm_ffn=min(512, m), tf=512, bt_attn=16, attn_sub=8, bb_scan=8)


def kernel(x_prompt, x_sample, state_rwkv, state_shift, cache_swa_k, cache_swa_v, norm_attn, w_in, mu_shift, w0, w2, a0, a2, g2, k_k, k_a, r_k, ln_x_w, ln_x_b, sinks, w_out, norm_ffn, w_gate, w_up, w_down, norm_final):
    assert norm_attn.shape[0] == 1, "single trunk layer"
    lw = _layer_weights(norm_attn[0], w_in[0], mu_shift[0], w0[0], w2[0], a0[0], a2[0], g2[0], k_k[0], k_a[0],
                        r_k[0], ln_x_w[0], ln_x_b[0], sinks[0], w_out[0], norm_ffn[0], w_gate[0], w_up[0],
                        w_down[0], norm_final)
    bp, tp, _ = x_prompt.shape
    bs, ts, _ = x_sample.shape
    assert bp == 1 and tp % BLOCK == 0 and ts < SUBLANES
    yp, p_state, p_shift, p_k, p_v = _layer(
        x_prompt, jnp.zeros((bp, RWKV_COLS), F32), jnp.zeros((bp, RWKV_HEADS, HEAD_DIM, HEAD_DIM), F32),
        None, lw, tiles=_tiles(bp * tp, tp))
    ys, s_state, s_shift, s_k, s_v = _layer(
        x_sample, state_shift[0], state_rwkv[0], (cache_swa_k[0], cache_swa_v[0]), lw, tiles=_tiles(bs * ts, ts))
    return (yp, ys, p_state[None], p_shift[None], p_k[None], p_v[None],
            s_state[None], s_shift[None], s_k[None], s_v[None])
```
